```python
import math
import jax, jax.numpy as jnp
from jax import lax
import numpy as np

D_MODEL = 2048
BATCH = 4
SEQ = 2048
DEPTH = 2

A_HEADS = 16
A_KV_HEADS = 4
A_HEAD_DIM = 128
IDX_HEADS = 16
IDX_DIM = 64
TOPK_MAX = 256
B_DILATIONS = ((128, 1), (512, 4), (2048, 16))
N_GROUPS = 3
B_HEADS = 16
B_HEAD_DIM = 64
N_BUCKETS = 32
MAX_DISTANCE = 2048
BIAS_HEADS = 16
D_FF = 5632
N_EXPERTS = 8
TOP_K_EXPERTS = 2
D_FF_EXPERT = 7168
QBLOCK = 128
EPS = 1e-6

N_A = DEPTH // 2
N_B = DEPTH - N_A
N_DENSE = (DEPTH + 1) // 2
N_MOE = DEPTH // 2
A_Q = A_HEADS * A_HEAD_DIM
A_KV = A_KV_HEADS * A_HEAD_DIM
A_QI = IDX_HEADS * IDX_DIM
A_IN = A_Q + 2 * A_KV + A_QI + IDX_DIM + IDX_HEADS
B_Q = N_GROUPS * B_HEADS * B_HEAD_DIM
B_OUT = B_HEADS * B_HEAD_DIM

kernel_name = "yoco_dsa_dilated_moe_trunk"


def rmsnorm(x, g):
    xf = x.astype(jnp.float32)
    y = xf * lax.rsqrt(jnp.mean(xf * xf, axis=-1, keepdims=True) + EPS)
    return (y * g.astype(jnp.float32)).astype(x.dtype)


def modulate(h, shift, scale):
    return h * (1 + scale[:, None, :]) + shift[:, None, :]


def t5_bucket(rel):
    n = jnp.maximum(rel, 0)
    max_exact = N_BUCKETS // 2
    nf = jnp.maximum(n, 1).astype(jnp.float32)
    large = max_exact + (jnp.log(nf / max_exact) / math.log(MAX_DISTANCE / max_exact)
                         * (N_BUCKETS - max_exact)).astype(jnp.int32)
    large = jnp.minimum(large, N_BUCKETS - 1)
    return jnp.where(n < max_exact, n, large)


def swiglu(h, w1, w3, w2):
    return (jax.nn.silu(h @ w1) * (h @ w3)) @ w2


def dsa_attention(h, pos, w_in, w_out, g_qn, g_kn, rel_bias):
    b_, s_len, _ = h.shape
    proj = h @ w_in
    cuts = [A_Q, A_Q + A_KV, A_Q + 2 * A_KV, A_Q + 2 * A_KV + A_QI, A_Q + 2 * A_KV + A_QI + IDX_DIM]
    q, k, v, qi, ki, wi = jnp.split(proj, cuts, axis=-1)
    q = rmsnorm(q.reshape(b_, s_len, A_HEADS, A_HEAD_DIM), g_qn)
    k = rmsnorm(k.reshape(b_, s_len, A_KV_HEADS, A_HEAD_DIM), g_kn)
    v = v.reshape(b_, s_len, A_KV_HEADS, A_HEAD_DIM)
    qi = qi.reshape(b_, s_len, IDX_HEADS, IDX_DIM)
    wi = wi * (IDX_HEADS ** -0.5)
    topk = min(TOPK_MAX, s_len // 4)
    nq = s_len // QBLOCK
    grp = A_HEADS // A_KV_HEADS
    scale = A_HEAD_DIM ** -0.5

    def to_blocks(a):
        return jnp.moveaxis(a.reshape((b_, nq, QBLOCK) + a.shape[2:]), 1, 0)

    def block(args):
        bi, qb, qib, wib, qpos = args
        t = bi * QBLOCK + jnp.arange(QBLOCK)
        causal = jnp.arange(s_len)[None, :] <= t[:, None]
        isc = jnp.einsum('bqhd,bsd->bqhs', qib, ki, preferred_element_type=jnp.float32)
        isc = jnp.einsum('bqhs,bqh->bqs', jax.nn.relu(isc) * (IDX_DIM ** -0.5),
                         wib.astype(jnp.float32))
        isc = jnp.where(causal[None], isc, -jnp.inf)
        _, sel = lax.top_k(isc, topk)
        valid = sel <= t[None, :, None]
        flat = sel.reshape(b_, QBLOCK * topk)
        kg = jnp.take_along_axis(k, flat[:, :, None, None], axis=1).reshape(
            b_, QBLOCK, topk, A_KV_HEADS, A_HEAD_DIM)
        vg = jnp.take_along_axis(v, flat[:, :, None, None], axis=1).reshape(
            b_, QBLOCK, topk, A_KV_HEADS, A_HEAD_DIM)
        kpos = jnp.take_along_axis(pos, flat, axis=1).reshape(b_, QBLOCK, topk)
        bias = rel_bias[t5_bucket(qpos[:, :, None] - kpos)]
        bias = jnp.transpose(bias.reshape(b_, QBLOCK, topk, A_KV_HEADS, grp), (0, 1, 3, 4, 2))
        qg = qb.reshape(b_, QBLOCK, A_KV_HEADS, grp, A_HEAD_DIM)
        logits = jnp.einsum('bqkgd,bqjkd->bqkgj', qg, kg, preferred_element_type=jnp.float32) * scale + bias
        logits = jnp.where(valid[:, :, None, None, :], logits, -jnp.inf)
        p = jax.nn.softmax(logits, axis=-1)
        o = jnp.einsum('bqkgj,bqjkd->bqkgd', p.astype(vg.dtype), vg)
        return o.reshape(b_, QBLOCK, A_Q)

    outs = lax.map(block, (jnp.arange(nq), to_blocks(q), to_blocks(qi), to_blocks(wi), to_blocks(pos)))
    o = jnp.moveaxis(outs, 0, 1).reshape(b_, s_len, A_Q)
    return o @ w_out


def dilated_group(q, k, v, pos, window, r, rel_bias):
    b_, s_len, nh, hd = q.shape
    wk = window // r
    n = s_len // r
    bq = math.gcd(QBLOCK, n)
    nb = n // bq
    qs = q.reshape(b_, nb, bq, r, nh, hd)
    pad5 = ((0, 0), (wk, 0), (0, 0), (0, 0), (0, 0))
    kp = jnp.pad(k.reshape(b_, n, r, nh, hd), pad5)
    vp = jnp.pad(v.reshape(b_, n, r, nh, hd), pad5)
    pp = jnp.pad(pos.reshape(b_, n, r), ((0, 0), (wk, 0), (0, 0)))
    idx = np.arange(nb)[:, None] * bq + np.arange(bq + wk)[None, :]
    kb = kp[:, idx]
    vb = vp[:, idx]
    kpos = pp[:, idx]
    qpos = pos.reshape(b_, nb, bq, r)
    i = np.arange(bq)[:, None]
    j = np.arange(bq + wk)[None, :]
    band = (j >= i) & (j <= i + wk)
    start_ok = (np.arange(nb)[:, None, None] * bq + j[None] - wk) >= 0
    mask = jnp.asarray(band[None] & start_ok)
    logits = jnp.einsum('bnqrhd,bnkrhd->bnrhqk', qs, kb,
                        preferred_element_type=jnp.float32) * (hd ** -0.5)
    rel = qpos[:, :, :, None, :] - kpos[:, :, None, :, :]
    bias = rel_bias[t5_bucket(rel)]
    logits = logits + jnp.transpose(bias, (0, 1, 4, 5, 2, 3))
    logits = jnp.where(mask[None, :, None, None], logits, -jnp.inf)
    m = jnp.max(logits, axis=-1, keepdims=True)
    p = jnp.exp(logits - m)
    den = jnp.sum(p, axis=-1)
    num = jnp.einsum('bnrhqk,bnkrhd->bnrhqd', p.astype(vb.dtype), vb,
                     preferred_element_type=jnp.float32)
    num = jnp.transpose(num, (0, 1, 4, 2, 3, 5)).reshape(b_, s_len, nh, hd)
    den = jnp.transpose(den, (0, 1, 4, 2, 3)).reshape(b_, s_len, nh)
    m = jnp.transpose(m[..., 0], (0, 1, 4, 2, 3)).reshape(b_, s_len, nh)
    return num, den, m


def dilated_attention(h, pos, k_sh, v_sh, w_q, w_out, g_qn, rel_bias):
    b_, s_len, _ = h.shape
    q = rmsnorm((h @ w_q).reshape(b_, s_len, N_GROUPS, B_HEADS, B_HEAD_DIM), g_qn)
    nums, dens, ms = [], [], []
    for g, (window, r) in enumerate(B_DILATIONS):
        num, den, m = dilated_group(q[:, :, g], k_sh[:, :, g], v_sh[:, :, g], pos, window, r, rel_bias)
        nums.append(num); dens.append(den); ms.append(m)
    m_all = jnp.maximum(jnp.maximum(ms[0], ms[1]), ms[2])
    wts = [jnp.exp(mg - m_all) for mg in ms]
    num_tot = nums[0] * wts[0][..., None] + nums[1] * wts[1][..., None] + nums[2] * wts[2][..., None]
    den_tot = dens[0] * wts[0] + dens[1] * wts[1] + dens[2] * wts[2]
    o = (num_tot / den_tot[..., None]).astype(h.dtype).reshape(b_, s_len, B_OUT)
    return o @ w_out


def moe_swiglu(h, router, router_b, w1, w3, w2):
    logits = (h @ router).astype(jnp.float32) + router_b.astype(jnp.float32)
    topv, topi = lax.top_k(logits, TOP_K_EXPERTS)
    gates = jax.nn.softmax(topv, axis=-1)
    dense_gates = jnp.sum(jax.nn.one_hot(topi, N_EXPERTS, dtype=jnp.float32) * gates[..., None], axis=-2)
    out = jnp.zeros_like(h)
    for e in range(N_EXPERTS):
        out = out + dense_gates[..., e:e + 1].astype(h.dtype) * swiglu(h, w1[e], w3[e], w2[e])
    return out


def setup_inputs(seed: int = 0) -> dict:
    key = jax.random.key(seed)
    ks = jax.random.split(key, 32)

    def nrm(k, shape, scale):
        return jax.random.normal(k, shape, jnp.float32) * scale

    def gain(k, shape):
        return 1.0 + 0.02 * jax.random.normal(k, shape, jnp.float32)

    d = D_MODEL
    pos = (jnp.arange(SEQ, dtype=jnp.int32)[None, :]
           + jax.random.randint(ks[2], (BATCH, 1), 0, 4096, dtype=jnp.int32))
    return {
        "x": nrm(ks[0], (BATCH, SEQ, d), 1.0),
        "c": nrm(ks[1], (BATCH, d), 1.0),
        "positions": pos,
        "rel_bias": nrm(ks[3], (N_BUCKETS, BIAS_HEADS), 0.5),
        "w_mod": nrm(ks[4], (DEPTH, d, 6 * d), 0.5 * d ** -0.5),
        "b_mod": nrm(ks[5], (DEPTH, 6 * d), 0.01),
        "g_attn": gain(ks[6], (DEPTH, d)),
        "g_ffn": gain(ks[7], (DEPTH, d)),
        "a_w_in": nrm(ks[8], (N_A, d, A_IN), d ** -0.5),
        "a_w_out": nrm(ks[9], (N_A, A_Q, d), A_Q ** -0.5),
        "a_g_qn": gain(ks[10], (N_A, A_HEAD_DIM)),
        "a_g_kn": gain(ks[11], (N_A, A_HEAD_DIM)),
        "kv_w_mod": nrm(ks[12], (d, 2 * d), 0.5 * d ** -0.5),
        "kv_b_mod": nrm(ks[13], (2 * d,), 0.01),
        "kv_g": gain(ks[14], (d,)),
        "kv_w": nrm(ks[15], (d, 2 * B_Q), d ** -0.5),
        "b_g_kn": gain(ks[16], (B_HEAD_DIM,)),
        "b_w_q": nrm(ks[17], (N_B, d, B_Q), d ** -0.5),
        "b_w_out": nrm(ks[18], (N_B, B_OUT, d), B_OUT ** -0.5),
        "b_g_qn": gain(ks[19], (N_B, B_HEAD_DIM)),
        "ffn_w1": nrm(ks[20], (N_DENSE, d, D_FF), d ** -0.5),
        "ffn_w3": nrm(ks[21], (N_DENSE, d, D_FF), d ** -0.5),
        "ffn_w2": nrm(ks[22], (N_DENSE, D_FF, d), D_FF ** -0.5),
        "moe_router": nrm(ks[23], (N_MOE, d, N_EXPERTS), d ** -0.5),
        "moe_router_b": nrm(ks[24], (N_MOE, N_EXPERTS), 0.01),
        "moe_w1": nrm(ks[25], (N_MOE, N_EXPERTS, d, D_FF_EXPERT), d ** -0.5),
        "moe_w3": nrm(ks[26], (N_MOE, N_EXPERTS, d, D_FF_EXPERT), d ** -0.5),
        "moe_w2": nrm(ks[27], (N_MOE, N_EXPERTS, D_FF_EXPERT, d), D_FF_EXPERT ** -0.5),
    }


def reference(x, c, positions, rel_bias, w_mod, b_mod, g_attn, g_ffn, a_w_in, a_w_out, a_g_qn,
              a_g_kn, kv_w_mod, kv_b_mod, kv_g, kv_w, b_g_kn, b_w_q, b_w_out, b_g_qn, ffn_w1,
              ffn_w3, ffn_w2, moe_router, moe_router_b, moe_w1, moe_w3, moe_w2):
    b_, s_len, _ = x.shape
    cs = jax.nn.silu(c)
    k_sh = v_sh = None
    for l in range(DEPTH):
        mod = cs @ w_mod[l] + b_mod[l]
        sh1, sc1, gt1, sh2, sc2, gt2 = jnp.split(mod, 6, axis=-1)
        if l < N_A:
            h = modulate(rmsnorm(x, g_attn[l]), sh1, sc1)
            a = dsa_attention(h, positions, a_w_in[l], a_w_out[l], a_g_qn[l], a_g_kn[l], rel_bias)
        else:
            if l == N_A:
                kv_sh, kv_sc = jnp.split(cs @ kv_w_mod + kv_b_mod, 2, axis=-1)
                hkv = modulate(rmsnorm(x, kv_g), kv_sh, kv_sc)
                kv = (hkv @ kv_w).reshape(b_, s_len, 2, N_GROUPS, B_HEADS, B_HEAD_DIM)
                k_sh = rmsnorm(kv[:, :, 0], b_g_kn)
                v_sh = kv[:, :, 1]
            bi = l - N_A
            h = modulate(rmsnorm(x, g_attn[l]), sh1, sc1)
            a = dilated_attention(h, positions, k_sh, v_sh, b_w_q[bi], b_w_out[bi], b_g_qn[bi], rel_bias)
        x = x + gt1[:, None, :] * a
        h = modulate(rmsnorm(x, g_ffn[l]), sh2, sc2)
        if l % 2 == 0:
            f = swiglu(h, ffn_w1[l // 2], ffn_w3[l // 2], ffn_w2[l // 2])
        else:
            f = moe_swiglu(h, moe_router[l // 2], moe_router_b[l // 2], moe_w1[l // 2],
                           moe_w3[l // 2], moe_w2[l // 2])
        x = x + gt2[:, None, :] * f
    return x
```

```python
import functools
import math

import jax
import jax.numpy as jnp
from jax import lax
from jax.experimental import pallas as pl
from jax.experimental.pallas import tpu as pltpu

F32 = jnp.float32
BF16 = jnp.bfloat16
I32 = jnp.int32

EPS = 1e-6
NEG = -1e30
INT_MIN = -(2 ** 31)

A_HEADS, A_KV_HEADS, A_HEAD_DIM = 16, 4, 128
IDX_HEADS, IDX_DIM = 16, 64
TOPK_MAX = 256
B_DILATIONS = ((128, 1), (512, 4), (2048, 16))
B_HEADS, B_HEAD_DIM = 16, 64
N_BUCKETS, MAX_DISTANCE = 32, 2048
N_EXPERTS = 8
LANES = 128

VMEM_LIMIT_BYTES = 56 * 1024 * 1024

_NT = (((1,), (1,)), ((), ()))


def _params(*sem):
    return pltpu.CompilerParams(dimension_semantics=sem, vmem_limit_bytes=VMEM_LIMIT_BYTES)


def _dot(a, b):
    return jnp.dot(a, b, preferred_element_type=F32)


def _dot_nt(a, b):
    return lax.dot_general(a, b, _NT, preferred_element_type=F32)


def _sigmoid(x):
    return 1.0 / (1.0 + jnp.exp(-x))


def _t5_bucket(rel):
    n = jnp.maximum(rel, 0)
    max_exact = N_BUCKETS // 2
    nf = jnp.maximum(n, 1).astype(F32)
    large = max_exact + (jnp.log(nf / max_exact) / math.log(MAX_DISTANCE / max_exact)
                         * (N_BUCKETS - max_exact)).astype(I32)
    large = jnp.minimum(large, N_BUCKETS - 1)
    return jnp.where(n < max_exact, n, large)


def _norm_mod(x, g, sc, sh):
    y = x * lax.rsqrt(jnp.mean(x * x, axis=-1, keepdims=True) + EPS)
    return (y * g) * (1.0 + sc) + sh


def _bias_lookup(tab_row, bkt, rows):
    tab = jnp.broadcast_to(tab_row, (rows, LANES))
    parts = [jnp.take_along_axis(tab, bkt[:, c * LANES:(c + 1) * LANES], axis=1)
             for c in range(bkt.shape[1] // LANES)]
    return parts[0] if len(parts) == 1 else jnp.concatenate(parts, axis=1)


def _mod_body(c_ref, w_ref, b_ref, o_ref):
    c = c_ref[...]
    cs = c * _sigmoid(c)
    o_ref[...] = _dot(cs.astype(BF16), w_ref[...].astype(BF16)) + b_ref[...]


def _mod_call(c8, w3, layer, b2):
    _, d, n = w3.shape
    tn = 1024
    return pl.pallas_call(
        _mod_body,
        grid=(n // tn,),
        in_specs=[pl.BlockSpec((8, d), lambda j: (0, 0)),
                  pl.BlockSpec((None, d, tn), lambda j: (layer, 0, j)),
                  pl.BlockSpec((None, 1, tn), lambda j: (layer, 0, j))],
        out_specs=pl.BlockSpec((8, tn), lambda j: (0, j)),
        out_shape=jax.ShapeDtypeStruct((8, n), F32),
        compiler_params=_params("arbitrary"),
        name="adaln_mod",
    )(c8, w3, b2.reshape(b2.shape[0], 1, n))


def _nm_body(x_ref, g_ref, sc_ref, sh_ref, w_ref, *rest, has_tail):
    if has_tail:
        wt_ref, o_ref, ot_ref, h_scr = rest
    else:
        o_ref, h_scr = rest

    @pl.when(pl.program_id(1) == 0)
    def _():
        h = _norm_mod(x_ref[...], g_ref[...], sc_ref[0], sh_ref[0]).astype(BF16)
        h_scr[...] = h
        if has_tail:
            ot_ref[...] = _dot(h, wt_ref[...].astype(BF16)).astype(ot_ref.dtype)

    o_ref[...] = _dot(h_scr[...], w_ref[...].astype(BF16)).astype(o_ref.dtype)


def _nm_matmul(x, g, sc, sh, w3, layer, n_cols, seq, w_tail=None, tm=1024, tn=512):
    m, d = x.shape
    nb = seq // tm
    has_tail = w_tail is not None
    in_specs = [pl.BlockSpec((tm, d), lambda i, j: (i, 0)),
                pl.BlockSpec((1, d), lambda i, j: (0, 0)),
                pl.BlockSpec((1, 1, d), lambda i, j: (i // nb, 0, 0)),
                pl.BlockSpec((1, 1, d), lambda i, j: (i // nb, 0, 0)),
                pl.BlockSpec((None, d, tn), lambda i, j: (layer, 0, j))]
    args = [x, g.reshape(1, d), sc.reshape(-1, 1, d), sh.reshape(-1, 1, d), w3]
    out_specs = [pl.BlockSpec((tm, tn), lambda i, j: (i, j))]
    out_shape = [jax.ShapeDtypeStruct((m, n_cols), BF16)]
    if has_tail:
        in_specs.append(pl.BlockSpec((d, LANES), lambda i, j: (0, 0)))
        args.append(w_tail)
        out_specs.append(pl.BlockSpec((tm, LANES), lambda i, j: (i, 0)))
        out_shape.append(jax.ShapeDtypeStruct((m, LANES), BF16))
    res = pl.pallas_call(
        functools.partial(_nm_body, has_tail=has_tail),
        grid=(m // tm, n_cols // tn),
        in_specs=in_specs,
        out_specs=out_specs,
        out_shape=out_shape,
        scratch_shapes=[pltpu.VMEM((tm, d), BF16)],
        compiler_params=_params("arbitrary", "arbitrary"),
        name="norm_mod_matmul",
    )(*args)
    return res if has_tail else res[0]


def _mmres_body(a_ref, w_ref, x_ref, gt_ref, o_ref):
    o_ref[...] = x_ref[...] + gt_ref[0] * _dot(a_ref[...], w_ref[...].astype(BF16))


def _matmul_residual(a, w3, layer, x, gt, seq, tm=1024, tn=256):
    m, k = a.shape
    d = x.shape[1]
    nb = seq // tm
    return pl.pallas_call(
        _mmres_body,
        grid=(m // tm, d // tn),
        in_specs=[pl.BlockSpec((tm, k), lambda i, j: (i, 0)),
                  pl.BlockSpec((None, k, tn), lambda i, j: (layer, 0, j)),
                  pl.BlockSpec((tm, tn), lambda i, j: (i, j)),
                  pl.BlockSpec((1, 1, tn), lambda i, j: (i // nb, 0, j))],
        out_specs=pl.BlockSpec((tm, tn), lambda i, j: (i, j)),
        out_shape=jax.ShapeDtypeStruct((m, d), F32),
        compiler_params=_params("arbitrary", "arbitrary"),
        name="matmul_residual",
    )(a, w3, x, gt.reshape(-1, 1, d))


def _nm_swiglu_body(x_ref, g_ref, sc_ref, sh_ref, w1_ref, w3_ref, o_ref, h_scr):
    @pl.when(pl.program_id(1) == 0)
    def _():
        h_scr[...] = _norm_mod(x_ref[...], g_ref[...], sc_ref[0], sh_ref[0]).astype(BF16)

    h = h_scr[...]
    a = _dot(h, w1_ref[...].astype(BF16))
    b = _dot(h, w3_ref[...].astype(BF16))
    o_ref[...] = (a * _sigmoid(a) * b).astype(o_ref.dtype)


def _nm_swiglu(x, g, sc, sh, w1, w3, layer, seq, tm=1024, tf=256):
    m, d = x.shape
    f = w1.shape[2]
    nb = seq // tm
    return pl.pallas_call(
        _nm_swiglu_body,
        grid=(m // tm, f // tf),
        in_specs=[pl.BlockSpec((tm, d), lambda i, j: (i, 0)),
                  pl.BlockSpec((1, d), lambda i, j: (0, 0)),
                  pl.BlockSpec((1, 1, d), lambda i, j: (i // nb, 0, 0)),
                  pl.BlockSpec((1, 1, d), lambda i, j: (i // nb, 0, 0)),
                  pl.BlockSpec((None, d, tf), lambda i, j: (layer, 0, j)),
                  pl.BlockSpec((None, d, tf), lambda i, j: (layer, 0, j))],
        out_specs=pl.BlockSpec((tm, tf), lambda i, j: (i, j)),
        out_shape=jax.ShapeDtypeStruct((m, f), BF16),
        scratch_shapes=[pltpu.VMEM((tm, d), BF16)],
        compiler_params=_params("arbitrary", "arbitrary"),
        name="norm_mod_swiglu_up",
    )(x, g.reshape(1, d), sc.reshape(-1, 1, d), sh.reshape(-1, 1, d), w1, w3)


def _dsa_body(q_ref, qi_ref, k_ref, v_ref, tq_ref, tk_ref, pc_ref, pr_ref, tab_ref, gq_ref, gk_ref,
              o_ref,
              kn_scr, kke_scr, kko_scr, qst_scr, wib_scr, key_scr, qn_scr, m_scr, l_scr, acc_scr,
              *, topk, tq, seq):
    i = pl.program_id(1)
    nc = i + 1
    grp = A_HEADS // A_KV_HEADS
    hd = A_HEAD_DIM
    lane = lax.broadcasted_iota(I32, (1, LANES), 1)
    row = lax.broadcasted_iota(I32, (tq, tq), 0)
    col = lax.broadcasted_iota(I32, (tq, tq), 1)

    @pl.when(i == 0)
    def _prepare_keys():
        gk = gk_ref[...]

        def body(r, carry):
            rows = pl.ds(pl.multiple_of(r * tq, tq), tq)
            for kh in range(A_KV_HEADS):
                kb = k_ref[rows, kh * hd:(kh + 1) * hd].astype(F32)
                ms = jnp.mean(kb * kb, axis=-1, keepdims=True)
                kn_scr[rows, kh * hd:(kh + 1) * hd] = ((kb * lax.rsqrt(ms + EPS)) * gk).astype(BF16)
            t = tk_ref[rows, :].astype(F32)
            kke_scr[rows, :] = jnp.where(lane < IDX_DIM, t, 0.0).astype(BF16)
            kko_scr[rows, :] = jnp.where(lane >= IDX_DIM, pltpu.roll(t, IDX_DIM, 1), 0.0).astype(BF16)
            return carry

        lax.fori_loop(0, seq // tq, body, 0)

    for j in range(IDX_HEADS // 2):
        qst_scr[j * tq:(j + 1) * tq, :] = qi_ref[:, j * LANES:(j + 1) * LANES]
    tqf = tq_ref[...].astype(F32)
    w_scale = (IDX_DIM ** -0.5) * (IDX_HEADS ** -0.5)
    for h in range(IDX_HEADS):
        wib_scr[h] = jnp.broadcast_to(tqf[:, IDX_DIM + h:IDX_DIM + h + 1], (tq, LANES)) * w_scale

    def idx_body(c, carry):
        rows = pl.ds(pl.multiple_of(c * tq, tq), tq)
        qst = qst_scr[...]
        re = _dot_nt(qst, kke_scr[rows, :])
        ro = _dot_nt(qst, kko_scr[rows, :])
        acc = jnp.zeros((tq, tq), F32)
        for j in range(IDX_HEADS // 2):
            we = wib_scr[2 * j]
            wo = wib_scr[2 * j + 1]
            we = jnp.concatenate([we] * (tq // LANES), axis=1)
            wo = jnp.concatenate([wo] * (tq // LANES), axis=1)
            acc = acc + jnp.maximum(re[j * tq:(j + 1) * tq], 0.0) * we
            acc = acc + jnp.maximum(ro[j * tq:(j + 1) * tq], 0.0) * wo
        bits = pltpu.bitcast(acc, I32)
        key = bits ^ ((bits >> 31) & 0x7FFFFFFF)
        causal = (c < i) | (col <= row)
        key_scr[c] = jnp.where(causal, key, INT_MIN)
        return carry

    lax.fori_loop(0, nc, idx_body, 0)

    def bit_body(bi, t_u):
        cand_u = t_u | lax.shift_left(jnp.int32(1), 31 - bi)
        cand_s = cand_u ^ INT_MIN

        def cnt_body(c, cnt):
            ge = jnp.where(key_scr[c] >= cand_s, 1, 0)
            for s in range(tq // LANES):
                cnt = cnt + ge[:, s * LANES:(s + 1) * LANES]
            return cnt

        cnt = lax.fori_loop(0, nc, cnt_body, jnp.zeros((tq, LANES), I32))
        total = jnp.sum(cnt, axis=1, keepdims=True)
        return jnp.where(total >= topk, cand_u, t_u)

    nbits = jnp.where(nc * tq > topk, 32, 0)
    t_u = lax.fori_loop(0, nbits, bit_body, jnp.zeros((tq, 1), I32))
    thr = t_u ^ INT_MIN

    qpos = pc_ref[...]

    def bkt_body(c, carry):
        kpos = pr_ref[0, c]
        bkt = _t5_bucket(qpos - kpos)
        causal = (c < i) | (col <= row)
        sel = (key_scr[c] >= thr) & causal
        key_scr[c] = jnp.where(sel, bkt, N_BUCKETS)
        return carry

    lax.fori_loop(0, nc, bkt_body, 0)

    gq = gq_ref[...]
    scale = hd ** -0.5
    for g in range(A_KV_HEADS):
        for hh in range(grp):
            h = g * grp + hh
            qh = q_ref[:, h * hd:(h + 1) * hd].astype(F32)
            ms = jnp.mean(qh * qh, axis=-1, keepdims=True)
            qn_scr[hh * tq:(hh + 1) * tq, :] = (((qh * lax.rsqrt(ms + EPS)) * gq) * scale).astype(BF16)
        m_scr[...] = jnp.full(m_scr.shape, NEG, F32)
        l_scr[...] = jnp.zeros(l_scr.shape, F32)
        acc_scr[...] = jnp.zeros(acc_scr.shape, F32)

        def att_body(c, carry, g=g):
            rows = pl.ds(pl.multiple_of(c * tq, tq), tq)
            kc = kn_scr[rows, g * hd:(g + 1) * hd]
            vc = v_ref[rows, g * hd:(g + 1) * hd]
            s = _dot_nt(qn_scr[...], kc)
            bkt = key_scr[c]
            parts = []
            for hh in range(grp):
                h = g * grp + hh
                bias = _bias_lookup(tab_ref[h:h + 1, :], bkt, tq)
                parts.append(s[hh * tq:(hh + 1) * tq] + bias)
            s = jnp.concatenate(parts, axis=0)
            m_old = m_scr[...]
            m_new = jnp.maximum(m_old, jnp.max(s, axis=1, keepdims=True))
            p = jnp.exp(s - m_new)
            alpha = jnp.exp(m_old - m_new)
            l_scr[...] = alpha * l_scr[...] + jnp.sum(p, axis=1, keepdims=True)
            acc_scr[...] = alpha * acc_scr[...] + _dot(p.astype(BF16), vc)
            m_scr[...] = m_new
            return carry

        lax.fori_loop(0, nc, att_body, 0)
        o = acc_scr[...] * (1.0 / l_scr[...])
        for hh in range(grp):
            h = g * grp + hh
            o_ref[:, h * hd:(h + 1) * hd] = o[hh * tq:(hh + 1) * tq].astype(o_ref.dtype)


def _dsa_attention(qkv, tail, pos_col, pos_row, tab, gq, gk, batch, seq, tq=256):
    nq = seq // tq
    a_q = A_HEADS * A_HEAD_DIM
    a_kv = A_KV_HEADS * A_HEAD_DIM
    a_qi = IDX_HEADS * IDX_DIM
    topk = min(TOPK_MAX, seq // 4)
    grp = A_HEADS // A_KV_HEADS
    body = functools.partial(_dsa_body, topk=topk, tq=tq, seq=seq)
    return pl.pallas_call(
        body,
        grid=(batch, nq),
        in_specs=[
            pl.BlockSpec((tq, a_q), lambda b, i: (b * nq + i, 0)),
            pl.BlockSpec((tq, a_qi), lambda b, i: (b * nq + i, (a_q + 2 * a_kv) // a_qi)),
            pl.BlockSpec((seq, a_kv), lambda b, i: (b, a_q // a_kv)),
            pl.BlockSpec((seq, a_kv), lambda b, i: (b, a_q // a_kv + 1)),
            pl.BlockSpec((tq, LANES), lambda b, i: (b * nq + i, 0)),
            pl.BlockSpec((seq, LANES), lambda b, i: (b, 0)),
            pl.BlockSpec((tq, 1), lambda b, i: (b * nq + i, 0)),
            pl.BlockSpec((1, nq, 1, tq), lambda b, i: (b, 0, 0, 0)),
            pl.BlockSpec((A_HEADS, LANES), lambda b, i: (0, 0)),
            pl.BlockSpec((1, A_HEAD_DIM), lambda b, i: (0, 0)),
            pl.BlockSpec((1, A_HEAD_DIM), lambda b, i: (0, 0)),
        ],
        out_specs=pl.BlockSpec((tq, a_q), lambda b, i: (b * nq + i, 0)),
        out_shape=jax.ShapeDtypeStruct((batch * seq, a_q), BF16),
        scratch_shapes=[
            pltpu.VMEM((seq, a_kv), BF16),
            pltpu.VMEM((seq, LANES), BF16),
            pltpu.VMEM((seq, LANES), BF16),
            pltpu.VMEM((IDX_HEADS // 2 * tq, LANES), BF16),
            pltpu.VMEM((IDX_HEADS, tq, LANES), F32),
            pltpu.VMEM((nq, tq, tq), I32),
            pltpu.VMEM((grp * tq, A_HEAD_DIM), BF16),
            pltpu.VMEM((grp * tq, 1), F32),
            pltpu.VMEM((grp * tq, 1), F32),
            pltpu.VMEM((grp * tq, A_HEAD_DIM), F32),
        ],
        compiler_params=_params("arbitrary", "arbitrary"),
        name="dsa_attention",
    )(qkv, qkv, qkv, qkv, tail, tail, pos_col, pos_row, tab, gq, gk)


def _band_body(*refs, cls_len, wk, use_prev, tq):
    if use_prev:
        (q_ref, kc_ref, vc_ref, kp_ref, vp_ref, pc_ref, prc_ref, prp_ref,
         tab_ref, gq_ref, gk_ref, o_ref, lse_ref, bk_scr) = refs
    else:
        (q_ref, kc_ref, vc_ref, pc_ref, prc_ref,
         tab_ref, gq_ref, gk_ref, o_ref, lse_ref, bk_scr) = refs
    t = pl.program_id(1)
    hd = B_HEAD_DIM
    pw = 2 * hd
    shift = cls_len.bit_length() - 1
    lane = lax.broadcasted_iota(I32, (1, pw), 1)
    lo = lane < hd
    qpos = pc_ref[...]
    fq = t * tq + lax.broadcasted_iota(I32, (tq, 1), 0)

    def bucket_mask(fk, kpos):
        same = ((fq + cls_len) >> shift) == ((fk + cls_len) >> shift)
        d = (fq & (cls_len - 1)) - (fk & (cls_len - 1))
        ok = same & (d >= 0) & (d <= wk)
        return jnp.where(ok, _t5_bucket(qpos - kpos), N_BUCKETS)

    off = LANES if use_prev else 0
    if use_prev:
        fk = t * tq - LANES + lax.broadcasted_iota(I32, (1, LANES), 1)
        bk_scr[:, 0:LANES] = bucket_mask(fk, prp_ref[0])
    fk = t * tq + lax.broadcasted_iota(I32, (1, tq), 1)
    bk_scr[:, off:off + tq] = bucket_mask(fk, prc_ref[0])

    gq = gq_ref[...]
    gk = gk_ref[...]
    scale = hd ** -0.5

    def pair_norm(x, g):
        sq = x * x
        ms_lo = jnp.sum(jnp.where(lo, sq, 0.0), axis=-1, keepdims=True) * (1.0 / hd)
        ms_hi = jnp.sum(jnp.where(lo, 0.0, sq), axis=-1, keepdims=True) * (1.0 / hd)
        inv = jnp.where(lo, lax.rsqrt(ms_lo + EPS), lax.rsqrt(ms_hi + EPS))
        return (x * inv) * g

    for p in range(B_HEADS // 2):
        cols = slice(p * pw, (p + 1) * pw)
        qn = pair_norm(q_ref[:, cols].astype(F32), gq) * scale
        if use_prev:
            kraw = jnp.concatenate([kp_ref[:, cols], kc_ref[:, cols]], axis=0)
            v = jnp.concatenate([vp_ref[:, cols], vc_ref[:, cols]], axis=0)
        else:
            kraw = kc_ref[:, cols]
            v = vc_ref[:, cols]
        kn = pair_norm(kraw.astype(F32), gk).astype(BF16)
        bkt = bk_scr[...]
        o_pair = jnp.zeros((tq, pw), F32)
        lse_pair = jnp.zeros((tq, pw), F32)
        for half in range(2):
            h = 2 * p + half
            keep = lo if half == 0 else jnp.logical_not(lo)
            qh = jnp.where(keep, qn, 0.0).astype(BF16)
            vh = jnp.where(keep, v, jnp.zeros_like(v))
            s = _dot_nt(qh, kn) + _bias_lookup(tab_ref[h:h + 1, :], bkt, tq)
            m = jnp.max(s, axis=1, keepdims=True)
            e = jnp.exp(s - m)
            l = jnp.sum(e, axis=1, keepdims=True)
            o_pair = o_pair + _dot(e.astype(BF16), vh) * (1.0 / l)
            lse_pair = jnp.where(keep, m + jnp.log(l), lse_pair)
        o_ref[:, cols] = o_pair.astype(o_ref.dtype)
        lse_ref[:, cols] = lse_pair


def _band_attention(q_arr, q_col, k_arr, k_col, v_arr, v_col, pos, tab, gq2, gk2,
                    batch, seq, dilation, window, tq=512):
    cls_len = seq // dilation
    wk = window // dilation
    assert cls_len & (cls_len - 1) == 0 and wk <= LANES
    tq = min(tq, seq)
    use_prev = cls_len > tq
    assert use_prev or tq % cls_len == 0
    nt = seq // tq
    w = B_HEADS * B_HEAD_DIM
    sub = tq // LANES
    pos_col = pos.reshape(batch * seq, 1)
    pos_row = pos.reshape(batch * nt, 1, tq)
    in_specs = [pl.BlockSpec((tq, w), lambda b, t: (b * nt + t, q_col)),
                pl.BlockSpec((tq, w), lambda b, t: (b * nt + t, k_col)),
                pl.BlockSpec((tq, w), lambda b, t: (b * nt + t, v_col))]
    args = [q_arr, k_arr, v_arr]
    if use_prev:
        prev = lambda b, t: jnp.maximum((b * nt + t) * sub - 1, 0)
        in_specs += [pl.BlockSpec((LANES, w), lambda b, t: (prev(b, t), k_col)),
                     pl.BlockSpec((LANES, w), lambda b, t: (prev(b, t), v_col))]
        args += [k_arr, v_arr]
    in_specs += [pl.BlockSpec((tq, 1), lambda b, t: (b * nt + t, 0)),
                 pl.BlockSpec((1, 1, tq), lambda b, t: (b * nt + t, 0, 0))]
    args += [pos_col, pos_row]
    if use_prev:
        in_specs.append(pl.BlockSpec((1, 1, LANES), lambda b, t: (prev(b, t), 0, 0)))
        args.append(pos.reshape(batch * seq // LANES, 1, LANES))
    in_specs += [pl.BlockSpec((B_HEADS, LANES), lambda b, t: (0, 0)),
                 pl.BlockSpec((1, LANES), lambda b, t: (0, 0)),
                 pl.BlockSpec((1, LANES), lambda b, t: (0, 0))]
    args += [tab, gq2, gk2]
    nk = tq + (LANES if use_prev else 0)
    body = functools.partial(_band_body, cls_len=cls_len, wk=wk, use_prev=use_prev, tq=tq)
    return pl.pallas_call(
        body,
        grid=(batch, nt),
        in_specs=in_specs,
        out_specs=[pl.BlockSpec((tq, w), lambda b, t: (b * nt + t, 0)),
                   pl.BlockSpec((tq, w), lambda b, t: (b * nt + t, 0))],
        out_shape=[jax.ShapeDtypeStruct((batch * seq, w), BF16),
                   jax.ShapeDtypeStruct((batch * seq, w), F32)],
        scratch_shapes=[pltpu.VMEM((tq, nk), I32)],
        compiler_params=_params("arbitrary", "arbitrary"),
        name="dilated_attention",
    )(*args)


def _merge_body(o0, o1, o2, l0, l1, l2, w_ref, x_ref, gt_ref, out_ref, a_scr):
    @pl.when(pl.program_id(1) == 0)
    def _():
        m = jnp.maximum(jnp.maximum(l0[...], l1[...]), l2[...])
        w0 = jnp.exp(l0[...] - m)
        w1 = jnp.exp(l1[...] - m)
        w2 = jnp.exp(l2[...] - m)
        num = w0 * o0[...].astype(F32) + w1 * o1[...].astype(F32) + w2 * o2[...].astype(F32)
        a_scr[...] = (num / (w0 + w1 + w2)).astype(BF16)

    out_ref[...] = x_ref[...] + gt_ref[0] * _dot(a_scr[...], w_ref[...].astype(BF16))


def _merge_out(os_, ls_, w3, layer, x, gt, seq, tm=512, tn=512):
    m, k = os_[0].shape
    d = x.shape[1]
    nb = seq // tm
    row = pl.BlockSpec((tm, k), lambda i, j: (i, 0))
    return pl.pallas_call(
        _merge_body,
        grid=(m // tm, d // tn),
        in_specs=[row] * 6 + [
            pl.BlockSpec((None, k, tn), lambda i, j: (layer, 0, j)),
            pl.BlockSpec((tm, tn), lambda i, j: (i, j)),
            pl.BlockSpec((1, 1, tn), lambda i, j: (i // nb, 0, j))],
        out_specs=pl.BlockSpec((tm, tn), lambda i, j: (i, j)),
        out_shape=jax.ShapeDtypeStruct((m, d), F32),
        scratch_shapes=[pltpu.VMEM((tm, k), BF16)],
        compiler_params=_params("arbitrary", "arbitrary"),
        name="merge_out_proj",
    )(*os_, *ls_, w3, x, gt.reshape(-1, 1, d))


def _router_body(x_ref, g_ref, sc_ref, sh_ref, rh_ref, rl_ref, rb_ref, h_ref, rt_ref):
    h = _norm_mod(x_ref[...], g_ref[...], sc_ref[0], sh_ref[0])
    h_ref[...] = h
    hh = h.astype(BF16)
    hl = (h - hh.astype(F32)).astype(BF16)
    logits = _dot(hh, rh_ref[...]) + _dot(hl, rh_ref[...]) + _dot(hh, rl_ref[...]) + rb_ref[...]
    lane = lax.broadcasted_iota(I32, logits.shape, 1)
    logits = jnp.where(lane < N_EXPERTS, logits, NEG)
    v1 = jnp.max(logits, axis=1, keepdims=True)
    i1 = jnp.min(jnp.where(logits == v1, lane, LANES), axis=1, keepdims=True)
    rest = jnp.where(lane == i1, NEG, logits)
    v2 = jnp.max(rest, axis=1, keepdims=True)
    i2 = jnp.min(jnp.where(rest == v2, lane, LANES), axis=1, keepdims=True)
    e = jnp.exp(v2 - v1)
    g1 = 1.0 / (1.0 + e)
    g2 = e * g1
    rt_ref[...] = jnp.where(lane == 0, i1.astype(F32),
                            jnp.where(lane == 1, i2.astype(F32),
                                      jnp.where(lane == 2, g1, jnp.where(lane == 3, g2, 0.0))))


def _router(x, g, sc, sh, rw, rb, seq, tm=512):
    m, d = x.shape
    nb = seq // tm
    ne = rw.shape[1]
    rw_p = jnp.zeros((d, LANES), F32).at[:, :ne].set(rw)
    rh = rw_p.astype(BF16)
    rl = (rw_p - rh.astype(F32)).astype(BF16)
    rb_p = jnp.zeros((1, LANES), F32).at[0, :ne].set(rb)
    return pl.pallas_call(
        _router_body,
        grid=(m // tm,),
        in_specs=[pl.BlockSpec((tm, d), lambda i: (i, 0)),
                  pl.BlockSpec((1, d), lambda i: (0, 0)),
                  pl.BlockSpec((1, 1, d), lambda i: (i // nb, 0, 0)),
                  pl.BlockSpec((1, 1, d), lambda i: (i // nb, 0, 0)),
                  pl.BlockSpec((d, LANES), lambda i: (0, 0)),
                  pl.BlockSpec((d, LANES), lambda i: (0, 0)),
                  pl.BlockSpec((1, LANES), lambda i: (0, 0))],
        out_specs=[pl.BlockSpec((tm, d), lambda i: (i, 0)),
                   pl.BlockSpec((tm, LANES), lambda i: (i, 0))],
        out_shape=[jax.ShapeDtypeStruct((m, d), F32),
                   jax.ShapeDtypeStruct((m, LANES), F32)],
        compiler_params=_params("arbitrary"),
        name="router_top2",
    )(x, g.reshape(1, d), sc.reshape(-1, 1, d), sh.reshape(-1, 1, d), rh, rl, rb_p)


def _row_copy(src_hbm, idx, buf, r, sem):
    return pltpu.make_async_copy(src_hbm.at[pl.ds(idx, 1), :], buf.at[pl.ds(r, 1), :], sem)


def _gather_body(src_ref, h_hbm, o_ref, buf, sem, *, rows):
    base = pl.program_id(0) * rows

    def issue(r, carry):
        idx = src_ref[base + r]

        @pl.when(idx >= 0)
        def _():
            _row_copy(h_hbm, idx, buf, r, sem).start()

        @pl.when(idx < 0)
        def _():
            buf[pl.ds(r, 1), :] = jnp.zeros((1, buf.shape[1]), buf.dtype)

        return carry

    lax.fori_loop(0, rows, issue, 0)

    def drain(r, carry):
        @pl.when(src_ref[base + r] >= 0)
        def _():
            _row_copy(h_hbm, 0, buf, r, sem).wait()

        return carry

    lax.fori_loop(0, rows, drain, 0)
    o_ref[...] = buf[...].astype(o_ref.dtype)


def _gather_rows(h, src, rows=128):
    r_total = src.shape[0]
    d = h.shape[1]
    return pl.pallas_call(
        functools.partial(_gather_body, rows=rows),
        grid_spec=pltpu.PrefetchScalarGridSpec(
            num_scalar_prefetch=1,
            grid=(r_total // rows,),
            in_specs=[pl.BlockSpec(memory_space=pl.ANY)],
            out_specs=pl.BlockSpec((rows, d), lambda i, s: (i, 0)),
            scratch_shapes=[pltpu.VMEM((rows, d), F32), pltpu.SemaphoreType.DMA(())],
        ),
        out_shape=jax.ShapeDtypeStruct((r_total, d), BF16),
        compiler_params=_params("arbitrary"),
        name="moe_dispatch_gather",
    )(src, h)


def _moe_body(te_ref, tv_ref, hs_ref, w1_ref, w3_ref, w2_ref, o_ref, w1b, w3b, w2b, *, chunk):
    t = pl.program_id(0)
    f = pl.program_id(1)
    valid = tv_ref[t]
    tm = hs_ref.shape[0]
    nch = (valid + chunk - 1) // chunk

    @pl.when(valid > 0)
    def _():
        w1b[...] = w1_ref[...].astype(BF16)
        w3b[...] = w3_ref[...].astype(BF16)
        w2b[...] = w2_ref[...].astype(BF16)

        def body(c, carry):
            rows = pl.ds(pl.multiple_of(c * chunk, chunk), chunk)
            hc = hs_ref[rows, :]
            a = _dot(hc, w1b[...])
            b = _dot(hc, w3b[...])
            u = (a * _sigmoid(a) * b).astype(BF16)
            y = _dot(u, w2b[...])

            @pl.when(f == 0)
            def _():
                o_ref[rows, :] = y

            @pl.when(f > 0)
            def _():
                o_ref[rows, :] += y

            return carry

        lax.fori_loop(0, nch, body, 0)

    @pl.when(f == 0)
    def _():
        def zero(c, carry):
            rows = pl.ds(pl.multiple_of(c * chunk, chunk), chunk)
            o_ref[rows, :] = jnp.zeros((chunk, o_ref.shape[1]), o_ref.dtype)
            return carry

        lax.fori_loop(nch, tm // chunk, zero, 0)


def _moe_experts(hs, tile_expert, tile_valid, w1, w3, w2, tm, tf=256, chunk=128):
    r_total, d = hs.shape
    n_tiles = r_total // tm
    nf = w1.shape[2] // tf

    def w13_map(t, f, te, tv):
        return (te[t], 0, jnp.where(tv[t] > 0, f, nf - 1))

    def w2_map(t, f, te, tv):
        return (te[t], jnp.where(tv[t] > 0, f, nf - 1), 0)

    return pl.pallas_call(
        functools.partial(_moe_body, chunk=chunk),
        grid_spec=pltpu.PrefetchScalarGridSpec(
            num_scalar_prefetch=2,
            grid=(n_tiles, nf),
            in_specs=[pl.BlockSpec((tm, d), lambda t, f, te, tv: (t, 0)),
                      pl.BlockSpec((None, d, tf), w13_map),
                      pl.BlockSpec((None, d, tf), w13_map),
                      pl.BlockSpec((None, tf, d), w2_map)],
            out_specs=pl.BlockSpec((tm, d), lambda t, f, te, tv: (t, 0)),
            scratch_shapes=[pltpu.VMEM((d, tf), BF16), pltpu.VMEM((d, tf), BF16),
                            pltpu.VMEM((tf, d), BF16)],
        ),
        out_shape=jax.ShapeDtypeStruct((r_total, d), F32),
        compiler_params=_params("arbitrary", "arbitrary"),
        name="moe_experts",
    )(tile_expert, tile_valid, hs, w1, w3, w2)


def _combine_body(p1_ref, p2_ref, y_hbm, x_ref, gt_ref, rt_ref, o_ref, buf_a, buf_b, sem, *, rows):
    base = pl.program_id(0) * rows

    def issue(r, carry):
        _row_copy(y_hbm, p1_ref[base + r], buf_a, r, sem).start()
        _row_copy(y_hbm, p2_ref[base + r], buf_b, r, sem).start()
        return carry

    lax.fori_loop(0, rows, issue, 0)

    def drain(r, carry):
        _row_copy(y_hbm, 0, buf_a, r, sem).wait()
        _row_copy(y_hbm, 0, buf_b, r, sem).wait()
        return carry

    lax.fori_loop(0, rows, drain, 0)
    rt = rt_ref[...]
    g1 = rt[:, 2:3]
    g2 = rt[:, 3:4]
    o_ref[...] = x_ref[...] + gt_ref[0] * (g1 * buf_a[...] + g2 * buf_b[...])


def _combine(ys, p1, p2, x, gt, route, seq, rows=256):
    m, d = x.shape
    nb = seq // rows
    return pl.pallas_call(
        functools.partial(_combine_body, rows=rows),
        grid_spec=pltpu.PrefetchScalarGridSpec(
            num_scalar_prefetch=2,
            grid=(m // rows,),
            in_specs=[pl.BlockSpec(memory_space=pl.ANY),
                      pl.BlockSpec((rows, d), lambda i, a, b: (i, 0)),
                      pl.BlockSpec((1, 1, d), lambda i, a, b: (i // nb, 0, 0)),
                      pl.BlockSpec((rows, LANES), lambda i, a, b: (i, 0))],
            out_specs=pl.BlockSpec((rows, d), lambda i, a, b: (i, 0)),
            scratch_shapes=[pltpu.VMEM((rows, d), F32), pltpu.VMEM((rows, d), F32),
                            pltpu.SemaphoreType.DMA(())],
        ),
        out_shape=jax.ShapeDtypeStruct((m, d), F32),
        compiler_params=_params("arbitrary"),
        name="moe_combine",
    )(p1, p2, ys, x, gt.reshape(-1, 1, d), route)


def _routing_tables(route, tm, n_tiles):
    t = route.shape[0]
    experts = route[:, :2].astype(I32).reshape(-1)
    onehot = (experts[:, None] == jnp.arange(N_EXPERTS, dtype=I32)[None, :]).astype(I32)
    csum = jnp.cumsum(onehot, axis=0)
    rank = jnp.sum(csum * onehot, axis=1) - 1
    counts = csum[-1]
    tiles = (counts + tm - 1) // tm
    tend = jnp.cumsum(tiles)
    tstart = tend - tiles
    slot = tstart[experts] * tm + rank
    token = jnp.arange(2 * t, dtype=I32) // 2
    src = jnp.full((n_tiles * tm,), -1, I32).at[slot].set(token)
    tile_id = jnp.arange(n_tiles, dtype=I32)
    te = jnp.sum((tile_id[:, None] >= tend[None, :]).astype(I32), axis=1)
    active = tile_id < tend[-1]
    last_e = jnp.sum((tend[-1] - 1 >= tend).astype(I32))
    te = jnp.where(active, te, last_e)
    tv = jnp.where(active, jnp.clip(counts[te] - (tile_id - tstart[te]) * tm, 0, tm), 0)
    slots = slot.reshape(t, 2)
    return src, te.astype(I32), tv.astype(I32), slots[:, 0], slots[:, 1]


def _bias_table(rel_bias):
    h = rel_bias.shape[1]
    return jnp.full((h, LANES), NEG, F32).at[:, :N_BUCKETS].set(rel_bias.T)


def _to_classes(a, batch, seq, r):
    if r == 1:
        return a
    w = a.shape[-1]
    return a.reshape(batch, seq // r, r, w).transpose(0, 2, 1, 3).reshape(batch * seq, w)


def _from_classes(a, batch, seq, r):
    if r == 1:
        return a
    w = a.shape[-1]
    return a.reshape(batch, r, seq // r, w).transpose(0, 2, 1, 3).reshape(batch * seq, w)


def kernel(x, c, positions, rel_bias, w_mod, b_mod, g_attn, g_ffn, a_w_in, a_w_out, a_g_qn, a_g_kn,
           kv_w_mod, kv_b_mod, kv_g, kv_w, b_g_kn, b_w_q, b_w_out, b_g_qn, ffn_w1, ffn_w3, ffn_w2,
           moe_router, moe_router_b, moe_w1, moe_w3, moe_w2):
    batch, seq, d = x.shape
    m = batch * seq
    x2 = x.reshape(m, d)
    positions = positions.astype(I32)

    c8 = jnp.zeros((8, d), F32).at[:batch].set(c)
    mod0 = _mod_call(c8, w_mod, 0, b_mod)[:batch]
    mod1 = _mod_call(c8, w_mod, 1, b_mod)[:batch]
    kvm = _mod_call(c8, kv_w_mod[None], 0, kv_b_mod[None])[:batch]
    sh1_0, sc1_0, gt1_0, sh2_0, sc2_0, gt2_0 = jnp.split(mod0, 6, axis=-1)
    sh1_1, sc1_1, gt1_1, sh2_1, sc2_1, gt2_1 = jnp.split(mod1, 6, axis=-1)
    kv_sh, kv_sc = jnp.split(kvm, 2, axis=-1)

    tab = _bias_table(rel_bias)

    a_main = A_HEADS * A_HEAD_DIM + 2 * A_KV_HEADS * A_HEAD_DIM + IDX_HEADS * IDX_DIM
    n_tail = IDX_DIM + IDX_HEADS
    w_tail = jnp.zeros((d, LANES), F32).at[:, :n_tail].set(a_w_in[0, :, a_main:a_main + n_tail])
    qkv, tail = _nm_matmul(x2, g_attn[0], sc1_0, sh1_0, a_w_in, 0, a_main, seq, w_tail=w_tail)
    tq = min(256, seq)
    nq = seq // tq
    attn = _dsa_attention(qkv, tail, positions.reshape(m, 1), positions.reshape(batch, nq, 1, tq), tab,
                          a_g_qn[0].reshape(1, -1), a_g_kn[0].reshape(1, -1), batch, seq, tq=tq)
    x2 = _matmul_residual(attn, a_w_out, 0, x2, gt1_0, seq)
    u = _nm_swiglu(x2, g_ffn[0], sc2_0, sh2_0, ffn_w1, ffn_w3, 0, seq)
    x2 = _matmul_residual(u, ffn_w2, 0, x2, gt2_0, seq)

    b_q = len(B_DILATIONS) * B_HEADS * B_HEAD_DIM
    kvall = _nm_matmul(x2, kv_g, kv_sc, kv_sh, kv_w[None], 0, 2 * b_q, seq)
    qall = _nm_matmul(x2, g_attn[1], sc1_1, sh1_1, b_w_q, 0, b_q, seq)
    gq2 = jnp.tile(b_g_qn[0], 2).reshape(1, LANES)
    gk2 = jnp.tile(b_g_kn, 2).reshape(1, LANES)
    w = B_HEADS * B_HEAD_DIM
    ng = len(B_DILATIONS)
    outs, lses = [], []
    for g, (window, r) in enumerate(B_DILATIONS):
        if r == 1:
            q_arr, q_col, k_arr, k_col, v_arr, v_col = qall, g, kvall, g, kvall, ng + g
            pos_g = positions
        else:
            q_arr = _to_classes(qall[:, g * w:(g + 1) * w], batch, seq, r)
            k_arr = _to_classes(kvall[:, g * w:(g + 1) * w], batch, seq, r)
            v_arr = _to_classes(kvall[:, (ng + g) * w:(ng + g + 1) * w], batch, seq, r)
            q_col = k_col = v_col = 0
            pos_g = positions.reshape(batch, seq // r, r).transpose(0, 2, 1).reshape(batch, seq)
        o_g, lse_g = _band_attention(q_arr, q_col, k_arr, k_col, v_arr, v_col, pos_g, tab, gq2, gk2,
                                     batch, seq, r, window)
        outs.append(_from_classes(o_g, batch, seq, r))
        lses.append(_from_classes(lse_g, batch, seq, r))
    x2 = _merge_out(outs, lses, b_w_out, 0, x2, gt1_1, seq)

    h, route = _router(x2, g_ffn[1], sc2_1, sh2_1, moe_router[0], moe_router_b[0], seq)
    tm = 1152
    n_tiles = (2 * m) // tm + N_EXPERTS
    src, te, tv, p1, p2 = _routing_tables(route, tm, n_tiles)
    hs = _gather_rows(h, src)
    ys = _moe_experts(hs, te, tv, moe_w1.reshape(moe_w1.shape[1:]), moe_w3.reshape(moe_w3.shape[1:]),
                      moe_w2.reshape(moe_w2.shape[1:]), tm)
    out = _combine(ys, p1, p2, x2, gt2_1, route, seq)
    return out.reshape(batch, seq, d)
```

```python
import functools
import math

import jax
import jax.numpy as jnp
from jax import lax
from jax.experimental import pallas as pl
from jax.experimental.pallas import tpu as pltpu

F32 = jnp.float32
BF16 = jnp.bfloat16
I32 = jnp.int32

EPS = 1e-6
NEG = -1e30
INT_MIN = -(2 ** 31)

A_HEADS, A_KV_HEADS, A_HEAD_DIM = 16, 4, 128
IDX_HEADS, IDX_DIM = 16, 64
TOPK_MAX = 256
B_DILATIONS = ((128, 1), (512, 4), (2048, 16))
B_HEADS, B_HEAD_DIM = 16, 64
N_BUCKETS, MAX_DISTANCE = 32, 2048
N_EXPERTS = 8
LANES = 128

VMEM_LIMIT_BYTES = 56 * 1024 * 1024

_NT = (((1,), (1,)), ((), ()))


def _params(*sem):
    return pltpu.CompilerParams(dimension_semantics=sem, vmem_limit_bytes=VMEM_LIMIT_BYTES)


def _dot(a, b):
    return jnp.dot(a, b, preferred_element_type=F32)


def _dot_nt(a, b):
    return lax.dot_general(a, b, _NT, preferred_element_type=F32)


def _sigmoid(x):
    return 1.0 / (1.0 + jnp.exp(-x))


def _t5_bucket(rel):
    n = jnp.maximum(rel, 0)
    max_exact = N_BUCKETS // 2
    nf = jnp.maximum(n, 1).astype(F32)
    large = max_exact + (jnp.log(nf / max_exact) / math.log(MAX_DISTANCE / max_exact)
                         * (N_BUCKETS - max_exact)).astype(I32)
    large = jnp.minimum(large, N_BUCKETS - 1)
    return jnp.where(n < max_exact, n, large)


def _norm_mod(x, g, sc, sh):
    y = x * lax.rsqrt(jnp.mean(x * x, axis=-1, keepdims=True) + EPS)
    return (y * g) * (1.0 + sc) + sh


def _bias_lookup(tab_row, bkt, rows):
    tab = jnp.broadcast_to(tab_row, (rows, LANES))
    parts = [jnp.take_along_axis(tab, bkt[:, c * LANES:(c + 1) * LANES], axis=1)
             for c in range(bkt.shape[1] // LANES)]
    return parts[0] if len(parts) == 1 else jnp.concatenate(parts, axis=1)


def _mod_body(c_ref, w_ref, b_ref, o_ref):
    c = c_ref[...]
    cs = c * _sigmoid(c)
    o_ref[...] = _dot(cs.astype(BF16), w_ref[...].astype(BF16)) + b_ref[...]


def _mod_call(c8, w3, layer, b2):
    _, d, n = w3.shape
    tn = 1024
    return pl.pallas_call(
        _mod_body,
        grid=(n // tn,),
        in_specs=[pl.BlockSpec((8, d), lambda j: (0, 0)),
                  pl.BlockSpec((None, d, tn), lambda j: (layer, 0, j)),
                  pl.BlockSpec((None, 1, tn), lambda j: (layer, 0, j))],
        out_specs=pl.BlockSpec((8, tn), lambda j: (0, j)),
        out_shape=jax.ShapeDtypeStruct((8, n), F32),
        compiler_params=_params("arbitrary"),
        name="adaln_mod",
    )(c8, w3, b2.reshape(b2.shape[0], 1, n))


def _nm_body(x_ref, g_ref, sc_ref, sh_ref, w_ref, *rest, has_tail):
    if has_tail:
        wt_ref, o_ref, ot_ref, h_scr = rest
    else:
        o_ref, h_scr = rest

    @pl.when(pl.program_id(1) == 0)
    def _():
        h = _norm_mod(x_ref[...], g_ref[...], sc_ref[0], sh_ref[0]).astype(BF16)
        h_scr[...] = h
        if has_tail:
            ot_ref[...] = _dot(h, wt_ref[...].astype(BF16)).astype(ot_ref.dtype)

    o_ref[...] = _dot(h_scr[...], w_ref[...].astype(BF16)).astype(o_ref.dtype)


def _nm_matmul(x, g, sc, sh, w3, layer, n_cols, seq, w_tail=None, tm=1024, tn=512):
    m, d = x.shape
    nb = seq // tm
    has_tail = w_tail is not None
    in_specs = [pl.BlockSpec((tm, d), lambda i, j: (i, 0)),
                pl.BlockSpec((1, d), lambda i, j: (0, 0)),
                pl.BlockSpec((1, 1, d), lambda i, j: (i // nb, 0, 0)),
                pl.BlockSpec((1, 1, d), lambda i, j: (i // nb, 0, 0)),
                pl.BlockSpec((None, d, tn), lambda i, j: (layer, 0, j))]
    args = [x, g.reshape(1, d), sc.reshape(-1, 1, d), sh.reshape(-1, 1, d), w3]
    out_specs = [pl.BlockSpec((tm, tn), lambda i, j: (i, j))]
    out_shape = [jax.ShapeDtypeStruct((m, n_cols), BF16)]
    if has_tail:
        in_specs.append(pl.BlockSpec((d, LANES), lambda i, j: (0, 0)))
        args.append(w_tail)
        out_specs.append(pl.BlockSpec((tm, LANES), lambda i, j: (i, 0)))
        out_shape.append(jax.ShapeDtypeStruct((m, LANES), BF16))
    res = pl.pallas_call(
        functools.partial(_nm_body, has_tail=has_tail),
        grid=(m // tm, n_cols // tn),
        in_specs=in_specs,
        out_specs=out_specs,
        out_shape=out_shape,
        scratch_shapes=[pltpu.VMEM((tm, d), BF16)],
        compiler_params=_params("arbitrary", "arbitrary"),
        name="norm_mod_matmul",
    )(*args)
    return res if has_tail else res[0]


def _mmres_body(a_ref, w_ref, x_ref, gt_ref, o_ref):
    o_ref[...] = x_ref[...] + gt_ref[0] * _dot(a_ref[...], w_ref[...].astype(BF16))


def _matmul_residual(a, w3, layer, x, gt, seq, tm=1024, tn=256):
    m, k = a.shape
    d = x.shape[1]
    nb = seq // tm
    return pl.pallas_call(
        _mmres_body,
        grid=(m // tm, d // tn),
        in_specs=[pl.BlockSpec((tm, k), lambda i, j: (i, 0)),
                  pl.BlockSpec((None, k, tn), lambda i, j: (layer, 0, j)),
                  pl.BlockSpec((tm, tn), lambda i, j: (i, j)),
                  pl.BlockSpec((1, 1, tn), lambda i, j: (i // nb, 0, j))],
        out_specs=pl.BlockSpec((tm, tn), lambda i, j: (i, j)),
        out_shape=jax.ShapeDtypeStruct((m, d), F32),
        compiler_params=_params("arbitrary", "arbitrary"),
        name="matmul_residual",
    )(a, w3, x, gt.reshape(-1, 1, d))


def _nm_swiglu_body(x_ref, g_ref, sc_ref, sh_ref, w1_ref, w3_ref, o_ref, h_scr):
    @pl.when(pl.program_id(1) == 0)
    def _():
        h_scr[...] = _norm_mod(x_ref[...], g_ref[...], sc_ref[0], sh_ref[0]).astype(BF16)

    h = h_scr[...]
    a = _dot(h, w1_ref[...].astype(BF16))
    b = _dot(h, w3_ref[...].astype(BF16))
    o_ref[...] = (a * _sigmoid(a) * b).astype(o_ref.dtype)


def _nm_swiglu(x, g, sc, sh, w1, w3, layer, seq, tm=1024, tf=256):
    m, d = x.shape
    f = w1.shape[2]
    nb = seq // tm
    return pl.pallas_call(
        _nm_swiglu_body,
        grid=(m // tm, f // tf),
        in_specs=[pl.BlockSpec((tm, d), lambda i, j: (i, 0)),
                  pl.BlockSpec((1, d), lambda i, j: (0, 0)),
                  pl.BlockSpec((1, 1, d), lambda i, j: (i // nb, 0, 0)),
                  pl.BlockSpec((1, 1, d), lambda i, j: (i // nb, 0, 0)),
                  pl.BlockSpec((None, d, tf), lambda i, j: (layer, 0, j)),
                  pl.BlockSpec((None, d, tf), lambda i, j: (layer, 0, j))],
        out_specs=pl.BlockSpec((tm, tf), lambda i, j: (i, j)),
        out_shape=jax.ShapeDtypeStruct((m, f), BF16),
        scratch_shapes=[pltpu.VMEM((tm, d), BF16)],
        compiler_params=_params("arbitrary", "arbitrary"),
        name="norm_mod_swiglu_up",
    )(x, g.reshape(1, d), sc.reshape(-1, 1, d), sh.reshape(-1, 1, d), w1, w3)


def _dsa_body(q_ref, qi_ref, k_ref, v_ref, tq_ref, tk_ref, pc_ref, pr_ref, tab_ref, gq_ref, gk_ref,
              o_ref,
              kn_scr, kke_scr, kko_scr, qst_scr, wib_scr, key_scr, qn_scr, m_scr, l_scr, acc_scr,
              *, topk, tq, seq):
    i = pl.program_id(1)
    nc = i + 1
    grp = A_HEADS // A_KV_HEADS
    hd = A_HEAD_DIM
    lane = lax.broadcasted_iota(I32, (1, LANES), 1)
    row = lax.broadcasted_iota(I32, (tq, tq), 0)
    col = lax.broadcasted_iota(I32, (tq, tq), 1)

    @pl.when(i == 0)
    def _prepare_keys():
        gk = gk_ref[...]

        def body(r, carry):
            rows = pl.ds(pl.multiple_of(r * tq, tq), tq)
            for kh in range(A_KV_HEADS):
                kb = k_ref[rows, kh * hd:(kh + 1) * hd].astype(F32)
                ms = jnp.mean(kb * kb, axis=-1, keepdims=True)
                kn_scr[rows, kh * hd:(kh + 1) * hd] = ((kb * lax.rsqrt(ms + EPS)) * gk).astype(BF16)
            t = tk_ref[rows, :].astype(F32)
            kke_scr[rows, :] = jnp.where(lane < IDX_DIM, t, 0.0).astype(BF16)
            kko_scr[rows, :] = jnp.where(lane >= IDX_DIM, pltpu.roll(t, IDX_DIM, 1), 0.0).astype(BF16)
            return carry

        lax.fori_loop(0, seq // tq, body, 0)

    for j in range(IDX_HEADS // 2):
        qst_scr[j * tq:(j + 1) * tq, :] = qi_ref[:, j * LANES:(j + 1) * LANES]
    tqf = tq_ref[...].astype(F32)
    w_scale = (IDX_DIM ** -0.5) * (IDX_HEADS ** -0.5)
    for h in range(IDX_HEADS):
        wib_scr[h] = jnp.broadcast_to(tqf[:, IDX_DIM + h:IDX_DIM + h + 1], (tq, LANES)) * w_scale

    def idx_body(c, carry):
        rows = pl.ds(pl.multiple_of(c * tq, tq), tq)
        qst = qst_scr[...]
        re = _dot_nt(qst, kke_scr[rows, :])
        ro = _dot_nt(qst, kko_scr[rows, :])
        acc = jnp.zeros((tq, tq), F32)
        for j in range(IDX_HEADS // 2):
            we = wib_scr[2 * j]
            wo = wib_scr[2 * j + 1]
            we = jnp.concatenate([we] * (tq // LANES), axis=1)
            wo = jnp.concatenate([wo] * (tq // LANES), axis=1)
            acc = acc + jnp.maximum(re[j * tq:(j + 1) * tq], 0.0) * we
            acc = acc + jnp.maximum(ro[j * tq:(j + 1) * tq], 0.0) * wo
        bits = pltpu.bitcast(acc, I32)
        key = bits ^ ((bits >> 31) & 0x7FFFFFFF)
        causal = (c < i) | (col <= row)
        key_scr[c] = jnp.where(causal, key, INT_MIN)
        return carry

    lax.fori_loop(0, nc, idx_body, 0)

    def bit_body(bi, t_u):
        cand_u = t_u | lax.shift_left(jnp.int32(1), 31 - bi)
        cand_s = cand_u ^ INT_MIN

        def cnt_body(c, cnt):
            ge = jnp.where(key_scr[c] >= cand_s, 1, 0)
            for s in range(tq // LANES):
                cnt = cnt + ge[:, s * LANES:(s + 1) * LANES]
            return cnt

        cnt = lax.fori_loop(0, nc, cnt_body, jnp.zeros((tq, LANES), I32))
        total = jnp.sum(cnt, axis=1, keepdims=True)
        return jnp.where(total >= topk, cand_u, t_u)

    nbits = jnp.where(nc * tq > topk, 32, 0)
    t_u = lax.fori_loop(0, nbits, bit_body, jnp.zeros((tq, 1), I32))
    thr = t_u ^ INT_MIN

    qpos = pc_ref[...]

    def bkt_body(c, carry):
        kpos = pr_ref[0, c]
        bkt = _t5_bucket(qpos - kpos)
        causal = (c < i) | (col <= row)
        sel = (key_scr[c] >= thr) & causal
        key_scr[c] = jnp.where(sel, bkt, N_BUCKETS)
        return carry

    lax.fori_loop(0, nc, bkt_body, 0)

    gq = gq_ref[...]
    scale = hd ** -0.5
    for g in range(A_KV_HEADS):
        for hh in range(grp):
            h = g * grp + hh
            qh = q_ref[:, h * hd:(h + 1) * hd].astype(F32)
            ms = jnp.mean(qh * qh, axis=-1, keepdims=True)
            qn_scr[hh * tq:(hh + 1) * tq, :] = (((qh * lax.rsqrt(ms + EPS)) * gq) * scale).astype(BF16)
        m_scr[...] = jnp.full(m_scr.shape, NEG, F32)
        l_scr[...] = jnp.zeros(l_scr.shape, F32)
        acc_scr[...] = jnp.zeros(acc_scr.shape, F32)

        def att_body(c, carry, g=g):
            rows = pl.ds(pl.multiple_of(c * tq, tq), tq)
            kc = kn_scr[rows, g * hd:(g + 1) * hd]
            vc = v_ref[rows, g * hd:(g + 1) * hd]
            s = _dot_nt(qn_scr[...], kc)
            bkt = key_scr[c]
            parts = []
            for hh in range(grp):
                h = g * grp + hh
                bias = _bias_lookup(tab_ref[h:h + 1, :], bkt, tq)
                parts.append(s[hh * tq:(hh + 1) * tq] + bias)
            s = jnp.concatenate(parts, axis=0)
            m_old = m_scr[...]
            m_new = jnp.maximum(m_old, jnp.max(s, axis=1, keepdims=True))
            p = jnp.exp(s - m_new)
            alpha = jnp.exp(m_old - m_new)
            l_scr[...] = alpha * l_scr[...] + jnp.sum(p, axis=1, keepdims=True)
            acc_scr[...] = alpha * acc_scr[...] + _dot(p.astype(BF16), vc)
            m_scr[...] = m_new
            return carry

        lax.fori_loop(0, nc, att_body, 0)
        o = acc_scr[...] * (1.0 / l_scr[...])
        for hh in range(grp):
            h = g * grp + hh
            o_ref[:, h * hd:(h + 1) * hd] = o[hh * tq:(hh + 1) * tq].astype(o_ref.dtype)


def _dsa_attention(qkv, tail, pos_col, pos_row, tab, gq, gk, batch, seq, tq=256):
    nq = seq // tq
    a_q = A_HEADS * A_HEAD_DIM
    a_kv = A_KV_HEADS * A_HEAD_DIM
    a_qi = IDX_HEADS * IDX_DIM
    topk = min(TOPK_MAX, seq // 4)
    grp = A_HEADS // A_KV_HEADS
    body = functools.partial(_dsa_body, topk=topk, tq=tq, seq=seq)
    return pl.pallas_call(
        body,
        grid=(batch, nq),
        in_specs=[
            pl.BlockSpec((tq, a_q), lambda b, i: (b * nq + i, 0)),
            pl.BlockSpec((tq, a_qi), lambda b, i: (b * nq + i, (a_q + 2 * a_kv) // a_qi)),
            pl.BlockSpec((seq, a_kv), lambda b, i: (b, a_q // a_kv)),
            pl.BlockSpec((seq, a_kv), lambda b, i: (b, a_q // a_kv + 1)),
            pl.BlockSpec((tq, LANES), lambda b, i: (b * nq + i, 0)),
            pl.BlockSpec((seq, LANES), lambda b, i: (b, 0)),
            pl.BlockSpec((tq, 1), lambda b, i: (b * nq + i, 0)),
            pl.BlockSpec((1, nq, 1, tq), lambda b, i: (b, 0, 0, 0)),
            pl.BlockSpec((A_HEADS, LANES), lambda b, i: (0, 0)),
            pl.BlockSpec((1, A_HEAD_DIM), lambda b, i: (0, 0)),
            pl.BlockSpec((1, A_HEAD_DIM), lambda b, i: (0, 0)),
        ],
        out_specs=pl.BlockSpec((tq, a_q), lambda b, i: (b * nq + i, 0)),
        out_shape=jax.ShapeDtypeStruct((batch * seq, a_q), BF16),
        scratch_shapes=[
            pltpu.VMEM((seq, a_kv), BF16),
            pltpu.VMEM((seq, LANES), BF16),
            pltpu.VMEM((seq, LANES), BF16),
            pltpu.VMEM((IDX_HEADS // 2 * tq, LANES), BF16),
            pltpu.VMEM((IDX_HEADS, tq, LANES), F32),
            pltpu.VMEM((nq, tq, tq), I32),
            pltpu.VMEM((grp * tq, A_HEAD_DIM), BF16),
            pltpu.VMEM((grp * tq, 1), F32),
            pltpu.VMEM((grp * tq, 1), F32),
            pltpu.VMEM((grp * tq, A_HEAD_DIM), F32),
        ],
        compiler_params=_params("arbitrary", "arbitrary"),
        name="dsa_attention",
    )(qkv, qkv, qkv, qkv, tail, tail, pos_col, pos_row, tab, gq, gk)


def _band_body(*refs, cls_len, wk, use_prev, tq):
    if use_prev:
        (q_ref, kc_ref, vc_ref, kp_ref, vp_ref, pc_ref, prc_ref, prp_ref,
         tab_ref, gq_ref, gk_ref, o_ref, lse_ref, bk_scr) = refs
    else:
        (q_ref, kc_ref, vc_ref, pc_ref, prc_ref,
         tab_ref, gq_ref, gk_ref, o_ref, lse_ref, bk_scr) = refs
    t = pl.program_id(1)
    hd = B_HEAD_DIM
    pw = 2 * hd
    shift = cls_len.bit_length() - 1
    lane = lax.broadcasted_iota(I32, (1, pw), 1)
    lo = lane < hd
    qpos = pc_ref[...]
    fq = t * tq + lax.broadcasted_iota(I32, (tq, 1), 0)

    def bucket_mask(fk, kpos):
        same = ((fq + cls_len) >> shift) == ((fk + cls_len) >> shift)
        d = (fq & (cls_len - 1)) - (fk & (cls_len - 1))
        ok = same & (d >= 0) & (d <= wk)
        return jnp.where(ok, _t5_bucket(qpos - kpos), N_BUCKETS)

    off = LANES if use_prev else 0
    if use_prev:
        fk = t * tq - LANES + lax.broadcasted_iota(I32, (1, LANES), 1)
        bk_scr[:, 0:LANES] = bucket_mask(fk, prp_ref[0])
    fk = t * tq + lax.broadcasted_iota(I32, (1, tq), 1)
    bk_scr[:, off:off + tq] = bucket_mask(fk, prc_ref[0])

    gq = gq_ref[...]
    gk = gk_ref[...]
    scale = hd ** -0.5

    def pair_norm(x, g):
        sq = x * x
        ms_lo = jnp.sum(jnp.where(lo, sq, 0.0), axis=-1, keepdims=True) * (1.0 / hd)
        ms_hi = jnp.sum(jnp.where(lo, 0.0, sq), axis=-1, keepdims=True) * (1.0 / hd)
        inv = jnp.where(lo, lax.rsqrt(ms_lo + EPS), lax.rsqrt(ms_hi + EPS))
        return (x * inv) * g

    for p in range(B_HEADS // 2):
        cols = slice(p * pw, (p + 1) * pw)
        qn = pair_norm(q_ref[:, cols].astype(F32), gq) * scale
        if use_prev:
            kraw = jnp.concatenate([kp_ref[:, cols], kc_ref[:, cols]], axis=0)
            v = jnp.concatenate([vp_ref[:, cols], vc_ref[:, cols]], axis=0)
        else:
            kraw = kc_ref[:, cols]
            v = vc_ref[:, cols]
        kn = pair_norm(kraw.astype(F32), gk).astype(BF16)
        bkt = bk_scr[...]
        o_pair = jnp.zeros((tq, pw), F32)
        lse_pair = jnp.zeros((tq, pw), F32)
        for half in range(2):
            h = 2 * p + half
            keep = lo if half == 0 else jnp.logical_not(lo)
            qh = jnp.where(keep, qn, 0.0).astype(BF16)
            vh = jnp.where(keep, v, jnp.zeros_like(v))
            s = _dot_nt(qh, kn) + _bias_lookup(tab_ref[h:h + 1, :], bkt, tq)
            m = jnp.max(s, axis=1, keepdims=True)
            e = jnp.exp(s - m)
            l = jnp.sum(e, axis=1, keepdims=True)
            o_pair = o_pair + _dot(e.astype(BF16), vh) * (1.0 / l)
            lse_pair = jnp.where(keep, m + jnp.log(l), lse_pair)
        o_ref[:, cols] = o_pair.astype(o_ref.dtype)
        lse_ref[:, cols] = lse_pair


def _band_attention(q_arr, q_col, k_arr, k_col, v_arr, v_col, pos, tab, gq2, gk2,
                    batch, seq, dilation, window, tq=512):
    cls_len = seq // dilation
    wk = window // dilation
    assert cls_len & (cls_len - 1) == 0 and wk <= LANES
    tq = min(tq, seq)
    use_prev = cls_len > tq
    assert use_prev or tq % cls_len == 0
    nt = seq // tq
    w = B_HEADS * B_HEAD_DIM
    sub = tq // LANES
    pos_col = pos.reshape(batch * seq, 1)
    pos_row = pos.reshape(batch * nt, 1, tq)
    in_specs = [pl.BlockSpec((tq, w), lambda b, t: (b * nt + t, q_col)),
                pl.BlockSpec((tq, w), lambda b, t: (b * nt + t, k_col)),
                pl.BlockSpec((tq, w), lambda b, t: (b * nt + t, v_col))]
    args = [q_arr, k_arr, v_arr]
    if use_prev:
        prev = lambda b, t: jnp.maximum((b * nt + t) * sub - 1, 0)
        in_specs += [pl.BlockSpec((LANES, w), lambda b, t: (prev(b, t), k_col)),
                     pl.BlockSpec((LANES, w), lambda b, t: (prev(b, t), v_col))]
        args += [k_arr, v_arr]
    in_specs += [pl.BlockSpec((tq, 1), lambda b, t: (b * nt + t, 0)),
                 pl.BlockSpec((1, 1, tq), lambda b, t: (b * nt + t, 0, 0))]
    args += [pos_col, pos_row]
    if use_prev:
        in_specs.append(pl.BlockSpec((1, 1, LANES), lambda b, t: (prev(b, t), 0, 0)))
        args.append(pos.reshape(batch * seq // LANES, 1, LANES))
    in_specs += [pl.BlockSpec((B_HEADS, LANES), lambda b, t: (0, 0)),
                 pl.BlockSpec((1, LANES), lambda b, t: (0, 0)),
                 pl.BlockSpec((1, LANES), lambda b, t: (0, 0))]
    args += [tab, gq2, gk2]
    nk = tq + (LANES if use_prev else 0)
    body = functools.partial(_band_body, cls_len=cls_len, wk=wk, use_prev=use_prev, tq=tq)
    return pl.pallas_call(
        body,
        grid=(batch, nt),
        in_specs=in_specs,
        out_specs=[pl.BlockSpec((tq, w), lambda b, t: (b * nt + t, 0)),
                   pl.BlockSpec((tq, w), lambda b, t: (b * nt + t, 0))],
        out_shape=[jax.ShapeDtypeStruct((batch * seq, w), BF16),
                   jax.ShapeDtypeStruct((batch * seq, w), F32)],
        scratch_shapes=[pltpu.VMEM((tq, nk), I32)],
        compiler_params=_params("arbitrary", "arbitrary"),
        name="dilated_attention",
    )(*args)


def _merge_body(o0, o1, o2, l0, l1, l2, w_ref, x_ref, gt_ref, out_ref, a_scr):
    @pl.when(pl.program_id(1) == 0)
    def _():
        m = jnp.maximum(jnp.maximum(l0[...], l1[...]), l2[...])
        w0 = jnp.exp(l0[...] - m)
        w1 = jnp.exp(l1[...] - m)
        w2 = jnp.exp(l2[...] - m)
        num = w0 * o0[...].astype(F32) + w1 * o1[...].astype(F32) + w2 * o2[...].astype(F32)
        a_scr[...] = (num / (w0 + w1 + w2)).astype(BF16)

    out_ref[...] = x_ref[...] + gt_ref[0] * _dot(a_scr[...], w_ref[...].astype(BF16))


def _merge_out(os_, ls_, w3, layer, x, gt, seq, tm=512, tn=512):
    m, k = os_[0].shape
    d = x.shape[1]
    nb = seq // tm
    row = pl.BlockSpec((tm, k), lambda i, j: (i, 0))
    return pl.pallas_call(
        _merge_body,
        grid=(m // tm, d // tn),
        in_specs=[row] * 6 + [
            pl.BlockSpec((None, k, tn), lambda i, j: (layer, 0, j)),
            pl.BlockSpec((tm, tn), lambda i, j: (i, j)),
            pl.BlockSpec((1, 1, tn), lambda i, j: (i // nb, 0, j))],
        out_specs=pl.BlockSpec((tm, tn), lambda i, j: (i, j)),
        out_shape=jax.ShapeDtypeStruct((m, d), F32),
        scratch_shapes=[pltpu.VMEM((tm, k), BF16)],
        compiler_params=_params("arbitrary", "arbitrary"),
        name="merge_out_proj",
    )(*os_, *ls_, w3, x, gt.reshape(-1, 1, d))


def _router_body(x_ref, g_ref, sc_ref, sh_ref, rh_ref, rl_ref, rb_ref, h_ref, rt_ref):
    h = _norm_mod(x_ref[...], g_ref[...], sc_ref[0], sh_ref[0])
    h_ref[...] = h
    hh = h.astype(BF16)
    hl = (h - hh.astype(F32)).astype(BF16)
    logits = _dot(hh, rh_ref[...]) + _dot(hl, rh_ref[...]) + _dot(hh, rl_ref[...]) + rb_ref[...]
    lane = lax.broadcasted_iota(I32, logits.shape, 1)
    logits = jnp.where(lane < N_EXPERTS, logits, NEG)
    v1 = jnp.max(logits, axis=1, keepdims=True)
    i1 = jnp.min(jnp.where(logits == v1, lane, LANES), axis=1, keepdims=True)
    rest = jnp.where(lane == i1, NEG, logits)
    v2 = jnp.max(rest, axis=1, keepdims=True)
    i2 = jnp.min(jnp.where(rest == v2, lane, LANES), axis=1, keepdims=True)
    e = jnp.exp(v2 - v1)
    g1 = 1.0 / (1.0 + e)
    g2 = e * g1
    rt_ref[...] = jnp.where(lane == 0, i1.astype(F32),
                            jnp.where(lane == 1, i2.astype(F32),
                                      jnp.where(lane == 2, g1, jnp.where(lane == 3, g2, 0.0))))


def _router(x, g, sc, sh, rw, rb, seq, tm=512):
    m, d = x.shape
    nb = seq // tm
    ne = rw.shape[1]
    rw_p = jnp.zeros((d, LANES), F32).at[:, :ne].set(rw)
    rh = rw_p.astype(BF16)
    rl = (rw_p - rh.astype(F32)).astype(BF16)
    rb_p = jnp.zeros((1, LANES), F32).at[0, :ne].set(rb)
    return pl.pallas_call(
        _router_body,
        grid=(m // tm,),
        in_specs=[pl.BlockSpec((tm, d), lambda i: (i, 0)),
                  pl.BlockSpec((1, d), lambda i: (0, 0)),
                  pl.BlockSpec((1, 1, d), lambda i: (i // nb, 0, 0)),
                  pl.BlockSpec((1, 1, d), lambda i: (i // nb, 0, 0)),
                  pl.BlockSpec((d, LANES), lambda i: (0, 0)),
                  pl.BlockSpec((d, LANES), lambda i: (0, 0)),
                  pl.BlockSpec((1, LANES), lambda i: (0, 0))],
        out_specs=[pl.BlockSpec((tm, d), lambda i: (i, 0)),
                   pl.BlockSpec((tm, LANES), lambda i: (i, 0))],
        out_shape=[jax.ShapeDtypeStruct((m, d), F32),
                   jax.ShapeDtypeStruct((m, LANES), F32)],
        compiler_params=_params("arbitrary"),
        name="router_top2",
    )(x, g.reshape(1, d), sc.reshape(-1, 1, d), sh.reshape(-1, 1, d), rh, rl, rb_p)


def _row_copy(src_hbm, idx, buf, r, sem):
    return pltpu.make_async_copy(src_hbm.at[pl.ds(idx, 1), :], buf.at[pl.ds(r, 1), :], sem)


def _gather_body(src_ref, h_hbm, o_ref, buf, sem, *, rows):
    base = pl.program_id(0) * rows

    def issue(r, carry):
        _row_copy(h_hbm, src_ref[base + r], buf, r, sem).start()
        return carry

    lax.fori_loop(0, rows, issue, 0)

    def drain(r, carry):
        _row_copy(h_hbm, 0, buf, r, sem).wait()
        return carry

    lax.fori_loop(0, rows, drain, 0)
    o_ref[...] = buf[...].astype(o_ref.dtype)


def _gather_rows(h, src, rows):
    r_total = src.shape[0]
    d = h.shape[1]
    return pl.pallas_call(
        functools.partial(_gather_body, rows=rows),
        grid_spec=pltpu.PrefetchScalarGridSpec(
            num_scalar_prefetch=1,
            grid=(r_total // rows,),
            in_specs=[pl.BlockSpec(memory_space=pl.ANY)],
            out_specs=pl.BlockSpec((rows, d), lambda i, s: (i, 0)),
            scratch_shapes=[pltpu.VMEM((rows, d), F32), pltpu.SemaphoreType.DMA(())],
        ),
        out_shape=jax.ShapeDtypeStruct((r_total, d), BF16),
        compiler_params=_params("arbitrary"),
        name="moe_dispatch_gather",
    )(src, h)


def _moe_body(te_ref, tv_ref, hs_ref, w1_ref, w3_ref, w2_ref, o_ref, *, ncol):
    t = pl.program_id(0)
    f = pl.program_id(1)
    valid = tv_ref[t]
    d = o_ref.shape[1]
    cw = d // ncol

    @pl.when(valid > 0)
    def _():
        h = hs_ref[...]
        a = _dot(h, w1_ref[...].astype(BF16))
        b = _dot(h, w3_ref[...].astype(BF16))
        u = (a * _sigmoid(a) * b).astype(BF16)
        w2 = w2_ref[...].astype(BF16)

        @pl.when(f == 0)
        def _():
            for n in range(ncol):
                o_ref[:, n * cw:(n + 1) * cw] = _dot(u, w2[:, n * cw:(n + 1) * cw])

        @pl.when(f > 0)
        def _():
            for n in range(ncol):
                o_ref[:, n * cw:(n + 1) * cw] += _dot(u, w2[:, n * cw:(n + 1) * cw])

    @pl.when((valid == 0) & (f == 0))
    def _():
        o_ref[...] = jnp.zeros(o_ref.shape, o_ref.dtype)


def _moe_experts(hs, tile_expert, tile_valid, w1, w3, w2, tm, tf=256, ncol=4):
    r_total, d = hs.shape
    n_tiles = r_total // tm
    nf = w1.shape[2] // tf

    def w13_map(t, f, te, tv):
        return (te[t], 0, jnp.where(tv[t] > 0, f, nf - 1))

    def w2_map(t, f, te, tv):
        return (te[t], jnp.where(tv[t] > 0, f, nf - 1), 0)

    return pl.pallas_call(
        functools.partial(_moe_body, ncol=ncol),
        grid_spec=pltpu.PrefetchScalarGridSpec(
            num_scalar_prefetch=2,
            grid=(n_tiles, nf),
            in_specs=[pl.BlockSpec((tm, d), lambda t, f, te, tv: (t, 0)),
                      pl.BlockSpec((None, d, tf), w13_map),
                      pl.BlockSpec((None, d, tf), w13_map),
                      pl.BlockSpec((None, tf, d), w2_map)],
            out_specs=pl.BlockSpec((tm, d), lambda t, f, te, tv: (t, 0)),
        ),
        out_shape=jax.ShapeDtypeStruct((r_total, d), F32),
        compiler_params=_params("arbitrary", "arbitrary"),
        name="moe_experts",
    )(tile_expert, tile_valid, hs, w1, w3, w2)


def _combine_body(p1_ref, p2_ref, y_hbm, x_ref, gt_ref, rt_ref, o_ref, buf_a, buf_b, sem, *, rows):
    base = pl.program_id(0) * rows

    def issue(r, carry):
        _row_copy(y_hbm, p1_ref[base + r], buf_a, r, sem).start()
        _row_copy(y_hbm, p2_ref[base + r], buf_b, r, sem).start()
        return carry

    lax.fori_loop(0, rows, issue, 0)

    def drain(r, carry):
        _row_copy(y_hbm, 0, buf_a, r, sem).wait()
        _row_copy(y_hbm, 0, buf_b, r, sem).wait()
        return carry

    lax.fori_loop(0, rows, drain, 0)
    rt = rt_ref[...]
    g1 = rt[:, 2:3]
    g2 = rt[:, 3:4]
    o_ref[...] = x_ref[...] + gt_ref[0] * (g1 * buf_a[...] + g2 * buf_b[...])


def _combine(ys, p1, p2, x, gt, route, seq, rows=256):
    m, d = x.shape
    nb = seq // rows
    return pl.pallas_call(
        functools.partial(_combine_body, rows=rows),
        grid_spec=pltpu.PrefetchScalarGridSpec(
            num_scalar_prefetch=2,
            grid=(m // rows,),
            in_specs=[pl.BlockSpec(memory_space=pl.ANY),
                      pl.BlockSpec((rows, d), lambda i, a, b: (i, 0)),
                      pl.BlockSpec((1, 1, d), lambda i, a, b: (i // nb, 0, 0)),
                      pl.BlockSpec((rows, LANES), lambda i, a, b: (i, 0))],
            out_specs=pl.BlockSpec((rows, d), lambda i, a, b: (i, 0)),
            scratch_shapes=[pltpu.VMEM((rows, d), F32), pltpu.VMEM((rows, d), F32),
                            pltpu.SemaphoreType.DMA(())],
        ),
        out_shape=jax.ShapeDtypeStruct((m, d), F32),
        compiler_params=_params("arbitrary"),
        name="moe_combine",
    )(p1, p2, ys, x, gt.reshape(-1, 1, d), route)


def _routing_tables(route, tm, n_tiles):
    t = route.shape[0]
    experts = route[:, :2].astype(I32).reshape(-1)
    onehot = (experts[:, None] == jnp.arange(N_EXPERTS, dtype=I32)[None, :]).astype(I32)
    csum = jnp.cumsum(onehot, axis=0)
    rank = jnp.sum(csum * onehot, axis=1) - 1
    counts = csum[-1]
    tiles = (counts + tm - 1) // tm
    tend = jnp.cumsum(tiles)
    tstart = tend - tiles
    slot = tstart[experts] * tm + rank
    token = jnp.arange(2 * t, dtype=I32) // 2
    src = jnp.zeros((n_tiles * tm,), I32).at[slot].set(token)
    tile_id = jnp.arange(n_tiles, dtype=I32)
    te = jnp.sum((tile_id[:, None] >= tend[None, :]).astype(I32), axis=1)
    active = tile_id < tend[-1]
    last_e = jnp.sum((tend[-1] - 1 >= tend).astype(I32))
    te = jnp.where(active, te, last_e)
    tv = jnp.where(active, jnp.clip(counts[te] - (tile_id - tstart[te]) * tm, 0, tm), 0)
    slots = slot.reshape(t, 2)
    return src, te.astype(I32), tv.astype(I32), slots[:, 0], slots[:, 1]


def _bias_table(rel_bias):
    h = rel_bias.shape[1]
    return jnp.full((h, LANES), NEG, F32).at[:, :N_BUCKETS].set(rel_bias.T)


def _to_classes(a, batch, seq, r):
    if r == 1:
        return a
    w = a.shape[-1]
    return a.reshape(batch, seq // r, r, w).transpose(0, 2, 1, 3).reshape(batch * seq, w)


def _from_classes(a, batch, seq, r):
    if r == 1:
        return a
    w = a.shape[-1]
    return a.reshape(batch, r, seq // r, w).transpose(0, 2, 1, 3).reshape(batch * seq, w)


def kernel(x, c, positions, rel_bias, w_mod, b_mod, g_attn, g_ffn, a_w_in, a_w_out, a_g_qn, a_g_kn,
           kv_w_mod, kv_b_mod, kv_g, kv_w, b_g_kn, b_w_q, b_w_out, b_g_qn, ffn_w1, ffn_w3, ffn_w2,
           moe_router, moe_router_b, moe_w1, moe_w3, moe_w2):
    batch, seq, d = x.shape
    m = batch * seq
    x2 = x.reshape(m, d)
    positions = positions.astype(I32)

    c8 = jnp.zeros((8, d), F32).at[:batch].set(c)
    mod0 = _mod_call(c8, w_mod, 0, b_mod)[:batch]
    mod1 = _mod_call(c8, w_mod, 1, b_mod)[:batch]
    kvm = _mod_call(c8, kv_w_mod[None], 0, kv_b_mod[None])[:batch]
    sh1_0, sc1_0, gt1_0, sh2_0, sc2_0, gt2_0 = jnp.split(mod0, 6, axis=-1)
    sh1_1, sc1_1, gt1_1, sh2_1, sc2_1, gt2_1 = jnp.split(mod1, 6, axis=-1)
    kv_sh, kv_sc = jnp.split(kvm, 2, axis=-1)

    tab = _bias_table(rel_bias)

    a_main = A_HEADS * A_HEAD_DIM + 2 * A_KV_HEADS * A_HEAD_DIM + IDX_HEADS * IDX_DIM
    n_tail = IDX_DIM + IDX_HEADS
    w_tail = jnp.zeros((d, LANES), F32).at[:, :n_tail].set(a_w_in[0, :, a_main:a_main + n_tail])
    qkv, tail = _nm_matmul(x2, g_attn[0], sc1_0, sh1_0, a_w_in, 0, a_main, seq, w_tail=w_tail)
    tq = min(256, seq)
    nq = seq // tq
    attn = _dsa_attention(qkv, tail, positions.reshape(m, 1), positions.reshape(batch, nq, 1, tq), tab,
                          a_g_qn[0].reshape(1, -1), a_g_kn[0].reshape(1, -1), batch, seq, tq=tq)
    x2 = _matmul_residual(attn, a_w_out, 0, x2, gt1_0, seq)
    u = _nm_swiglu(x2, g_ffn[0], sc2_0, sh2_0, ffn_w1, ffn_w3, 0, seq)
    x2 = _matmul_residual(u, ffn_w2, 0, x2, gt2_0, seq)

    b_q = len(B_DILATIONS) * B_HEADS * B_HEAD_DIM
    kvall = _nm_matmul(x2, kv_g, kv_sc, kv_sh, kv_w[None], 0, 2 * b_q, seq)
    qall = _nm_matmul(x2, g_attn[1], sc1_1, sh1_1, b_w_q, 0, b_q, seq)
    gq2 = jnp.tile(b_g_qn[0], 2).reshape(1, LANES)
    gk2 = jnp.tile(b_g_kn, 2).reshape(1, LANES)
    w = B_HEADS * B_HEAD_DIM
    ng = len(B_DILATIONS)
    outs, lses = [], []
    for g, (window, r) in enumerate(B_DILATIONS):
        if r == 1:
            q_arr, q_col, k_arr, k_col, v_arr, v_col = qall, g, kvall, g, kvall, ng + g
            pos_g = positions
        else:
            q_arr = _to_classes(qall[:, g * w:(g + 1) * w], batch, seq, r)
            k_arr = _to_classes(kvall[:, g * w:(g + 1) * w], batch, seq, r)
            v_arr = _to_classes(kvall[:, (ng + g) * w:(ng + g + 1) * w], batch, seq, r)
            q_col = k_col = v_col = 0
            pos_g = positions.reshape(batch, seq // r, r).transpose(0, 2, 1).reshape(batch, seq)
        o_g, lse_g = _band_attention(q_arr, q_col, k_arr, k_col, v_arr, v_col, pos_g, tab, gq2, gk2,
                                     batch, seq, r, window)
        outs.append(_from_classes(o_g, batch, seq, r))
        lses.append(_from_classes(lse_g, batch, seq, r))
    x2 = _merge_out(outs, lses, b_w_out, 0, x2, gt1_1, seq)

    h, route = _router(x2, g_ffn[1], sc2_1, sh2_1, moe_router[0], moe_router_b[0], seq)
    share = (2 * m) // N_EXPERTS
    tm = -(-(share * 17 // 32) // 64) * 64
    n_tiles = (2 * m) // tm + N_EXPERTS
    src, te, tv, p1, p2 = _routing_tables(route, tm, n_tiles)
    hs = _gather_rows(h, src, tm // 4)
    ys = _moe_experts(hs, te, tv, moe_w1.reshape(moe_w1.shape[1:]), moe_w3.reshape(moe_w3.shape[1:]),
                      moe_w2.reshape(moe_w2.shape[1:]), tm)
    out = _combine(ys, p1, p2, x2, gt2_1, route, seq)
    return out.reshape(batch, seq, d)
```

```python
import functools
import math

import jax
import jax.numpy as jnp
from jax import lax
from jax.experimental import pallas as pl
from jax.experimental.pallas import tpu as pltpu

F32 = jnp.float32
BF16 = jnp.bfloat16
I32 = jnp.int32

EPS = 1e-6
NEG = -1e30
INT_MIN = -(2 ** 31)
LOG2E = 1.0 / math.log(2.0)

A_HEADS, A_KV_HEADS, A_HEAD_DIM = 16, 4, 128
IDX_HEADS, IDX_DIM = 16, 64
TOPK_MAX = 256
B_DILATIONS = ((128, 1), (512, 4), (2048, 16))
B_HEADS, B_HEAD_DIM = 16, 64
N_BUCKETS, MAX_DISTANCE = 32, 2048
N_EXPERTS = 8
LANES = 128

VMEM_LIMIT_BYTES = 56 * 1024 * 1024

_NT = (((1,), (1,)), ((), ()))


def _params(*sem):
    return pltpu.CompilerParams(dimension_semantics=sem, vmem_limit_bytes=VMEM_LIMIT_BYTES)


def _dot(a, b):
    return jnp.dot(a, b, preferred_element_type=F32)


def _dot_nt(a, b):
    return lax.dot_general(a, b, _NT, preferred_element_type=F32)


def _sigmoid(x):
    return 1.0 / (1.0 + jnp.exp(-x))


def _t5_bucket(rel):
    n = jnp.maximum(rel, 0)
    max_exact = N_BUCKETS // 2
    nf = jnp.maximum(n, 1).astype(F32)
    large = max_exact + (jnp.log(nf / max_exact) / math.log(MAX_DISTANCE / max_exact)
                         * (N_BUCKETS - max_exact)).astype(I32)
    large = jnp.minimum(large, N_BUCKETS - 1)
    return jnp.where(n < max_exact, n, large)


def _norm_mod(x, g, sc, sh):
    y = x * lax.rsqrt(jnp.mean(x * x, axis=-1, keepdims=True) + EPS)
    return (y * g) * (1.0 + sc) + sh


def _bias_lookup(tab_row, bkt, rows):
    tab = jnp.broadcast_to(tab_row, (rows, LANES))
    parts = [jnp.take_along_axis(tab, bkt[:, c * LANES:(c + 1) * LANES], axis=1)
             for c in range(bkt.shape[1] // LANES)]
    return parts[0] if len(parts) == 1 else jnp.concatenate(parts, axis=1)


def _mod_body(c_ref, w_ref, b_ref, o_ref):
    c = c_ref[...]
    cs = c * _sigmoid(c)
    o_ref[...] = _dot(cs.astype(BF16), w_ref[...].astype(BF16)) + b_ref[...]


def _mod_call(c8, w3, layer, b2):
    _, d, n = w3.shape
    tn = 1024
    return pl.pallas_call(
        _mod_body,
        grid=(n // tn,),
        in_specs=[pl.BlockSpec((8, d), lambda j: (0, 0)),
                  pl.BlockSpec((None, d, tn), lambda j: (layer, 0, j)),
                  pl.BlockSpec((None, 1, tn), lambda j: (layer, 0, j))],
        out_specs=pl.BlockSpec((8, tn), lambda j: (0, j)),
        out_shape=jax.ShapeDtypeStruct((8, n), F32),
        compiler_params=_params("arbitrary"),
        name="adaln_mod",
    )(c8, w3, b2.reshape(b2.shape[0], 1, n))


def _nm_body(x_ref, g_ref, sc_ref, sh_ref, w_ref, *rest, has_tail):
    if has_tail:
        wt_ref, o_ref, ot_ref, h_scr = rest
    else:
        o_ref, h_scr = rest

    @pl.when(pl.program_id(1) == 0)
    def _():
        h = _norm_mod(x_ref[...], g_ref[...], sc_ref[0], sh_ref[0]).astype(BF16)
        h_scr[...] = h
        if has_tail:
            ot_ref[...] = _dot(h, wt_ref[...].astype(BF16)).astype(ot_ref.dtype)

    o_ref[...] = _dot(h_scr[...], w_ref[...].astype(BF16)).astype(o_ref.dtype)


def _nm_matmul(x, g, sc, sh, w3, layer, n_cols, seq, w_tail=None, tm=1024, tn=512):
    m, d = x.shape
    nb = seq // tm
    has_tail = w_tail is not None
    in_specs = [pl.BlockSpec((tm, d), lambda i, j: (i, 0)),
                pl.BlockSpec((1, d), lambda i, j: (0, 0)),
                pl.BlockSpec((1, 1, d), lambda i, j: (i // nb, 0, 0)),
                pl.BlockSpec((1, 1, d), lambda i, j: (i // nb, 0, 0)),
                pl.BlockSpec((None, d, tn), lambda i, j: (layer, 0, j))]
    args = [x, g.reshape(1, d), sc.reshape(-1, 1, d), sh.reshape(-1, 1, d), w3]
    out_specs = [pl.BlockSpec((tm, tn), lambda i, j: (i, j))]
    out_shape = [jax.ShapeDtypeStruct((m, n_cols), BF16)]
    if has_tail:
        in_specs.append(pl.BlockSpec((d, LANES), lambda i, j: (0, 0)))
        args.append(w_tail)
        out_specs.append(pl.BlockSpec((tm, LANES), lambda i, j: (i, 0)))
        out_shape.append(jax.ShapeDtypeStruct((m, LANES), BF16))
    res = pl.pallas_call(
        functools.partial(_nm_body, has_tail=has_tail),
        grid=(m // tm, n_cols // tn),
        in_specs=in_specs,
        out_specs=out_specs,
        out_shape=out_shape,
        scratch_shapes=[pltpu.VMEM((tm, d), BF16)],
        compiler_params=_params("arbitrary", "arbitrary"),
        name="norm_mod_matmul",
    )(*args)
    return res if has_tail else res[0]


def _mmres_body(a_ref, w_ref, x_ref, gt_ref, o_ref):
    o_ref[...] = x_ref[...] + gt_ref[0] * _dot(a_ref[...], w_ref[...].astype(BF16))


def _matmul_residual(a, w3, layer, x, gt, seq, tm=1024, tn=256):
    m, k = a.shape
    d = x.shape[1]
    nb = seq // tm
    return pl.pallas_call(
        _mmres_body,
        grid=(m // tm, d // tn),
        in_specs=[pl.BlockSpec((tm, k), lambda i, j: (i, 0)),
                  pl.BlockSpec((None, k, tn), lambda i, j: (layer, 0, j)),
                  pl.BlockSpec((tm, tn), lambda i, j: (i, j)),
                  pl.BlockSpec((1, 1, tn), lambda i, j: (i // nb, 0, j))],
        out_specs=pl.BlockSpec((tm, tn), lambda i, j: (i, j)),
        out_shape=jax.ShapeDtypeStruct((m, d), F32),
        compiler_params=_params("arbitrary", "arbitrary"),
        name="matmul_residual",
    )(a, w3, x, gt.reshape(-1, 1, d))


def _nm_swiglu_body(x_ref, g_ref, sc_ref, sh_ref, w1_ref, w3_ref, o_ref, h_scr):
    @pl.when(pl.program_id(1) == 0)
    def _():
        h_scr[...] = _norm_mod(x_ref[...], g_ref[...], sc_ref[0], sh_ref[0]).astype(BF16)

    h = h_scr[...]
    a = _dot(h, w1_ref[...].astype(BF16))
    b = _dot(h, w3_ref[...].astype(BF16))
    o_ref[...] = (a * _sigmoid(a) * b).astype(o_ref.dtype)


def _nm_swiglu(x, g, sc, sh, w1, w3, layer, seq, tm=1024, tf=256):
    m, d = x.shape
    f = w1.shape[2]
    nb = seq // tm
    return pl.pallas_call(
        _nm_swiglu_body,
        grid=(m // tm, f // tf),
        in_specs=[pl.BlockSpec((tm, d), lambda i, j: (i, 0)),
                  pl.BlockSpec((1, d), lambda i, j: (0, 0)),
                  pl.BlockSpec((1, 1, d), lambda i, j: (i // nb, 0, 0)),
                  pl.BlockSpec((1, 1, d), lambda i, j: (i // nb, 0, 0)),
                  pl.BlockSpec((None, d, tf), lambda i, j: (layer, 0, j)),
                  pl.BlockSpec((None, d, tf), lambda i, j: (layer, 0, j))],
        out_specs=pl.BlockSpec((tm, tf), lambda i, j: (i, j)),
        out_shape=jax.ShapeDtypeStruct((m, f), BF16),
        scratch_shapes=[pltpu.VMEM((tm, d), BF16)],
        compiler_params=_params("arbitrary", "arbitrary"),
        name="norm_mod_swiglu_up",
    )(x, g.reshape(1, d), sc.reshape(-1, 1, d), sh.reshape(-1, 1, d), w1, w3)


def _dsa_body(q_ref, qi_ref, k_ref, v_ref, tq_ref, tk_ref, pr_ref, pc_ref, tab_ref, gq_ref, gk_ref,
              o_ref,
              kn_scr, vt_scr, kke_scr, kko_scr, qst_scr, wib_scr, key_scr, qn_scr, m_scr, l_scr, acc_scr,
              *, topk, tq, seq):
    i = pl.program_id(1)
    nc = i + 1
    grp = A_HEADS // A_KV_HEADS
    hd = A_HEAD_DIM
    lane = lax.broadcasted_iota(I32, (1, LANES), 1)
    krow = lax.broadcasted_iota(I32, (tq, tq), 0)
    qcol = lax.broadcasted_iota(I32, (tq, tq), 1)

    @pl.when(i == 0)
    def _prepare_keys():
        gk = gk_ref[...]

        def body(r, carry):
            rows = pl.ds(pl.multiple_of(r * tq, tq), tq)
            for kh in range(A_KV_HEADS):
                cols = slice(kh * hd, (kh + 1) * hd)
                kb = k_ref[rows, cols].astype(F32)
                ms = jnp.mean(kb * kb, axis=-1, keepdims=True)
                kn_scr[rows, cols] = ((kb * lax.rsqrt(ms + EPS)) * gk).astype(BF16)
                vt_scr[r, cols, :] = v_ref[rows, cols].astype(F32).T.astype(BF16)
            t = tk_ref[rows, :].astype(F32)
            kke_scr[rows, :] = jnp.where(lane < IDX_DIM, t, 0.0).astype(BF16)
            kko_scr[rows, :] = jnp.where(lane >= IDX_DIM, pltpu.roll(t, IDX_DIM, 1), 0.0).astype(BF16)
            return carry

        lax.fori_loop(0, seq // tq, body, 0)

    for j in range(IDX_HEADS // 2):
        qst_scr[j * tq:(j + 1) * tq, :] = qi_ref[:, j * LANES:(j + 1) * LANES]
    w_scale = (IDX_DIM ** -0.5) * (IDX_HEADS ** -0.5)
    wib_scr[...] = tq_ref[...].astype(F32).T[IDX_DIM:IDX_DIM + IDX_HEADS, :] * w_scale

    def idx_body(c, carry):
        rows = pl.ds(pl.multiple_of(c * tq, tq), tq)
        qst = qst_scr[...]
        re = _dot_nt(kke_scr[rows, :], qst)
        ro = _dot_nt(kko_scr[rows, :], qst)
        acc = jnp.zeros((tq, tq), F32)
        for j in range(IDX_HEADS // 2):
            acc = acc + jnp.maximum(re[:, j * tq:(j + 1) * tq], 0.0) * wib_scr[2 * j:2 * j + 1, :]
            acc = acc + jnp.maximum(ro[:, j * tq:(j + 1) * tq], 0.0) * wib_scr[2 * j + 1:2 * j + 2, :]
        bits = pltpu.bitcast(acc, I32)
        key = bits ^ ((bits >> 31) & 0x7FFFFFFF)
        causal = (c < i) | (krow <= qcol)
        key_scr[c] = jnp.where(causal, key, INT_MIN)
        return carry

    lax.fori_loop(0, nc, idx_body, 0)

    def bit_body(bi, t_u):
        cand_u = t_u | lax.shift_left(jnp.int32(1), 31 - bi)
        cand_s = cand_u ^ INT_MIN

        def cnt_body(c, cnt):
            ge = jnp.where(key_scr[c] >= cand_s, 1, 0)
            return cnt + jnp.sum(ge.reshape(tq // 8, 8, tq), axis=0)

        cnt = lax.fori_loop(0, nc, cnt_body, jnp.zeros((8, tq), I32))
        total = jnp.sum(cnt, axis=0, keepdims=True)
        return jnp.where(total >= topk, cand_u, t_u)

    nbits = jnp.where(nc * tq > topk, 32, 0)
    t_u = lax.fori_loop(0, nbits, bit_body, jnp.zeros((1, tq), I32))
    thr = t_u ^ INT_MIN

    qpos = pr_ref[0, i]

    def bkt_body(c, carry):
        rows = pl.ds(pl.multiple_of(c * tq, tq), tq)
        bkt = _t5_bucket(qpos - pc_ref[rows, :])
        causal = (c < i) | (krow <= qcol)
        sel = (key_scr[c] >= thr) & causal
        key_scr[c] = jnp.where(sel, bkt, N_BUCKETS)
        return carry

    lax.fori_loop(0, nc, bkt_body, 0)

    gq = gq_ref[...]
    scale = (hd ** -0.5) * LOG2E
    for g in range(A_KV_HEADS):
        for hh in range(grp):
            h = g * grp + hh
            qh = q_ref[:, h * hd:(h + 1) * hd].astype(F32)
            ms = jnp.mean(qh * qh, axis=-1, keepdims=True)
            qn_scr[hh * tq:(hh + 1) * tq, :] = (((qh * lax.rsqrt(ms + EPS)) * gq) * scale).astype(BF16)
        m_scr[...] = jnp.full(m_scr.shape, NEG, F32)
        l_scr[...] = jnp.zeros(l_scr.shape, F32)
        acc_scr[...] = jnp.zeros(acc_scr.shape, F32)

        def att_body(c, carry, g=g):
            rows = pl.ds(pl.multiple_of(c * tq, tq), tq)
            s = _dot_nt(kn_scr[rows, g * hd:(g + 1) * hd], qn_scr[...])
            bkt = key_scr[c]
            parts = []
            for hh in range(grp):
                h = g * grp + hh
                parts.append(s[:, hh * tq:(hh + 1) * tq] + _bias_lookup(tab_ref[h:h + 1, :], bkt, tq))
            s = jnp.concatenate(parts, axis=1)
            m_old = m_scr[...]
            m_new = jnp.maximum(m_old, jnp.max(s, axis=0, keepdims=True))
            p = jnp.exp2(s - m_new)
            alpha = jnp.exp2(m_old - m_new)
            l_scr[...] = alpha * l_scr[...] + jnp.sum(p, axis=0, keepdims=True)
            acc_scr[...] = alpha * acc_scr[...] + _dot(vt_scr[c, g * hd:(g + 1) * hd, :], p.astype(BF16))
            m_scr[...] = m_new
            return carry

        lax.fori_loop(0, nc, att_body, 0)
        o = acc_scr[...] * (1.0 / l_scr[...])
        for hh in range(grp):
            h = g * grp + hh
            o_ref[:, h * hd:(h + 1) * hd] = o[:, hh * tq:(hh + 1) * tq].T.astype(o_ref.dtype)


def _dsa_attention(qkv, tail, pos, tab, gq, gk, batch, seq, tq=256):
    tq = min(tq, seq)
    nq = seq // tq
    a_q = A_HEADS * A_HEAD_DIM
    a_kv = A_KV_HEADS * A_HEAD_DIM
    a_qi = IDX_HEADS * IDX_DIM
    topk = min(TOPK_MAX, seq // 4)
    grp = A_HEADS // A_KV_HEADS
    body = functools.partial(_dsa_body, topk=topk, tq=tq, seq=seq)
    return pl.pallas_call(
        body,
        grid=(batch, nq),
        in_specs=[
            pl.BlockSpec((tq, a_q), lambda b, i: (b * nq + i, 0)),
            pl.BlockSpec((tq, a_qi), lambda b, i: (b * nq + i, (a_q + 2 * a_kv) // a_qi)),
            pl.BlockSpec((seq, a_kv), lambda b, i: (b, a_q // a_kv)),
            pl.BlockSpec((seq, a_kv), lambda b, i: (b, a_q // a_kv + 1)),
            pl.BlockSpec((tq, LANES), lambda b, i: (b * nq + i, 0)),
            pl.BlockSpec((seq, LANES), lambda b, i: (b, 0)),
            pl.BlockSpec((1, nq, 1, tq), lambda b, i: (b, 0, 0, 0)),
            pl.BlockSpec((seq, 1), lambda b, i: (b, 0)),
            pl.BlockSpec((A_HEADS, LANES), lambda b, i: (0, 0)),
            pl.BlockSpec((1, A_HEAD_DIM), lambda b, i: (0, 0)),
            pl.BlockSpec((1, A_HEAD_DIM), lambda b, i: (0, 0)),
        ],
        out_specs=pl.BlockSpec((tq, a_q), lambda b, i: (b * nq + i, 0)),
        out_shape=jax.ShapeDtypeStruct((batch * seq, a_q), BF16),
        scratch_shapes=[
            pltpu.VMEM((seq, a_kv), BF16),
            pltpu.VMEM((nq, a_kv, tq), BF16),
            pltpu.VMEM((seq, LANES), BF16),
            pltpu.VMEM((seq, LANES), BF16),
            pltpu.VMEM((IDX_HEADS // 2 * tq, LANES), BF16),
            pltpu.VMEM((IDX_HEADS, tq), F32),
            pltpu.VMEM((nq, tq, tq), I32),
            pltpu.VMEM((grp * tq, A_HEAD_DIM), BF16),
            pltpu.VMEM((1, grp * tq), F32),
            pltpu.VMEM((1, grp * tq), F32),
            pltpu.VMEM((A_HEAD_DIM, grp * tq), F32),
        ],
        compiler_params=_params("arbitrary", "arbitrary"),
        name="dsa_attention",
    )(qkv, qkv, qkv, qkv, tail, tail, pos.reshape(batch, nq, 1, tq), pos.reshape(batch * seq, 1),
      tab, gq, gk)


def _band_body(*refs, cls_len, wk, use_prev, tq):
    if use_prev:
        (q_ref, kc_ref, vc_ref, kp_ref, vp_ref, pc_ref, prc_ref, prp_ref,
         tab_ref, gq_ref, gk_ref, o_ref, lse_ref, bk_scr) = refs
    else:
        (q_ref, kc_ref, vc_ref, pc_ref, prc_ref,
         tab_ref, gq_ref, gk_ref, o_ref, lse_ref, bk_scr) = refs
    t = pl.program_id(1)
    hd = B_HEAD_DIM
    pw = 2 * hd
    shift = cls_len.bit_length() - 1
    lane = lax.broadcasted_iota(I32, (1, pw), 1)
    lo = lane < hd
    qpos = pc_ref[...]
    fq = t * tq + lax.broadcasted_iota(I32, (tq, 1), 0)

    def bucket_mask(fk, kpos):
        same = ((fq + cls_len) >> shift) == ((fk + cls_len) >> shift)
        d = (fq & (cls_len - 1)) - (fk & (cls_len - 1))
        ok = same & (d >= 0) & (d <= wk)
        return jnp.where(ok, _t5_bucket(qpos - kpos), N_BUCKETS)

    off = LANES if use_prev else 0
    if use_prev:
        fk = t * tq - LANES + lax.broadcasted_iota(I32, (1, LANES), 1)
        bk_scr[:, 0:LANES] = bucket_mask(fk, prp_ref[0])
    fk = t * tq + lax.broadcasted_iota(I32, (1, tq), 1)
    bk_scr[:, off:off + tq] = bucket_mask(fk, prc_ref[0])

    gq = gq_ref[...]
    gk = gk_ref[...]
    scale = hd ** -0.5

    def pair_norm(x, g):
        sq = x * x
        ms_lo = jnp.sum(jnp.where(lo, sq, 0.0), axis=-1, keepdims=True) * (1.0 / hd)
        ms_hi = jnp.sum(jnp.where(lo, 0.0, sq), axis=-1, keepdims=True) * (1.0 / hd)
        inv = jnp.where(lo, lax.rsqrt(ms_lo + EPS), lax.rsqrt(ms_hi + EPS))
        return (x * inv) * g

    for p in range(B_HEADS // 2):
        cols = slice(p * pw, (p + 1) * pw)
        qn = pair_norm(q_ref[:, cols].astype(F32), gq) * scale
        if use_prev:
            kraw = jnp.concatenate([kp_ref[:, cols], kc_ref[:, cols]], axis=0)
            v = jnp.concatenate([vp_ref[:, cols], vc_ref[:, cols]], axis=0)
        else:
            kraw = kc_ref[:, cols]
            v = vc_ref[:, cols]
        kn = pair_norm(kraw.astype(F32), gk).astype(BF16)
        bkt = bk_scr[...]
        o_pair = jnp.zeros((tq, pw), F32)
        lse_pair = jnp.zeros((tq, pw), F32)
        for half in range(2):
            h = 2 * p + half
            keep = lo if half == 0 else jnp.logical_not(lo)
            qh = jnp.where(keep, qn, 0.0).astype(BF16)
            vh = jnp.where(keep, v, jnp.zeros_like(v))
            s = _dot_nt(qh, kn) + _bias_lookup(tab_ref[h:h + 1, :], bkt, tq)
            m = jnp.max(s, axis=1, keepdims=True)
            e = jnp.exp(s - m)
            l = jnp.sum(e, axis=1, keepdims=True)
            o_pair = o_pair + _dot(e.astype(BF16), vh) * (1.0 / l)
            lse_pair = jnp.where(keep, m + jnp.log(l), lse_pair)
        o_ref[:, cols] = o_pair.astype(o_ref.dtype)
        lse_ref[:, cols] = lse_pair


def _band_attention(q_arr, q_col, k_arr, k_col, v_arr, v_col, pos, tab, gq2, gk2,
                    batch, seq, dilation, window, tq=512):
    cls_len = seq // dilation
    wk = window // dilation
    assert cls_len & (cls_len - 1) == 0 and wk <= LANES
    tq = min(tq, seq)
    use_prev = cls_len > tq
    assert use_prev or tq % cls_len == 0
    nt = seq // tq
    w = B_HEADS * B_HEAD_DIM
    sub = tq // LANES
    pos_col = pos.reshape(batch * seq, 1)
    pos_row = pos.reshape(batch * nt, 1, tq)
    in_specs = [pl.BlockSpec((tq, w), lambda b, t: (b * nt + t, q_col)),
                pl.BlockSpec((tq, w), lambda b, t: (b * nt + t, k_col)),
                pl.BlockSpec((tq, w), lambda b, t: (b * nt + t, v_col))]
    args = [q_arr, k_arr, v_arr]
    if use_prev:
        prev = lambda b, t: jnp.maximum((b * nt + t) * sub - 1, 0)
        in_specs += [pl.BlockSpec((LANES, w), lambda b, t: (prev(b, t), k_col)),
                     pl.BlockSpec((LANES, w), lambda b, t: (prev(b, t), v_col))]
        args += [k_arr, v_arr]
    in_specs += [pl.BlockSpec((tq, 1), lambda b, t: (b * nt + t, 0)),
                 pl.BlockSpec((1, 1, tq), lambda b, t: (b * nt + t, 0, 0))]
    args += [pos_col, pos_row]
    if use_prev:
        in_specs.append(pl.BlockSpec((1, 1, LANES), lambda b, t: (prev(b, t), 0, 0)))
        args.append(pos.reshape(batch * seq // LANES, 1, LANES))
    in_specs += [pl.BlockSpec((B_HEADS, LANES), lambda b, t: (0, 0)),
                 pl.BlockSpec((1, LANES), lambda b, t: (0, 0)),
                 pl.BlockSpec((1, LANES), lambda b, t: (0, 0))]
    args += [tab, gq2, gk2]
    nk = tq + (LANES if use_prev else 0)
    body = functools.partial(_band_body, cls_len=cls_len, wk=wk, use_prev=use_prev, tq=tq)
    return pl.pallas_call(
        body,
        grid=(batch, nt),
        in_specs=in_specs,
        out_specs=[pl.BlockSpec((tq, w), lambda b, t: (b * nt + t, 0)),
                   pl.BlockSpec((tq, w), lambda b, t: (b * nt + t, 0))],
        out_shape=[jax.ShapeDtypeStruct((batch * seq, w), BF16),
                   jax.ShapeDtypeStruct((batch * seq, w), F32)],
        scratch_shapes=[pltpu.VMEM((tq, nk), I32)],
        compiler_params=_params("arbitrary", "arbitrary"),
        name="dilated_attention",
    )(*args)


def _merge_body(o0, o1, o2, l0, l1, l2, w_ref, x_ref, gt_ref, out_ref, a_scr):
    @pl.when(pl.program_id(1) == 0)
    def _():
        m = jnp.maximum(jnp.maximum(l0[...], l1[...]), l2[...])
        w0 = jnp.exp(l0[...] - m)
        w1 = jnp.exp(l1[...] - m)
        w2 = jnp.exp(l2[...] - m)
        num = w0 * o0[...].astype(F32) + w1 * o1[...].astype(F32) + w2 * o2[...].astype(F32)
        a_scr[...] = (num / (w0 + w1 + w2)).astype(BF16)

    out_ref[...] = x_ref[...] + gt_ref[0] * _dot(a_scr[...], w_ref[...].astype(BF16))


def _merge_out(os_, ls_, w3, layer, x, gt, seq, tm=512, tn=512):
    m, k = os_[0].shape
    d = x.shape[1]
    nb = seq // tm
    row = pl.BlockSpec((tm, k), lambda i, j: (i, 0))
    return pl.pallas_call(
        _merge_body,
        grid=(m // tm, d // tn),
        in_specs=[row] * 6 + [
            pl.BlockSpec((None, k, tn), lambda i, j: (layer, 0, j)),
            pl.BlockSpec((tm, tn), lambda i, j: (i, j)),
            pl.BlockSpec((1, 1, tn), lambda i, j: (i // nb, 0, j))],
        out_specs=pl.BlockSpec((tm, tn), lambda i, j: (i, j)),
        out_shape=jax.ShapeDtypeStruct((m, d), F32),
        scratch_shapes=[pltpu.VMEM((tm, k), BF16)],
        compiler_params=_params("arbitrary", "arbitrary"),
        name="merge_out_proj",
    )(*os_, *ls_, w3, x, gt.reshape(-1, 1, d))


def _router_body(x_ref, g_ref, sc_ref, sh_ref, rh_ref, rl_ref, rb_ref, h_ref, rt_ref):
    h = _norm_mod(x_ref[...], g_ref[...], sc_ref[0], sh_ref[0])
    h_ref[...] = h
    hh = h.astype(BF16)
    hl = (h - hh.astype(F32)).astype(BF16)
    logits = _dot(hh, rh_ref[...]) + _dot(hl, rh_ref[...]) + _dot(hh, rl_ref[...]) + rb_ref[...]
    lane = lax.broadcasted_iota(I32, logits.shape, 1)
    logits = jnp.where(lane < N_EXPERTS, logits, NEG)
    v1 = jnp.max(logits, axis=1, keepdims=True)
    i1 = jnp.min(jnp.where(logits == v1, lane, LANES), axis=1, keepdims=True)
    rest = jnp.where(lane == i1, NEG, logits)
    v2 = jnp.max(rest, axis=1, keepdims=True)
    i2 = jnp.min(jnp.where(rest == v2, lane, LANES), axis=1, keepdims=True)
    e = jnp.exp(v2 - v1)
    g1 = 1.0 / (1.0 + e)
    g2 = e * g1
    rt_ref[...] = jnp.where(lane == 0, i1.astype(F32),
                            jnp.where(lane == 1, i2.astype(F32),
                                      jnp.where(lane == 2, g1, jnp.where(lane == 3, g2, 0.0))))


def _router(x, g, sc, sh, rw, rb, seq, tm=512):
    m, d = x.shape
    nb = seq // tm
    ne = rw.shape[1]
    rw_p = jnp.zeros((d, LANES), F32).at[:, :ne].set(rw)
    rh = rw_p.astype(BF16)
    rl = (rw_p - rh.astype(F32)).astype(BF16)
    rb_p = jnp.zeros((1, LANES), F32).at[0, :ne].set(rb)
    return pl.pallas_call(
        _router_body,
        grid=(m // tm,),
        in_specs=[pl.BlockSpec((tm, d), lambda i: (i, 0)),
                  pl.BlockSpec((1, d), lambda i: (0, 0)),
                  pl.BlockSpec((1, 1, d), lambda i: (i // nb, 0, 0)),
                  pl.BlockSpec((1, 1, d), lambda i: (i // nb, 0, 0)),
                  pl.BlockSpec((d, LANES), lambda i: (0, 0)),
                  pl.BlockSpec((d, LANES), lambda i: (0, 0)),
                  pl.BlockSpec((1, LANES), lambda i: (0, 0))],
        out_specs=[pl.BlockSpec((tm, d), lambda i: (i, 0)),
                   pl.BlockSpec((tm, LANES), lambda i: (i, 0))],
        out_shape=[jax.ShapeDtypeStruct((m, d), F32),
                   jax.ShapeDtypeStruct((m, LANES), F32)],
        compiler_params=_params("arbitrary"),
        name="router_top2",
    )(x, g.reshape(1, d), sc.reshape(-1, 1, d), sh.reshape(-1, 1, d), rh, rl, rb_p)


def _row_copy(src_hbm, idx, buf, r, sem):
    return pltpu.make_async_copy(src_hbm.at[pl.ds(idx, 1), :], buf.at[pl.ds(r, 1), :], sem)


def _gather_body(src_ref, nact_ref, h_hbm, o_ref, buf, sem, *, rows):
    i = pl.program_id(0)
    n_active = nact_ref[0]

    def issue(step):
        slot = step % 2

        def body(r, carry):
            _row_copy(h_hbm, src_ref[step * rows + r], buf.at[slot], r, sem.at[slot]).start()
            return carry

        lax.fori_loop(0, rows, body, 0)

    @pl.when(i == 0)
    def _():
        issue(i)

    @pl.when((i + 1) * rows < n_active)
    def _():
        issue(i + 1)

    slot = i % 2

    @pl.when(i * rows < n_active)
    def _():
        def drain(r, carry):
            _row_copy(h_hbm, 0, buf.at[slot], r, sem.at[slot]).wait()
            return carry

        lax.fori_loop(0, rows, drain, 0)
        o_ref[...] = buf[slot].astype(o_ref.dtype)

    @pl.when(i * rows >= n_active)
    def _():
        o_ref[...] = jnp.zeros(o_ref.shape, o_ref.dtype)


def _gather_rows(h, src, n_active, rows):
    r_total = src.shape[0]
    d = h.shape[1]
    return pl.pallas_call(
        functools.partial(_gather_body, rows=rows),
        grid_spec=pltpu.PrefetchScalarGridSpec(
            num_scalar_prefetch=2,
            grid=(r_total // rows,),
            in_specs=[pl.BlockSpec(memory_space=pl.ANY)],
            out_specs=pl.BlockSpec((rows, d), lambda i, s, n: (i, 0)),
            scratch_shapes=[pltpu.VMEM((2, rows, d), F32), pltpu.SemaphoreType.DMA((2,))],
        ),
        out_shape=jax.ShapeDtypeStruct((r_total, d), BF16),
        compiler_params=_params("arbitrary"),
        name="moe_dispatch_gather",
    )(src, n_active, h)


def _moe_body(te_ref, tv_ref, hs_ref, w1_ref, w3_ref, w2_ref, o_ref, *, ncol):
    t = pl.program_id(0)
    f = pl.program_id(1)
    valid = tv_ref[t]
    d = o_ref.shape[1]
    cw = d // ncol

    @pl.when(valid > 0)
    def _():
        h = hs_ref[...]
        a = _dot(h, w1_ref[...].astype(BF16))
        b = _dot(h, w3_ref[...].astype(BF16))
        u = (a * _sigmoid(a) * b).astype(BF16)
        w2 = w2_ref[...].astype(BF16)

        @pl.when(f == 0)
        def _():
            for n in range(ncol):
                o_ref[:, n * cw:(n + 1) * cw] = _dot(u, w2[:, n * cw:(n + 1) * cw])

        @pl.when(f > 0)
        def _():
            for n in range(ncol):
                o_ref[:, n * cw:(n + 1) * cw] += _dot(u, w2[:, n * cw:(n + 1) * cw])

    @pl.when((valid == 0) & (f == 0))
    def _():
        o_ref[...] = jnp.zeros(o_ref.shape, o_ref.dtype)


def _moe_experts(hs, tile_expert, tile_valid, w1, w3, w2, tm, tf=256, ncol=4):
    r_total, d = hs.shape
    n_tiles = r_total // tm
    nf = w1.shape[2] // tf

    def w13_map(t, f, te, tv):
        return (te[t], 0, jnp.where(tv[t] > 0, f, nf - 1))

    def w2_map(t, f, te, tv):
        return (te[t], jnp.where(tv[t] > 0, f, nf - 1), 0)

    return pl.pallas_call(
        functools.partial(_moe_body, ncol=ncol),
        grid_spec=pltpu.PrefetchScalarGridSpec(
            num_scalar_prefetch=2,
            grid=(n_tiles, nf),
            in_specs=[pl.BlockSpec((tm, d), lambda t, f, te, tv: (t, 0)),
                      pl.BlockSpec((None, d, tf), w13_map),
                      pl.BlockSpec((None, d, tf), w13_map),
                      pl.BlockSpec((None, tf, d), w2_map)],
            out_specs=pl.BlockSpec((tm, d), lambda t, f, te, tv: (t, 0)),
        ),
        out_shape=jax.ShapeDtypeStruct((r_total, d), F32),
        compiler_params=_params("arbitrary", "arbitrary"),
        name="moe_experts",
    )(tile_expert, tile_valid, hs, w1, w3, w2)


def _combine_body(p1_ref, p2_ref, y_hbm, x_ref, gt_ref, rt_ref, o_ref, buf_a, buf_b, sem, *, rows):
    base = pl.program_id(0) * rows

    def issue(r, carry):
        _row_copy(y_hbm, p1_ref[base + r], buf_a, r, sem).start()
        _row_copy(y_hbm, p2_ref[base + r], buf_b, r, sem).start()
        return carry

    lax.fori_loop(0, rows, issue, 0)

    def drain(r, carry):
        _row_copy(y_hbm, 0, buf_a, r, sem).wait()
        _row_copy(y_hbm, 0, buf_b, r, sem).wait()
        return carry

    lax.fori_loop(0, rows, drain, 0)
    rt = rt_ref[...]
    g1 = rt[:, 2:3]
    g2 = rt[:, 3:4]
    o_ref[...] = x_ref[...] + gt_ref[0] * (g1 * buf_a[...] + g2 * buf_b[...])


def _combine(ys, p1, p2, x, gt, route, seq, rows=256):
    m, d = x.shape
    nb = seq // rows
    return pl.pallas_call(
        functools.partial(_combine_body, rows=rows),
        grid_spec=pltpu.PrefetchScalarGridSpec(
            num_scalar_prefetch=2,
            grid=(m // rows,),
            in_specs=[pl.BlockSpec(memory_space=pl.ANY),
                      pl.BlockSpec((rows, d), lambda i, a, b: (i, 0)),
                      pl.BlockSpec((1, 1, d), lambda i, a, b: (i // nb, 0, 0)),
                      pl.BlockSpec((rows, LANES), lambda i, a, b: (i, 0))],
            out_specs=pl.BlockSpec((rows, d), lambda i, a, b: (i, 0)),
            scratch_shapes=[pltpu.VMEM((rows, d), F32), pltpu.VMEM((rows, d), F32),
                            pltpu.SemaphoreType.DMA(())],
        ),
        out_shape=jax.ShapeDtypeStruct((m, d), F32),
        compiler_params=_params("arbitrary"),
        name="moe_combine",
    )(p1, p2, ys, x, gt.reshape(-1, 1, d), route)


def _routing_tables(route, tm, n_tiles):
    t = route.shape[0]
    experts = route[:, :2].astype(I32).reshape(-1)
    onehot = (experts[:, None] == jnp.arange(N_EXPERTS, dtype=I32)[None, :]).astype(I32)
    csum = jnp.cumsum(onehot, axis=0)
    rank = jnp.sum(csum * onehot, axis=1) - 1
    counts = csum[-1]
    tiles = (counts + tm - 1) // tm
    tend = jnp.cumsum(tiles)
    tstart = tend - tiles
    slot = tstart[experts] * tm + rank
    token = jnp.arange(2 * t, dtype=I32) // 2
    src = jnp.zeros((n_tiles * tm,), I32).at[slot].set(token)
    tile_id = jnp.arange(n_tiles, dtype=I32)
    te = jnp.sum((tile_id[:, None] >= tend[None, :]).astype(I32), axis=1)
    active = tile_id < tend[-1]
    last_e = jnp.sum((tend[-1] - 1 >= tend).astype(I32))
    te = jnp.where(active, te, last_e)
    tv = jnp.where(active, jnp.clip(counts[te] - (tile_id - tstart[te]) * tm, 0, tm), 0)
    slots = slot.reshape(t, 2)
    n_active = (tend[-1] * tm).astype(I32).reshape(1)
    return src, n_active, te.astype(I32), tv.astype(I32), slots[:, 0], slots[:, 1]


def _bias_table(rel_bias, scale=1.0):
    h = rel_bias.shape[1]
    return jnp.full((h, LANES), NEG, F32).at[:, :N_BUCKETS].set(rel_bias.T * scale)


def _to_classes(a, batch, seq, r):
    if r == 1:
        return a
    w = a.shape[-1]
    return a.reshape(batch, seq // r, r, w).transpose(0, 2, 1, 3).reshape(batch * seq, w)


def _from_classes(a, batch, seq, r):
    if r == 1:
        return a
    w = a.shape[-1]
    return a.reshape(batch, r, seq // r, w).transpose(0, 2, 1, 3).reshape(batch * seq, w)


def kernel(x, c, positions, rel_bias, w_mod, b_mod, g_attn, g_ffn, a_w_in, a_w_out, a_g_qn, a_g_kn,
           kv_w_mod, kv_b_mod, kv_g, kv_w, b_g_kn, b_w_q, b_w_out, b_g_qn, ffn_w1, ffn_w3, ffn_w2,
           moe_router, moe_router_b, moe_w1, moe_w3, moe_w2):
    batch, seq, d = x.shape
    m = batch * seq
    x2 = x.reshape(m, d)
    positions = positions.astype(I32)

    c8 = jnp.zeros((8, d), F32).at[:batch].set(c)
    mod0 = _mod_call(c8, w_mod, 0, b_mod)[:batch]
    mod1 = _mod_call(c8, w_mod, 1, b_mod)[:batch]
    kvm = _mod_call(c8, kv_w_mod[None], 0, kv_b_mod[None])[:batch]
    sh1_0, sc1_0, gt1_0, sh2_0, sc2_0, gt2_0 = jnp.split(mod0, 6, axis=-1)
    sh1_1, sc1_1, gt1_1, sh2_1, sc2_1, gt2_1 = jnp.split(mod1, 6, axis=-1)
    kv_sh, kv_sc = jnp.split(kvm, 2, axis=-1)

    tab = _bias_table(rel_bias)

    a_main = A_HEADS * A_HEAD_DIM + 2 * A_KV_HEADS * A_HEAD_DIM + IDX_HEADS * IDX_DIM
    n_tail = IDX_DIM + IDX_HEADS
    w_tail = jnp.zeros((d, LANES), F32).at[:, :n_tail].set(a_w_in[0, :, a_main:a_main + n_tail])
    qkv, tail = _nm_matmul(x2, g_attn[0], sc1_0, sh1_0, a_w_in, 0, a_main, seq, w_tail=w_tail)
    attn = _dsa_attention(qkv, tail, positions, _bias_table(rel_bias, LOG2E),
                          a_g_qn[0].reshape(1, -1), a_g_kn[0].reshape(1, -1), batch, seq)
    x2 = _matmul_residual(attn, a_w_out, 0, x2, gt1_0, seq)
    u = _nm_swiglu(x2, g_ffn[0], sc2_0, sh2_0, ffn_w1, ffn_w3, 0, seq)
    x2 = _matmul_residual(u, ffn_w2, 0, x2, gt2_0, seq)

    b_q = len(B_DILATIONS) * B_HEADS * B_HEAD_DIM
    kvall = _nm_matmul(x2, kv_g, kv_sc, kv_sh, kv_w[None], 0, 2 * b_q, seq)
    qall = _nm_matmul(x2, g_attn[1], sc1_1, sh1_1, b_w_q, 0, b_q, seq)
    gq2 = jnp.tile(b_g_qn[0], 2).reshape(1, LANES)
    gk2 = jnp.tile(b_g_kn, 2).reshape(1, LANES)
    w = B_HEADS * B_HEAD_DIM
    ng = len(B_DILATIONS)
    outs, lses = [], []
    for g, (window, r) in enumerate(B_DILATIONS):
        if r == 1:
            q_arr, q_col, k_arr, k_col, v_arr, v_col = qall, g, kvall, g, kvall, ng + g
            pos_g = positions
        else:
            q_arr = _to_classes(qall[:, g * w:(g + 1) * w], batch, seq, r)
            k_arr = _to_classes(kvall[:, g * w:(g + 1) * w], batch, seq, r)
            v_arr = _to_classes(kvall[:, (ng + g) * w:(ng + g + 1) * w], batch, seq, r)
            q_col = k_col = v_col = 0
            pos_g = positions.reshape(batch, seq // r, r).transpose(0, 2, 1).reshape(batch, seq)
        o_g, lse_g = _band_attention(q_arr, q_col, k_arr, k_col, v_arr, v_col, pos_g, tab, gq2, gk2,
                                     batch, seq, r, window)
        outs.append(_from_classes(o_g, batch, seq, r))
        lses.append(_from_classes(lse_g, batch, seq, r))
    x2 = _merge_out(outs, lses, b_w_out, 0, x2, gt1_1, seq)

    h, route = _router(x2, g_ffn[1], sc2_1, sh2_1, moe_router[0], moe_router_b[0], seq)
    share = (2 * m) // N_EXPERTS
    tm = -(-(share * 17 // 32) // 64) * 64
    n_tiles = (2 * m) // tm + N_EXPERTS
    src, n_active, te, tv, p1, p2 = _routing_tables(route, tm, n_tiles)
    hs = _gather_rows(h, src, n_active, tm // 4)
    ys = _moe_experts(hs, te, tv, moe_w1.reshape(moe_w1.shape[1:]), moe_w3.reshape(moe_w3.shape[1:]),
                      moe_w2.reshape(moe_w2.shape[1:]), tm)
    out = _combine(ys, p1, p2, x2, gt2_1, route, seq)
    return out.reshape(batch, seq, d)
```

```python
import functools
import math

import jax
import jax.numpy as jnp
from jax import lax
from jax.experimental import pallas as pl
from jax.experimental.pallas import tpu as pltpu

F32 = jnp.float32
BF16 = jnp.bfloat16
I32 = jnp.int32

EPS = 1e-6
NEG = -1e30
INT_MIN = -(2 ** 31)
LOG2E = 1.0 / math.log(2.0)

A_HEADS, A_KV_HEADS, A_HEAD_DIM = 16, 4, 128
IDX_HEADS, IDX_DIM = 16, 64
TOPK_MAX = 256
B_DILATIONS = ((128, 1), (512, 4), (2048, 16))
B_HEADS, B_HEAD_DIM = 16, 64
N_BUCKETS, MAX_DISTANCE = 32, 2048
N_EXPERTS = 8
LANES = 128

VMEM_LIMIT_BYTES = 56 * 1024 * 1024

_NT = (((1,), (1,)), ((), ()))


def _params(*sem):
    return pltpu.CompilerParams(dimension_semantics=sem, vmem_limit_bytes=VMEM_LIMIT_BYTES)


def _dot(a, b):
    return jnp.dot(a, b, preferred_element_type=F32)


def _dot_nt(a, b):
    return lax.dot_general(a, b, _NT, preferred_element_type=F32)


def _sigmoid(x):
    return 1.0 / (1.0 + jnp.exp(-x))


def _t5_bucket(rel):
    n = jnp.maximum(rel, 0)
    max_exact = N_BUCKETS // 2
    nf = jnp.maximum(n, 1).astype(F32)
    large = max_exact + (jnp.log(nf / max_exact) / math.log(MAX_DISTANCE / max_exact)
                         * (N_BUCKETS - max_exact)).astype(I32)
    large = jnp.minimum(large, N_BUCKETS - 1)
    return jnp.where(n < max_exact, n, large)


def _norm_mod(x, g, sc, sh):
    y = x * lax.rsqrt(jnp.mean(x * x, axis=-1, keepdims=True) + EPS)
    return (y * g) * (1.0 + sc) + sh


def _bias_lookup(tab_row, bkt, rows):
    tab = jnp.broadcast_to(tab_row, (rows, LANES))
    parts = [jnp.take_along_axis(tab, bkt[:, c * LANES:(c + 1) * LANES], axis=1)
             for c in range(bkt.shape[1] // LANES)]
    return parts[0] if len(parts) == 1 else jnp.concatenate(parts, axis=1)


def _mod_body(c_ref, w_ref, b_ref, o_ref):
    c = c_ref[...]
    cs = c * _sigmoid(c)
    o_ref[...] = _dot(cs.astype(BF16), w_ref[...].astype(BF16)) + b_ref[...]


def _mod_call(c8, w3, layer, b2):
    _, d, n = w3.shape
    tn = 1024
    return pl.pallas_call(
        _mod_body,
        grid=(n // tn,),
        in_specs=[pl.BlockSpec((8, d), lambda j: (0, 0)),
                  pl.BlockSpec((None, d, tn), lambda j: (layer, 0, j)),
                  pl.BlockSpec((None, 1, tn), lambda j: (layer, 0, j))],
        out_specs=pl.BlockSpec((8, tn), lambda j: (0, j)),
        out_shape=jax.ShapeDtypeStruct((8, n), F32),
        compiler_params=_params("arbitrary"),
        name="adaln_mod",
    )(c8, w3, b2.reshape(b2.shape[0], 1, n))


def _nm_body(x_ref, g_ref, sc_ref, sh_ref, w_ref, *rest, has_tail):
    if has_tail:
        wt_ref, o_ref, ot_ref, h_scr = rest
    else:
        o_ref, h_scr = rest

    @pl.when(pl.program_id(1) == 0)
    def _():
        h = _norm_mod(x_ref[...], g_ref[...], sc_ref[0], sh_ref[0]).astype(BF16)
        h_scr[...] = h
        if has_tail:
            ot_ref[...] = _dot(h, wt_ref[...].astype(BF16)).astype(ot_ref.dtype)

    o_ref[...] = _dot(h_scr[...], w_ref[...].astype(BF16)).astype(o_ref.dtype)


def _nm_matmul(x, g, sc, sh, w3, layer, n_cols, seq, w_tail=None, tm=2048, tn=512):
    m, d = x.shape
    tm = min(tm, seq)
    nb = seq // tm
    has_tail = w_tail is not None
    in_specs = [pl.BlockSpec((tm, d), lambda i, j: (i, 0), pipeline_mode=pl.Buffered(1)),
                pl.BlockSpec((1, d), lambda i, j: (0, 0)),
                pl.BlockSpec((1, 1, d), lambda i, j: (i // nb, 0, 0)),
                pl.BlockSpec((1, 1, d), lambda i, j: (i // nb, 0, 0)),
                pl.BlockSpec((None, d, tn), lambda i, j: (layer, 0, j))]
    args = [x, g.reshape(1, d), sc.reshape(-1, 1, d), sh.reshape(-1, 1, d), w3]
    out_specs = [pl.BlockSpec((tm, tn), lambda i, j: (i, j))]
    out_shape = [jax.ShapeDtypeStruct((m, n_cols), BF16)]
    if has_tail:
        in_specs.append(pl.BlockSpec((d, LANES), lambda i, j: (0, 0)))
        args.append(w_tail)
        out_specs.append(pl.BlockSpec((tm, LANES), lambda i, j: (i, 0)))
        out_shape.append(jax.ShapeDtypeStruct((m, LANES), BF16))
    res = pl.pallas_call(
        functools.partial(_nm_body, has_tail=has_tail),
        grid=(m // tm, n_cols // tn),
        in_specs=in_specs,
        out_specs=out_specs,
        out_shape=out_shape,
        scratch_shapes=[pltpu.VMEM((tm, d), BF16)],
        compiler_params=_params("arbitrary", "arbitrary"),
        name="norm_mod_matmul",
    )(*args)
    return res if has_tail else res[0]


def _mmres_body(a_ref, w_ref, x_ref, gt_ref, o_ref):
    o_ref[...] = x_ref[...] + gt_ref[0] * _dot(a_ref[...], w_ref[...].astype(BF16))


def _matmul_residual(a, w3, layer, x, gt, seq, tm=2048, tn=256):
    m, k = a.shape
    d = x.shape[1]
    tm = min(tm, seq)
    nb = seq // tm
    return pl.pallas_call(
        _mmres_body,
        grid=(m // tm, d // tn),
        in_specs=[pl.BlockSpec((tm, k), lambda i, j: (i, 0), pipeline_mode=pl.Buffered(1)),
                  pl.BlockSpec((None, k, tn), lambda i, j: (layer, 0, j)),
                  pl.BlockSpec((tm, tn), lambda i, j: (i, j)),
                  pl.BlockSpec((1, 1, tn), lambda i, j: (i // nb, 0, j))],
        out_specs=pl.BlockSpec((tm, tn), lambda i, j: (i, j)),
        out_shape=jax.ShapeDtypeStruct((m, d), F32),
        compiler_params=_params("arbitrary", "arbitrary"),
        name="matmul_residual",
    )(a, w3, x, gt.reshape(-1, 1, d))


def _nm_swiglu_body(x_ref, g_ref, sc_ref, sh_ref, w1_ref, w3_ref, o_ref, h_scr):
    @pl.when(pl.program_id(1) == 0)
    def _():
        h_scr[...] = _norm_mod(x_ref[...], g_ref[...], sc_ref[0], sh_ref[0]).astype(BF16)

    h = h_scr[...]
    a = _dot(h, w1_ref[...].astype(BF16))
    b = _dot(h, w3_ref[...].astype(BF16))
    o_ref[...] = (a * _sigmoid(a) * b).astype(o_ref.dtype)


def _nm_swiglu(x, g, sc, sh, w1, w3, layer, seq, tm=2048, tf=256):
    m, d = x.shape
    f = w1.shape[2]
    tm = min(tm, seq)
    nb = seq // tm
    return pl.pallas_call(
        _nm_swiglu_body,
        grid=(m // tm, f // tf),
        in_specs=[pl.BlockSpec((tm, d), lambda i, j: (i, 0), pipeline_mode=pl.Buffered(1)),
                  pl.BlockSpec((1, d), lambda i, j: (0, 0)),
                  pl.BlockSpec((1, 1, d), lambda i, j: (i // nb, 0, 0)),
                  pl.BlockSpec((1, 1, d), lambda i, j: (i // nb, 0, 0)),
                  pl.BlockSpec((None, d, tf), lambda i, j: (layer, 0, j)),
                  pl.BlockSpec((None, d, tf), lambda i, j: (layer, 0, j))],
        out_specs=pl.BlockSpec((tm, tf), lambda i, j: (i, j)),
        out_shape=jax.ShapeDtypeStruct((m, f), BF16),
        scratch_shapes=[pltpu.VMEM((tm, d), BF16)],
        compiler_params=_params("arbitrary", "arbitrary"),
        name="norm_mod_swiglu_up",
    )(x, g.reshape(1, d), sc.reshape(-1, 1, d), sh.reshape(-1, 1, d), w1, w3)


def _dsa_body(q_ref, qi_ref, k_ref, v_ref, tq_ref, tk_ref, pr_ref, pc_ref, tab_ref, gq_ref, gk_ref,
              o_ref,
              kn_scr, vt_scr, kke_scr, kko_scr, qst_scr, wib_scr, key_scr, qn_scr, m_scr, l_scr, acc_scr,
              *, topk, tq, seq):
    i = pl.program_id(1)
    nc = i + 1
    grp = A_HEADS // A_KV_HEADS
    hd = A_HEAD_DIM
    lane = lax.broadcasted_iota(I32, (1, LANES), 1)
    krow = lax.broadcasted_iota(I32, (tq, tq), 0)
    qcol = lax.broadcasted_iota(I32, (tq, tq), 1)

    @pl.when(i == 0)
    def _prepare_keys():
        gk = gk_ref[...]

        def body(r, carry):
            rows = pl.ds(pl.multiple_of(r * tq, tq), tq)
            for kh in range(A_KV_HEADS):
                cols = slice(kh * hd, (kh + 1) * hd)
                kb = k_ref[rows, cols].astype(F32)
                ms = jnp.mean(kb * kb, axis=-1, keepdims=True)
                kn_scr[rows, cols] = ((kb * lax.rsqrt(ms + EPS)) * gk).astype(BF16)
                vt_scr[r, cols, :] = v_ref[rows, cols].astype(F32).T.astype(BF16)
            t = tk_ref[rows, :].astype(F32)
            kke_scr[rows, :] = jnp.where(lane < IDX_DIM, t, 0.0).astype(BF16)
            kko_scr[rows, :] = jnp.where(lane >= IDX_DIM, pltpu.roll(t, IDX_DIM, 1), 0.0).astype(BF16)
            return carry

        lax.fori_loop(0, seq // tq, body, 0)

    for j in range(IDX_HEADS // 2):
        qst_scr[j * tq:(j + 1) * tq, :] = qi_ref[:, j * LANES:(j + 1) * LANES]
    w_scale = (IDX_DIM ** -0.5) * (IDX_HEADS ** -0.5)
    wib_scr[...] = tq_ref[...].astype(F32).T[IDX_DIM:IDX_DIM + IDX_HEADS, :] * w_scale

    def idx_body(c, carry):
        rows = pl.ds(pl.multiple_of(c * tq, tq), tq)
        qst = qst_scr[...]
        re = _dot_nt(kke_scr[rows, :], qst)
        ro = _dot_nt(kko_scr[rows, :], qst)
        acc = jnp.zeros((tq, tq), F32)
        for j in range(IDX_HEADS // 2):
            acc = acc + jnp.maximum(re[:, j * tq:(j + 1) * tq], 0.0) * wib_scr[2 * j:2 * j + 1, :]
            acc = acc + jnp.maximum(ro[:, j * tq:(j + 1) * tq], 0.0) * wib_scr[2 * j + 1:2 * j + 2, :]
        bits = pltpu.bitcast(acc, I32)
        key = bits ^ ((bits >> 31) & 0x7FFFFFFF)
        causal = (c < i) | (krow <= qcol)
        key_scr[c] = jnp.where(causal, key, INT_MIN)
        return carry

    lax.fori_loop(0, nc, idx_body, 0)

    def bit_body(bi, t_u):
        cand_u = t_u | lax.shift_left(jnp.int32(1), 31 - bi)
        cand_s = cand_u ^ INT_MIN

        def cnt_body(c, cnt):
            ge = jnp.where(key_scr[c] >= cand_s, 1, 0)
            return cnt + jnp.sum(ge.reshape(tq // 8, 8, tq), axis=0)

        cnt = lax.fori_loop(0, nc, cnt_body, jnp.zeros((8, tq), I32))
        total = jnp.sum(cnt, axis=0, keepdims=True)
        return jnp.where(total >= topk, cand_u, t_u)

    nbits = jnp.where(nc * tq > topk, 32, 0)
    t_u = lax.fori_loop(0, nbits, bit_body, jnp.zeros((1, tq), I32))
    thr = t_u ^ INT_MIN

    qpos = pr_ref[0, i]

    def bkt_body(c, carry):
        rows = pl.ds(pl.multiple_of(c * tq, tq), tq)
        bkt = _t5_bucket(qpos - pc_ref[rows, :])
        causal = (c < i) | (krow <= qcol)
        sel = (key_scr[c] >= thr) & causal
        key_scr[c] = jnp.where(sel, bkt, N_BUCKETS)
        return carry

    lax.fori_loop(0, nc, bkt_body, 0)

    gq = gq_ref[...]
    scale = (hd ** -0.5) * LOG2E
    for g in range(A_KV_HEADS):
        for hh in range(grp):
            h = g * grp + hh
            qh = q_ref[:, h * hd:(h + 1) * hd].astype(F32)
            ms = jnp.mean(qh * qh, axis=-1, keepdims=True)
            qn_scr[hh * tq:(hh + 1) * tq, :] = (((qh * lax.rsqrt(ms + EPS)) * gq) * scale).astype(BF16)
        m_scr[...] = jnp.full(m_scr.shape, NEG, F32)
        l_scr[...] = jnp.zeros(l_scr.shape, F32)
        acc_scr[...] = jnp.zeros(acc_scr.shape, F32)

        def att_body(c, carry, g=g):
            rows = pl.ds(pl.multiple_of(c * tq, tq), tq)
            s = _dot_nt(kn_scr[rows, g * hd:(g + 1) * hd], qn_scr[...])
            bkt = key_scr[c]
            parts = []
            for hh in range(grp):
                h = g * grp + hh
                parts.append(s[:, hh * tq:(hh + 1) * tq] + _bias_lookup(tab_ref[h:h + 1, :], bkt, tq))
            s = jnp.concatenate(parts, axis=1)
            m_old = m_scr[...]
            m_new = jnp.maximum(m_old, jnp.max(s, axis=0, keepdims=True))
            p = jnp.exp2(s - m_new)
            alpha = jnp.exp2(m_old - m_new)
            l_scr[...] = alpha * l_scr[...] + jnp.sum(p, axis=0, keepdims=True)
            acc_scr[...] = alpha * acc_scr[...] + _dot(vt_scr[c, g * hd:(g + 1) * hd, :], p.astype(BF16))
            m_scr[...] = m_new
            return carry

        lax.fori_loop(0, nc, att_body, 0)
        o = acc_scr[...] * (1.0 / l_scr[...])
        for hh in range(grp):
            h = g * grp + hh
            o_ref[:, h * hd:(h + 1) * hd] = o[:, hh * tq:(hh + 1) * tq].T.astype(o_ref.dtype)


def _dsa_attention(qkv, tail, pos, tab, gq, gk, batch, seq, tq=256):
    tq = min(tq, seq)
    nq = seq // tq
    a_q = A_HEADS * A_HEAD_DIM
    a_kv = A_KV_HEADS * A_HEAD_DIM
    a_qi = IDX_HEADS * IDX_DIM
    topk = min(TOPK_MAX, seq // 4)
    grp = A_HEADS // A_KV_HEADS
    body = functools.partial(_dsa_body, topk=topk, tq=tq, seq=seq)
    return pl.pallas_call(
        body,
        grid=(batch, nq),
        in_specs=[
            pl.BlockSpec((tq, a_q), lambda b, i: (b * nq + i, 0)),
            pl.BlockSpec((tq, a_qi), lambda b, i: (b * nq + i, (a_q + 2 * a_kv) // a_qi)),
            pl.BlockSpec((seq, a_kv), lambda b, i: (b, a_q // a_kv)),
            pl.BlockSpec((seq, a_kv), lambda b, i: (b, a_q // a_kv + 1)),
            pl.BlockSpec((tq, LANES), lambda b, i: (b * nq + i, 0)),
            pl.BlockSpec((seq, LANES), lambda b, i: (b, 0)),
            pl.BlockSpec((1, nq, 1, tq), lambda b, i: (b, 0, 0, 0)),
            pl.BlockSpec((seq, 1), lambda b, i: (b, 0)),
            pl.BlockSpec((A_HEADS, LANES), lambda b, i: (0, 0)),
            pl.BlockSpec((1, A_HEAD_DIM), lambda b, i: (0, 0)),
            pl.BlockSpec((1, A_HEAD_DIM), lambda b, i: (0, 0)),
        ],
        out_specs=pl.BlockSpec((tq, a_q), lambda b, i: (b * nq + i, 0)),
        out_shape=jax.ShapeDtypeStruct((batch * seq, a_q), BF16),
        scratch_shapes=[
            pltpu.VMEM((seq, a_kv), BF16),
            pltpu.VMEM((nq, a_kv, tq), BF16),
            pltpu.VMEM((seq, LANES), BF16),
            pltpu.VMEM((seq, LANES), BF16),
            pltpu.VMEM((IDX_HEADS // 2 * tq, LANES), BF16),
            pltpu.VMEM((IDX_HEADS, tq), F32),
            pltpu.VMEM((nq, tq, tq), I32),
            pltpu.VMEM((grp * tq, A_HEAD_DIM), BF16),
            pltpu.VMEM((1, grp * tq), F32),
            pltpu.VMEM((1, grp * tq), F32),
            pltpu.VMEM((A_HEAD_DIM, grp * tq), F32),
        ],
        compiler_params=_params("arbitrary", "arbitrary"),
        name="dsa_attention",
    )(qkv, qkv, qkv, qkv, tail, tail, pos.reshape(batch, nq, 1, tq), pos.reshape(batch * seq, 1),
      tab, gq, gk)


def _band_body(*refs, cls_len, wk, use_prev, tq):
    if use_prev:
        (q_ref, kc_ref, vc_ref, kp_ref, vp_ref, pc_ref, prc_ref, prp_ref,
         tab_ref, gq_ref, gk_ref, o_ref, lse_ref, bk_scr) = refs
    else:
        (q_ref, kc_ref, vc_ref, pc_ref, prc_ref,
         tab_ref, gq_ref, gk_ref, o_ref, lse_ref, bk_scr) = refs
    t = pl.program_id(1)
    hd = B_HEAD_DIM
    pw = 2 * hd
    shift = cls_len.bit_length() - 1
    lane = lax.broadcasted_iota(I32, (1, pw), 1)
    lo = lane < hd
    qpos = pc_ref[...]
    fq = t * tq + lax.broadcasted_iota(I32, (tq, 1), 0)

    def bucket_mask(fk, kpos):
        same = ((fq + cls_len) >> shift) == ((fk + cls_len) >> shift)
        d = (fq & (cls_len - 1)) - (fk & (cls_len - 1))
        ok = same & (d >= 0) & (d <= wk)
        return jnp.where(ok, _t5_bucket(qpos - kpos), N_BUCKETS)

    off = LANES if use_prev else 0
    if use_prev:
        fk = t * tq - LANES + lax.broadcasted_iota(I32, (1, LANES), 1)
        bk_scr[:, 0:LANES] = bucket_mask(fk, prp_ref[0])
    fk = t * tq + lax.broadcasted_iota(I32, (1, tq), 1)
    bk_scr[:, off:off + tq] = bucket_mask(fk, prc_ref[0])

    gq = gq_ref[...]
    gk = gk_ref[...]
    scale = hd ** -0.5

    def pair_norm(x, g):
        sq = x * x
        ms_lo = jnp.sum(jnp.where(lo, sq, 0.0), axis=-1, keepdims=True) * (1.0 / hd)
        ms_hi = jnp.sum(jnp.where(lo, 0.0, sq), axis=-1, keepdims=True) * (1.0 / hd)
        inv = jnp.where(lo, lax.rsqrt(ms_lo + EPS), lax.rsqrt(ms_hi + EPS))
        return (x * inv) * g

    for p in range(B_HEADS // 2):
        cols = slice(p * pw, (p + 1) * pw)
        qn = pair_norm(q_ref[:, cols].astype(F32), gq) * scale
        if use_prev:
            kraw = jnp.concatenate([kp_ref[:, cols], kc_ref[:, cols]], axis=0)
            v = jnp.concatenate([vp_ref[:, cols], vc_ref[:, cols]], axis=0)
        else:
            kraw = kc_ref[:, cols]
            v = vc_ref[:, cols]
        kn = pair_norm(kraw.astype(F32), gk).astype(BF16)
        bkt = bk_scr[...]
        o_pair = jnp.zeros((tq, pw), F32)
        lse_pair = jnp.zeros((tq, pw), F32)
        for half in range(2):
            h = 2 * p + half
            keep = lo if half == 0 else jnp.logical_not(lo)
            qh = jnp.where(keep, qn, 0.0).astype(BF16)
            vh = jnp.where(keep, v, jnp.zeros_like(v))
            s = _dot_nt(qh, kn) + _bias_lookup(tab_ref[h:h + 1, :], bkt, tq)
            m = jnp.max(s, axis=1, keepdims=True)
            e = jnp.exp(s - m)
            l = jnp.sum(e, axis=1, keepdims=True)
            o_pair = o_pair + _dot(e.astype(BF16), vh) * (1.0 / l)
            lse_pair = jnp.where(keep, m + jnp.log(l), lse_pair)
        o_ref[:, cols] = o_pair.astype(o_ref.dtype)
        lse_ref[:, cols] = lse_pair


def _band_attention(q_arr, q_col, k_arr, k_col, v_arr, v_col, pos, tab, gq2, gk2,
                    batch, seq, dilation, window, tq=512):
    cls_len = seq // dilation
    wk = window // dilation
    assert cls_len & (cls_len - 1) == 0 and wk <= LANES
    tq = min(tq, seq)
    use_prev = cls_len > tq
    assert use_prev or tq % cls_len == 0
    nt = seq // tq
    w = B_HEADS * B_HEAD_DIM
    sub = tq // LANES
    pos_col = pos.reshape(batch * seq, 1)
    pos_row = pos.reshape(batch * nt, 1, tq)
    in_specs = [pl.BlockSpec((tq, w), lambda b, t: (b * nt + t, q_col)),
                pl.BlockSpec((tq, w), lambda b, t: (b * nt + t, k_col)),
                pl.BlockSpec((tq, w), lambda b, t: (b * nt + t, v_col))]
    args = [q_arr, k_arr, v_arr]
    if use_prev:
        prev = lambda b, t: jnp.maximum((b * nt + t) * sub - 1, 0)
        in_specs += [pl.BlockSpec((LANES, w), lambda b, t: (prev(b, t), k_col)),
                     pl.BlockSpec((LANES, w), lambda b, t: (prev(b, t), v_col))]
        args += [k_arr, v_arr]
    in_specs += [pl.BlockSpec((tq, 1), lambda b, t: (b * nt + t, 0)),
                 pl.BlockSpec((1, 1, tq), lambda b, t: (b * nt + t, 0, 0))]
    args += [pos_col, pos_row]
    if use_prev:
        in_specs.append(pl.BlockSpec((1, 1, LANES), lambda b, t: (prev(b, t), 0, 0)))
        args.append(pos.reshape(batch * seq // LANES, 1, LANES))
    in_specs += [pl.BlockSpec((B_HEADS, LANES), lambda b, t: (0, 0)),
                 pl.BlockSpec((1, LANES), lambda b, t: (0, 0)),
                 pl.BlockSpec((1, LANES), lambda b, t: (0, 0))]
    args += [tab, gq2, gk2]
    nk = tq + (LANES if use_prev else 0)
    body = functools.partial(_band_body, cls_len=cls_len, wk=wk, use_prev=use_prev, tq=tq)
    return pl.pallas_call(
        body,
        grid=(batch, nt),
        in_specs=in_specs,
        out_specs=[pl.BlockSpec((tq, w), lambda b, t: (b * nt + t, 0)),
                   pl.BlockSpec((tq, w), lambda b, t: (b * nt + t, 0))],
        out_shape=[jax.ShapeDtypeStruct((batch * seq, w), BF16),
                   jax.ShapeDtypeStruct((batch * seq, w), F32)],
        scratch_shapes=[pltpu.VMEM((tq, nk), I32)],
        compiler_params=_params("arbitrary", "arbitrary"),
        name="dilated_attention",
    )(*args)


def _merge_body(o0, o1, o2, l0, l1, l2, w_ref, x_ref, gt_ref, out_ref, a_scr):
    @pl.when(pl.program_id(1) == 0)
    def _():
        m = jnp.maximum(jnp.maximum(l0[...], l1[...]), l2[...])
        w0 = jnp.exp(l0[...] - m)
        w1 = jnp.exp(l1[...] - m)
        w2 = jnp.exp(l2[...] - m)
        num = w0 * o0[...].astype(F32) + w1 * o1[...].astype(F32) + w2 * o2[...].astype(F32)
        a_scr[...] = (num / (w0 + w1 + w2)).astype(BF16)

    out_ref[...] = x_ref[...] + gt_ref[0] * _dot(a_scr[...], w_ref[...].astype(BF16))


def _merge_out(os_, ls_, w3, layer, x, gt, seq, tm=512, tn=512):
    m, k = os_[0].shape
    d = x.shape[1]
    nb = seq // tm
    row = pl.BlockSpec((tm, k), lambda i, j: (i, 0))
    return pl.pallas_call(
        _merge_body,
        grid=(m // tm, d // tn),
        in_specs=[row] * 6 + [
            pl.BlockSpec((None, k, tn), lambda i, j: (layer, 0, j)),
            pl.BlockSpec((tm, tn), lambda i, j: (i, j)),
            pl.BlockSpec((1, 1, tn), lambda i, j: (i // nb, 0, j))],
        out_specs=pl.BlockSpec((tm, tn), lambda i, j: (i, j)),
        out_shape=jax.ShapeDtypeStruct((m, d), F32),
        scratch_shapes=[pltpu.VMEM((tm, k), BF16)],
        compiler_params=_params("arbitrary", "arbitrary"),
        name="merge_out_proj",
    )(*os_, *ls_, w3, x, gt.reshape(-1, 1, d))


def _router_body(x_ref, g_ref, sc_ref, sh_ref, rh_ref, rl_ref, rb_ref, h_ref, rt_ref):
    h = _norm_mod(x_ref[...], g_ref[...], sc_ref[0], sh_ref[0])
    h_ref[...] = h
    hh = h.astype(BF16)
    hl = (h - hh.astype(F32)).astype(BF16)
    logits = _dot(hh, rh_ref[...]) + _dot(hl, rh_ref[...]) + _dot(hh, rl_ref[...]) + rb_ref[...]
    lane = lax.broadcasted_iota(I32, logits.shape, 1)
    logits = jnp.where(lane < N_EXPERTS, logits, NEG)
    v1 = jnp.max(logits, axis=1, keepdims=True)
    i1 = jnp.min(jnp.where(logits == v1, lane, LANES), axis=1, keepdims=True)
    rest = jnp.where(lane == i1, NEG, logits)
    v2 = jnp.max(rest, axis=1, keepdims=True)
    i2 = jnp.min(jnp.where(rest == v2, lane, LANES), axis=1, keepdims=True)
    e = jnp.exp(v2 - v1)
    g1 = 1.0 / (1.0 + e)
    g2 = e * g1
    rt_ref[...] = jnp.where(lane == 0, i1.astype(F32),
                            jnp.where(lane == 1, i2.astype(F32),
                                      jnp.where(lane == 2, g1, jnp.where(lane == 3, g2, 0.0))))


def _router(x, g, sc, sh, rw, rb, seq, tm=512):
    m, d = x.shape
    nb = seq // tm
    ne = rw.shape[1]
    rw_p = jnp.zeros((d, LANES), F32).at[:, :ne].set(rw)
    rh = rw_p.astype(BF16)
    rl = (rw_p - rh.astype(F32)).astype(BF16)
    rb_p = jnp.zeros((1, LANES), F32).at[0, :ne].set(rb)
    return pl.pallas_call(
        _router_body,
        grid=(m // tm,),
        in_specs=[pl.BlockSpec((tm, d), lambda i: (i, 0)),
                  pl.BlockSpec((1, d), lambda i: (0, 0)),
                  pl.BlockSpec((1, 1, d), lambda i: (i // nb, 0, 0)),
                  pl.BlockSpec((1, 1, d), lambda i: (i // nb, 0, 0)),
                  pl.BlockSpec((d, LANES), lambda i: (0, 0)),
                  pl.BlockSpec((d, LANES), lambda i: (0, 0)),
                  pl.BlockSpec((1, LANES), lambda i: (0, 0))],
        out_specs=[pl.BlockSpec((tm, d), lambda i: (i, 0)),
                   pl.BlockSpec((tm, LANES), lambda i: (i, 0))],
        out_shape=[jax.ShapeDtypeStruct((m, d), F32),
                   jax.ShapeDtypeStruct((m, LANES), F32)],
        compiler_params=_params("arbitrary"),
        name="router_top2",
    )(x, g.reshape(1, d), sc.reshape(-1, 1, d), sh.reshape(-1, 1, d), rh, rl, rb_p)


def _row_copy(src_hbm, idx, buf, r, sem):
    return pltpu.make_async_copy(src_hbm.at[pl.ds(idx, 1), :], buf.at[pl.ds(r, 1), :], sem)


def _gather_body(src_ref, nact_ref, h_hbm, o_ref, buf, sem, *, rows):
    i = pl.program_id(0)
    n_active = nact_ref[0]

    def issue(step):
        slot = step % 2

        def body(r, carry):
            _row_copy(h_hbm, src_ref[step * rows + r], buf.at[slot], r, sem.at[slot]).start()
            return carry

        lax.fori_loop(0, rows, body, 0, unroll=8)

    @pl.when(i == 0)
    def _():
        issue(i)

    @pl.when((i + 1) * rows < n_active)
    def _():
        issue(i + 1)

    slot = i % 2

    @pl.when(i * rows < n_active)
    def _():
        pltpu.make_async_copy(h_hbm.at[pl.ds(0, rows), :], buf.at[slot], sem.at[slot]).wait()
        o_ref[...] = buf[slot].astype(o_ref.dtype)

    @pl.when(i * rows >= n_active)
    def _():
        o_ref[...] = jnp.zeros(o_ref.shape, o_ref.dtype)


def _gather_rows(h, src, n_active, rows):
    r_total = src.shape[0]
    d = h.shape[1]
    return pl.pallas_call(
        functools.partial(_gather_body, rows=rows),
        grid_spec=pltpu.PrefetchScalarGridSpec(
            num_scalar_prefetch=2,
            grid=(r_total // rows,),
            in_specs=[pl.BlockSpec(memory_space=pl.ANY)],
            out_specs=pl.BlockSpec((rows, d), lambda i, s, n: (i, 0)),
            scratch_shapes=[pltpu.VMEM((2, rows, d), F32), pltpu.SemaphoreType.DMA((2,))],
        ),
        out_shape=jax.ShapeDtypeStruct((r_total, d), BF16),
        compiler_params=_params("arbitrary"),
        name="moe_dispatch_gather",
    )(src, n_active, h)


def _moe_body(te_ref, tv_ref, hs_ref, w1_ref, w3_ref, w2_ref, o_hbm, acc_scr, sem, *, ncol):
    t = pl.program_id(0)
    f = pl.program_id(1)
    last = pl.num_programs(1) - 1
    tm, d = acc_scr.shape
    cw = d // ncol
    out = pltpu.make_async_copy(acc_scr, o_hbm.at[pl.ds(pl.multiple_of(t * tm, tm), tm), :], sem)

    @pl.when((tv_ref[t] == 0) & (f == 0))
    def _():
        acc_scr[...] = jnp.zeros(acc_scr.shape, acc_scr.dtype)
        out.start()

    @pl.when((tv_ref[t] == 0) & (f == last))
    def _():
        out.wait()

    @pl.when(tv_ref[t] > 0)
    def _():
        h = hs_ref[...]
        a = _dot(h, w1_ref[...].astype(BF16))
        b = _dot(h, w3_ref[...].astype(BF16))
        u = (a * _sigmoid(a) * b).astype(BF16)
        w2 = w2_ref[...].astype(BF16)

        @pl.when(f == 0)
        def _():
            for n in range(ncol):
                acc_scr[:, n * cw:(n + 1) * cw] = _dot(u, w2[:, n * cw:(n + 1) * cw])

        @pl.when(f > 0)
        def _():
            for n in range(ncol):
                acc_scr[:, n * cw:(n + 1) * cw] += _dot(u, w2[:, n * cw:(n + 1) * cw])

        @pl.when(f == last)
        def _():
            out.start()
            out.wait()


def _moe_experts(hs, tile_expert, tile_valid, w1, w3, w2, tm, tf=256, ncol=4):
    r_total, d = hs.shape
    n_tiles = r_total // tm
    nf = w1.shape[2] // tf

    def w13_map(t, f, te, tv):
        return (te[t], 0, jnp.where(tv[t] > 0, f, nf - 1))

    def w2_map(t, f, te, tv):
        return (te[t], jnp.where(tv[t] > 0, f, nf - 1), 0)

    return pl.pallas_call(
        functools.partial(_moe_body, ncol=ncol),
        grid_spec=pltpu.PrefetchScalarGridSpec(
            num_scalar_prefetch=2,
            grid=(n_tiles, nf),
            in_specs=[pl.BlockSpec((tm, d), lambda t, f, te, tv: (jnp.where(tv[t] > 0, t, 0), 0),
                                   pipeline_mode=pl.Buffered(1)),
                      pl.BlockSpec((None, d, tf), w13_map),
                      pl.BlockSpec((None, d, tf), w13_map),
                      pl.BlockSpec((None, tf, d), w2_map)],
            out_specs=pl.BlockSpec(memory_space=pl.ANY),
            scratch_shapes=[pltpu.VMEM((tm, d), F32), pltpu.SemaphoreType.DMA(())],
        ),
        out_shape=jax.ShapeDtypeStruct((r_total, d), F32),
        compiler_params=_params("arbitrary", "arbitrary"),
        name="moe_experts",
    )(tile_expert, tile_valid, hs, w1, w3, w2)


def _combine_body(p1_ref, p2_ref, y_hbm, x_ref, gt_ref, rt_ref, o_ref, buf_a, buf_b, sem, *, rows):
    base = pl.program_id(0) * rows

    def issue(r, carry):
        _row_copy(y_hbm, p1_ref[base + r], buf_a, r, sem).start()
        _row_copy(y_hbm, p2_ref[base + r], buf_b, r, sem).start()
        return carry

    lax.fori_loop(0, rows, issue, 0, unroll=4)
    pltpu.make_async_copy(y_hbm.at[pl.ds(0, rows), :], buf_a, sem).wait()
    pltpu.make_async_copy(y_hbm.at[pl.ds(0, rows), :], buf_b, sem).wait()
    rt = rt_ref[...]
    g1 = rt[:, 2:3]
    g2 = rt[:, 3:4]
    o_ref[...] = x_ref[...] + gt_ref[0] * (g1 * buf_a[...] + g2 * buf_b[...])


def _combine(ys, p1, p2, x, gt, route, seq, rows=256):
    m, d = x.shape
    nb = seq // rows
    return pl.pallas_call(
        functools.partial(_combine_body, rows=rows),
        grid_spec=pltpu.PrefetchScalarGridSpec(
            num_scalar_prefetch=2,
            grid=(m // rows,),
            in_specs=[pl.BlockSpec(memory_space=pl.ANY),
                      pl.BlockSpec((rows, d), lambda i, a, b: (i, 0)),
                      pl.BlockSpec((1, 1, d), lambda i, a, b: (i // nb, 0, 0)),
                      pl.BlockSpec((rows, LANES), lambda i, a, b: (i, 0))],
            out_specs=pl.BlockSpec((rows, d), lambda i, a, b: (i, 0)),
            scratch_shapes=[pltpu.VMEM((rows, d), F32), pltpu.VMEM((rows, d), F32),
                            pltpu.SemaphoreType.DMA(())],
        ),
        out_shape=jax.ShapeDtypeStruct((m, d), F32),
        compiler_params=_params("arbitrary"),
        name="moe_combine",
    )(p1, p2, ys, x, gt.reshape(-1, 1, d), route)


def _routing_tables(route, tm, n_tiles):
    t = route.shape[0]
    experts = route[:, :2].astype(I32).reshape(-1)
    onehot = (experts[:, None] == jnp.arange(N_EXPERTS, dtype=I32)[None, :]).astype(I32)
    csum = jnp.cumsum(onehot, axis=0)
    rank = jnp.sum(csum * onehot, axis=1) - 1
    counts = csum[-1]
    tiles = (counts + tm - 1) // tm
    tend = jnp.cumsum(tiles)
    tstart = tend - tiles
    slot = tstart[experts] * tm + rank
    token = jnp.arange(2 * t, dtype=I32) // 2
    src = jnp.zeros((n_tiles * tm,), I32).at[slot].set(token)
    tile_id = jnp.arange(n_tiles, dtype=I32)
    te = jnp.sum((tile_id[:, None] >= tend[None, :]).astype(I32), axis=1)
    active = tile_id < tend[-1]
    last_e = jnp.sum((tend[-1] - 1 >= tend).astype(I32))
    te = jnp.where(active, te, last_e)
    tv = jnp.where(active, jnp.clip(counts[te] - (tile_id - tstart[te]) * tm, 0, tm), 0)
    slots = slot.reshape(t, 2)
    n_active = (tend[-1] * tm).astype(I32).reshape(1)
    return src, n_active, te.astype(I32), tv.astype(I32), slots[:, 0], slots[:, 1]


def _bias_table(rel_bias, scale=1.0):
    h = rel_bias.shape[1]
    return jnp.full((h, LANES), NEG, F32).at[:, :N_BUCKETS].set(rel_bias.T * scale)


def _to_classes(a, batch, seq, r):
    if r == 1:
        return a
    w = a.shape[-1]
    return a.reshape(batch, seq // r, r, w).transpose(0, 2, 1, 3).reshape(batch * seq, w)


def _from_classes(a, batch, seq, r):
    if r == 1:
        return a
    w = a.shape[-1]
    return a.reshape(batch, r, seq // r, w).transpose(0, 2, 1, 3).reshape(batch * seq, w)


def kernel(x, c, positions, rel_bias, w_mod, b_mod, g_attn, g_ffn, a_w_in, a_w_out, a_g_qn, a_g_kn,
           kv_w_mod, kv_b_mod, kv_g, kv_w, b_g_kn, b_w_q, b_w_out, b_g_qn, ffn_w1, ffn_w3, ffn_w2,
           moe_router, moe_router_b, moe_w1, moe_w3, moe_w2):
    batch, seq, d = x.shape
    m = batch * seq
    x2 = x.reshape(m, d)
    positions = positions.astype(I32)

    c8 = jnp.zeros((8, d), F32).at[:batch].set(c)
    mod0 = _mod_call(c8, w_mod, 0, b_mod)[:batch]
    mod1 = _mod_call(c8, w_mod, 1, b_mod)[:batch]
    kvm = _mod_call(c8, kv_w_mod[None], 0, kv_b_mod[None])[:batch]
    sh1_0, sc1_0, gt1_0, sh2_0, sc2_0, gt2_0 = jnp.split(mod0, 6, axis=-1)
    sh1_1, sc1_1, gt1_1, sh2_1, sc2_1, gt2_1 = jnp.split(mod1, 6, axis=-1)
    kv_sh, kv_sc = jnp.split(kvm, 2, axis=-1)

    tab = _bias_table(rel_bias)

    a_main = A_HEADS * A_HEAD_DIM + 2 * A_KV_HEADS * A_HEAD_DIM + IDX_HEADS * IDX_DIM
    n_tail = IDX_DIM + IDX_HEADS
    w_tail = jnp.zeros((d, LANES), F32).at[:, :n_tail].set(a_w_in[0, :, a_main:a_main + n_tail])
    qkv, tail = _nm_matmul(x2, g_attn[0], sc1_0, sh1_0, a_w_in, 0, a_main, seq, w_tail=w_tail)
    attn = _dsa_attention(qkv, tail, positions, _bias_table(rel_bias, LOG2E),
                          a_g_qn[0].reshape(1, -1), a_g_kn[0].reshape(1, -1), batch, seq)
    x2 = _matmul_residual(attn, a_w_out, 0, x2, gt1_0, seq, tn=512)
    u = _nm_swiglu(x2, g_ffn[0], sc2_0, sh2_0, ffn_w1, ffn_w3, 0, seq)
    x2 = _matmul_residual(u, ffn_w2, 0, x2, gt2_0, seq)

    b_q = len(B_DILATIONS) * B_HEADS * B_HEAD_DIM
    kvall = _nm_matmul(x2, kv_g, kv_sc, kv_sh, kv_w[None], 0, 2 * b_q, seq)
    qall = _nm_matmul(x2, g_attn[1], sc1_1, sh1_1, b_w_q, 0, b_q, seq)
    gq2 = jnp.tile(b_g_qn[0], 2).reshape(1, LANES)
    gk2 = jnp.tile(b_g_kn, 2).reshape(1, LANES)
    w = B_HEADS * B_HEAD_DIM
    ng = len(B_DILATIONS)
    outs, lses = [], []
    for g, (window, r) in enumerate(B_DILATIONS):
        if r == 1:
            q_arr, q_col, k_arr, k_col, v_arr, v_col = qall, g, kvall, g, kvall, ng + g
            pos_g = positions
        else:
            q_arr = _to_classes(qall[:, g * w:(g + 1) * w], batch, seq, r)
            k_arr = _to_classes(kvall[:, g * w:(g + 1) * w], batch, seq, r)
            v_arr = _to_classes(kvall[:, (ng + g) * w:(ng + g + 1) * w], batch, seq, r)
            q_col = k_col = v_col = 0
            pos_g = positions.reshape(batch, seq // r, r).transpose(0, 2, 1).reshape(batch, seq)
        o_g, lse_g = _band_attention(q_arr, q_col, k_arr, k_col, v_arr, v_col, pos_g, tab, gq2, gk2,
                                     batch, seq, r, window)
        outs.append(_from_classes(o_g, batch, seq, r))
        lses.append(_from_classes(lse_g, batch, seq, r))
    x2 = _merge_out(outs, lses, b_w_out, 0, x2, gt1_1, seq)

    h, route = _router(x2, g_ffn[1], sc2_1, sh2_1, moe_router[0], moe_router_b[0], seq)
    share = (2 * m) // N_EXPERTS
    tm = -(-(share * 17 // 16) // 128) * 128
    n_tiles = (2 * m) // tm + N_EXPERTS
    src, n_active, te, tv, p1, p2 = _routing_tables(route, tm, n_tiles)
    hs = _gather_rows(h, src, n_active, tm // 8)
    ys = _moe_experts(hs, te, tv, moe_w1.reshape(moe_w1.shape[1:]), moe_w3.reshape(moe_w3.shape[1:]),
                      moe_w2.reshape(moe_w2.shape[1:]), tm)
    out = _combine(ys, p1, p2, x2, gt2_1, route, seq)
    return out.reshape(batch, seq, d)
```

```python
import functools
import math

import jax
import jax.numpy as jnp
from jax import lax
from jax.experimental import pallas as pl
from jax.experimental.pallas import tpu as pltpu

F32 = jnp.float32
BF16 = jnp.bfloat16
I32 = jnp.int32

EPS = 1e-6
NEG = -1e30
INT_MIN = -(2 ** 31)
LOG2E = 1.0 / math.log(2.0)

A_HEADS, A_KV_HEADS, A_HEAD_DIM = 16, 4, 128
IDX_HEADS, IDX_DIM = 16, 64
TOPK_MAX = 256
B_DILATIONS = ((128, 1), (512, 4), (2048, 16))
B_HEADS, B_HEAD_DIM = 16, 64
N_BUCKETS, MAX_DISTANCE = 32, 2048
N_EXPERTS = 8
LANES = 128

VMEM_LIMIT_BYTES = 56 * 1024 * 1024

_NT = (((1,), (1,)), ((), ()))


def _params(*sem):
    return pltpu.CompilerParams(dimension_semantics=sem, vmem_limit_bytes=VMEM_LIMIT_BYTES)


def _dot(a, b):
    return jnp.dot(a, b, preferred_element_type=F32)


def _dot_nt(a, b):
    return lax.dot_general(a, b, _NT, preferred_element_type=F32)


def _sigmoid(x):
    return 1.0 / (1.0 + jnp.exp(-x))


def _t5_bucket(rel):
    n = jnp.maximum(rel, 0)
    max_exact = N_BUCKETS // 2
    nf = jnp.maximum(n, 1).astype(F32)
    large = max_exact + (jnp.log(nf / max_exact) / math.log(MAX_DISTANCE / max_exact)
                         * (N_BUCKETS - max_exact)).astype(I32)
    large = jnp.minimum(large, N_BUCKETS - 1)
    return jnp.where(n < max_exact, n, large)


def _norm_mod(x, g, sc, sh):
    y = x * lax.rsqrt(jnp.mean(x * x, axis=-1, keepdims=True) + EPS)
    return (y * g) * (1.0 + sc) + sh


def _bias_lookup(tab_row, bkt, rows):
    tab = jnp.broadcast_to(tab_row, (rows, LANES))
    parts = [jnp.take_along_axis(tab, bkt[:, c * LANES:(c + 1) * LANES], axis=1)
             for c in range(bkt.shape[1] // LANES)]
    return parts[0] if len(parts) == 1 else jnp.concatenate(parts, axis=1)


def _mod_body(c_ref, w_ref, b_ref, o_ref):
    c = c_ref[...]
    cs = c * _sigmoid(c)
    o_ref[...] = _dot(cs.astype(BF16), w_ref[...].astype(BF16)) + b_ref[...]


def _mod_call(c8, w3, layer, b2):
    _, d, n = w3.shape
    tn = 1024
    return pl.pallas_call(
        _mod_body,
        grid=(n // tn,),
        in_specs=[pl.BlockSpec((8, d), lambda j: (0, 0)),
                  pl.BlockSpec((None, d, tn), lambda j: (layer, 0, j)),
                  pl.BlockSpec((None, 1, tn), lambda j: (layer, 0, j))],
        out_specs=pl.BlockSpec((8, tn), lambda j: (0, j)),
        out_shape=jax.ShapeDtypeStruct((8, n), F32),
        compiler_params=_params("arbitrary"),
        name="adaln_mod",
    )(c8, w3, b2.reshape(b2.shape[0], 1, n))


def _nm_body(x_ref, g_ref, sc_ref, sh_ref, w_ref, *rest, has_tail):
    if has_tail:
        wt_ref, o_ref, ot_ref, h_scr = rest
    else:
        o_ref, h_scr = rest

    @pl.when(pl.program_id(1) == 0)
    def _():
        h = _norm_mod(x_ref[...], g_ref[...], sc_ref[0], sh_ref[0]).astype(BF16)
        h_scr[...] = h
        if has_tail:
            ot_ref[...] = _dot(h, wt_ref[...].astype(BF16)).astype(ot_ref.dtype)

    o_ref[...] = _dot(h_scr[...], w_ref[...].astype(BF16)).astype(o_ref.dtype)


def _nm_matmul(x, g, sc, sh, w3, layer, n_cols, seq, w_tail=None, tm=2048, tn=512):
    m, d = x.shape
    tm = min(tm, seq)
    nb = seq // tm
    has_tail = w_tail is not None
    in_specs = [pl.BlockSpec((tm, d), lambda i, j: (i, 0), pipeline_mode=pl.Buffered(1)),
                pl.BlockSpec((1, d), lambda i, j: (0, 0)),
                pl.BlockSpec((1, 1, d), lambda i, j: (i // nb, 0, 0)),
                pl.BlockSpec((1, 1, d), lambda i, j: (i // nb, 0, 0)),
                pl.BlockSpec((None, d, tn), lambda i, j: (layer, 0, j))]
    args = [x, g.reshape(1, d), sc.reshape(-1, 1, d), sh.reshape(-1, 1, d), w3]
    out_specs = [pl.BlockSpec((tm, tn), lambda i, j: (i, j))]
    out_shape = [jax.ShapeDtypeStruct((m, n_cols), BF16)]
    if has_tail:
        in_specs.append(pl.BlockSpec((d, LANES), lambda i, j: (0, 0)))
        args.append(w_tail)
        out_specs.append(pl.BlockSpec((tm, LANES), lambda i, j: (i, 0)))
        out_shape.append(jax.ShapeDtypeStruct((m, LANES), BF16))
    res = pl.pallas_call(
        functools.partial(_nm_body, has_tail=has_tail),
        grid=(m // tm, n_cols // tn),
        in_specs=in_specs,
        out_specs=out_specs,
        out_shape=out_shape,
        scratch_shapes=[pltpu.VMEM((tm, d), BF16)],
        compiler_params=_params("arbitrary", "arbitrary"),
        name="norm_mod_matmul",
    )(*args)
    return res if has_tail else res[0]


def _mmres_body(a_ref, w_ref, x_ref, gt_ref, o_ref):
    o_ref[...] = x_ref[...] + gt_ref[0] * _dot(a_ref[...], w_ref[...].astype(BF16))


def _matmul_residual(a, w3, layer, x, gt, seq, tm=2048, tn=256):
    m, k = a.shape
    d = x.shape[1]
    tm = min(tm, seq)
    nb = seq // tm
    return pl.pallas_call(
        _mmres_body,
        grid=(m // tm, d // tn),
        in_specs=[pl.BlockSpec((tm, k), lambda i, j: (i, 0), pipeline_mode=pl.Buffered(1)),
                  pl.BlockSpec((None, k, tn), lambda i, j: (layer, 0, j)),
                  pl.BlockSpec((tm, tn), lambda i, j: (i, j)),
                  pl.BlockSpec((1, 1, tn), lambda i, j: (i // nb, 0, j))],
        out_specs=pl.BlockSpec((tm, tn), lambda i, j: (i, j)),
        out_shape=jax.ShapeDtypeStruct((m, d), F32),
        compiler_params=_params("arbitrary", "arbitrary"),
        name="matmul_residual",
    )(a, w3, x, gt.reshape(-1, 1, d))


def _nm_swiglu_body(x_ref, g_ref, sc_ref, sh_ref, w1_ref, w3_ref, o_ref, h_scr):
    @pl.when(pl.program_id(1) == 0)
    def _():
        h_scr[...] = _norm_mod(x_ref[...], g_ref[...], sc_ref[0], sh_ref[0]).astype(BF16)

    h = h_scr[...]
    a = _dot(h, w1_ref[...].astype(BF16))
    b = _dot(h, w3_ref[...].astype(BF16))
    o_ref[...] = (a * _sigmoid(a) * b).astype(o_ref.dtype)


def _nm_swiglu(x, g, sc, sh, w1, w3, layer, seq, tm=2048, tf=256):
    m, d = x.shape
    f = w1.shape[2]
    tm = min(tm, seq)
    nb = seq // tm
    return pl.pallas_call(
        _nm_swiglu_body,
        grid=(m // tm, f // tf),
        in_specs=[pl.BlockSpec((tm, d), lambda i, j: (i, 0), pipeline_mode=pl.Buffered(1)),
                  pl.BlockSpec((1, d), lambda i, j: (0, 0)),
                  pl.BlockSpec((1, 1, d), lambda i, j: (i // nb, 0, 0)),
                  pl.BlockSpec((1, 1, d), lambda i, j: (i // nb, 0, 0)),
                  pl.BlockSpec((None, d, tf), lambda i, j: (layer, 0, j)),
                  pl.BlockSpec((None, d, tf), lambda i, j: (layer, 0, j))],
        out_specs=pl.BlockSpec((tm, tf), lambda i, j: (i, j)),
        out_shape=jax.ShapeDtypeStruct((m, f), BF16),
        scratch_shapes=[pltpu.VMEM((tm, d), BF16)],
        compiler_params=_params("arbitrary", "arbitrary"),
        name="norm_mod_swiglu_up",
    )(x, g.reshape(1, d), sc.reshape(-1, 1, d), sh.reshape(-1, 1, d), w1, w3)


def _dsa_body(q_ref, qi_ref, k_ref, v_ref, tq_ref, tk_ref, pr_ref, pc_ref, tab_ref, gq_ref, gk_ref,
              o_ref,
              kn_scr, vt_scr, kke_scr, kko_scr, qst_scr, wib_scr, key_scr, qn_scr, m_scr, l_scr, acc_scr,
              *, topk, tq, seq):
    i = pl.program_id(1)
    nc = i + 1
    grp = A_HEADS // A_KV_HEADS
    hd = A_HEAD_DIM
    lane = lax.broadcasted_iota(I32, (1, LANES), 1)
    krow = lax.broadcasted_iota(I32, (tq, tq), 0)
    qcol = lax.broadcasted_iota(I32, (tq, tq), 1)

    @pl.when(i == 0)
    def _prepare_keys():
        gk = gk_ref[...]

        def body(r, carry):
            rows = pl.ds(pl.multiple_of(r * tq, tq), tq)
            for kh in range(A_KV_HEADS):
                cols = slice(kh * hd, (kh + 1) * hd)
                kb = k_ref[rows, cols].astype(F32)
                ms = jnp.mean(kb * kb, axis=-1, keepdims=True)
                kn_scr[rows, cols] = ((kb * lax.rsqrt(ms + EPS)) * gk).astype(BF16)
                vt_scr[r, cols, :] = v_ref[rows, cols].astype(F32).T.astype(BF16)
            t = tk_ref[rows, :].astype(F32)
            kke_scr[rows, :] = jnp.where(lane < IDX_DIM, t, 0.0).astype(BF16)
            kko_scr[rows, :] = jnp.where(lane >= IDX_DIM, pltpu.roll(t, IDX_DIM, 1), 0.0).astype(BF16)
            return carry

        lax.fori_loop(0, seq // tq, body, 0)

    for j in range(IDX_HEADS // 2):
        qst_scr[j * tq:(j + 1) * tq, :] = qi_ref[:, j * LANES:(j + 1) * LANES]
    w_scale = (IDX_DIM ** -0.5) * (IDX_HEADS ** -0.5)
    wib_scr[...] = tq_ref[...].astype(F32).T[IDX_DIM:IDX_DIM + IDX_HEADS, :] * w_scale

    def idx_body(c, carry):
        rows = pl.ds(pl.multiple_of(c * tq, tq), tq)
        qst = qst_scr[...]
        re = _dot_nt(kke_scr[rows, :], qst)
        ro = _dot_nt(kko_scr[rows, :], qst)
        acc = jnp.zeros((tq, tq), F32)
        for j in range(IDX_HEADS // 2):
            acc = acc + jnp.maximum(re[:, j * tq:(j + 1) * tq], 0.0) * wib_scr[2 * j:2 * j + 1, :]
            acc = acc + jnp.maximum(ro[:, j * tq:(j + 1) * tq], 0.0) * wib_scr[2 * j + 1:2 * j + 2, :]
        bits = pltpu.bitcast(acc, I32)
        key = bits ^ ((bits >> 31) & 0x7FFFFFFF)
        causal = (c < i) | (krow <= qcol)
        key_scr[c] = jnp.where(causal, key, INT_MIN)
        return carry

    lax.fori_loop(0, nc, idx_body, 0)

    def bit_body(bi, t_u):
        cand_u = t_u | lax.shift_left(jnp.int32(1), 31 - bi)
        cand_s = cand_u ^ INT_MIN

        def cnt_body(c, cnt):
            ge = jnp.where(key_scr[c] >= cand_s, 1, 0)
            return cnt + jnp.sum(ge.reshape(tq // 8, 8, tq), axis=0)

        cnt = lax.fori_loop(0, nc, cnt_body, jnp.zeros((8, tq), I32))
        total = jnp.sum(cnt, axis=0, keepdims=True)
        return jnp.where(total >= topk, cand_u, t_u)

    nbits = jnp.where(nc * tq > topk, 32, 0)
    t_u = lax.fori_loop(0, nbits, bit_body, jnp.zeros((1, tq), I32))
    thr = t_u ^ INT_MIN

    qpos = pr_ref[0, i]

    def bkt_body(c, carry):
        rows = pl.ds(pl.multiple_of(c * tq, tq), tq)
        bkt = _t5_bucket(qpos - pc_ref[rows, :])
        causal = (c < i) | (krow <= qcol)
        sel = (key_scr[c] >= thr) & causal
        key_scr[c] = jnp.where(sel, bkt, N_BUCKETS)
        return carry

    lax.fori_loop(0, nc, bkt_body, 0)

    gq = gq_ref[...]
    scale = (hd ** -0.5) * LOG2E
    for g in range(A_KV_HEADS):
        for hh in range(grp):
            h = g * grp + hh
            qh = q_ref[:, h * hd:(h + 1) * hd].astype(F32)
            ms = jnp.mean(qh * qh, axis=-1, keepdims=True)
            qn_scr[hh * tq:(hh + 1) * tq, :] = (((qh * lax.rsqrt(ms + EPS)) * gq) * scale).astype(BF16)
        m_scr[...] = jnp.full(m_scr.shape, NEG, F32)
        l_scr[...] = jnp.zeros(l_scr.shape, F32)
        acc_scr[...] = jnp.zeros(acc_scr.shape, F32)

        def att_body(c, carry, g=g):
            rows = pl.ds(pl.multiple_of(c * tq, tq), tq)
            s = _dot_nt(kn_scr[rows, g * hd:(g + 1) * hd], qn_scr[...])
            bkt = key_scr[c]
            parts = []
            for hh in range(grp):
                h = g * grp + hh
                parts.append(s[:, hh * tq:(hh + 1) * tq] + _bias_lookup(tab_ref[h:h + 1, :], bkt, tq))
            s = jnp.concatenate(parts, axis=1)
            m_old = m_scr[...]
            m_new = jnp.maximum(m_old, jnp.max(s, axis=0, keepdims=True))
            p = jnp.exp2(s - m_new)
            alpha = jnp.exp2(m_old - m_new)
            l_scr[...] = alpha * l_scr[...] + jnp.sum(p, axis=0, keepdims=True)
            acc_scr[...] = alpha * acc_scr[...] + _dot(vt_scr[c, g * hd:(g + 1) * hd, :], p.astype(BF16))
            m_scr[...] = m_new
            return carry

        lax.fori_loop(0, nc, att_body, 0)
        o = acc_scr[...] * (1.0 / l_scr[...])
        for hh in range(grp):
            h = g * grp + hh
            o_ref[:, h * hd:(h + 1) * hd] = o[:, hh * tq:(hh + 1) * tq].T.astype(o_ref.dtype)


def _dsa_attention(qkv, tail, pos, tab, gq, gk, batch, seq, tq=256):
    tq = min(tq, seq)
    nq = seq // tq
    a_q = A_HEADS * A_HEAD_DIM
    a_kv = A_KV_HEADS * A_HEAD_DIM
    a_qi = IDX_HEADS * IDX_DIM
    topk = min(TOPK_MAX, seq // 4)
    grp = A_HEADS // A_KV_HEADS
    body = functools.partial(_dsa_body, topk=topk, tq=tq, seq=seq)
    return pl.pallas_call(
        body,
        grid=(batch, nq),
        in_specs=[
            pl.BlockSpec((tq, a_q), lambda b, i: (b * nq + i, 0)),
            pl.BlockSpec((tq, a_qi), lambda b, i: (b * nq + i, (a_q + 2 * a_kv) // a_qi)),
            pl.BlockSpec((seq, a_kv), lambda b, i: (b, a_q // a_kv)),
            pl.BlockSpec((seq, a_kv), lambda b, i: (b, a_q // a_kv + 1)),
            pl.BlockSpec((tq, LANES), lambda b, i: (b * nq + i, 0)),
            pl.BlockSpec((seq, LANES), lambda b, i: (b, 0)),
            pl.BlockSpec((1, nq, 1, tq), lambda b, i: (b, 0, 0, 0)),
            pl.BlockSpec((seq, 1), lambda b, i: (b, 0)),
            pl.BlockSpec((A_HEADS, LANES), lambda b, i: (0, 0)),
            pl.BlockSpec((1, A_HEAD_DIM), lambda b, i: (0, 0)),
            pl.BlockSpec((1, A_HEAD_DIM), lambda b, i: (0, 0)),
        ],
        out_specs=pl.BlockSpec((tq, a_q), lambda b, i: (b * nq + i, 0)),
        out_shape=jax.ShapeDtypeStruct((batch * seq, a_q), BF16),
        scratch_shapes=[
            pltpu.VMEM((seq, a_kv), BF16),
            pltpu.VMEM((nq, a_kv, tq), BF16),
            pltpu.VMEM((seq, LANES), BF16),
            pltpu.VMEM((seq, LANES), BF16),
            pltpu.VMEM((IDX_HEADS // 2 * tq, LANES), BF16),
            pltpu.VMEM((IDX_HEADS, tq), F32),
            pltpu.VMEM((nq, tq, tq), I32),
            pltpu.VMEM((grp * tq, A_HEAD_DIM), BF16),
            pltpu.VMEM((1, grp * tq), F32),
            pltpu.VMEM((1, grp * tq), F32),
            pltpu.VMEM((A_HEAD_DIM, grp * tq), F32),
        ],
        compiler_params=_params("arbitrary", "arbitrary"),
        name="dsa_attention",
    )(qkv, qkv, qkv, qkv, tail, tail, pos.reshape(batch, nq, 1, tq), pos.reshape(batch * seq, 1),
      tab, gq, gk)


def _win_body(*refs, cls_len, wk, use_prev, tq):
    if use_prev:
        (q_ref, kc_ref, vc_ref, kp_ref, vp_ref, pr_ref, pcc_ref, pcp_ref, tab_ref, gq_ref, gk_ref,
         o_ref, lse_ref, kn_scr, vt_scr, qn_scr, bkt_scr, bias_scr, lse_scr) = refs
    else:
        (q_ref, kc_ref, vc_ref, pr_ref, pcc_ref, tab_ref, gq_ref, gk_ref,
         o_ref, lse_ref, kn_scr, vt_scr, qn_scr, bkt_scr, bias_scr, lse_scr) = refs
    t = pl.program_id(1)
    hd = B_HEAD_DIM
    pw = 2 * hd
    blk = LANES
    shift = cls_len.bit_length() - 1
    koff = blk if use_prev else 0
    win = 2 * blk if cls_len > blk else blk
    lo = lax.broadcasted_iota(I32, (1, pw), 1) < hd
    lo_rows = lax.broadcasted_iota(I32, (pw, 1), 0) < hd
    gq = gq_ref[...]
    gk = gk_ref[...]
    scale = (hd ** -0.5) * LOG2E

    def pair_norm(x, g):
        sq = x * x
        ms_lo = jnp.sum(jnp.where(lo, sq, 0.0), axis=-1, keepdims=True) * (1.0 / hd)
        ms_hi = jnp.sum(jnp.where(lo, 0.0, sq), axis=-1, keepdims=True) * (1.0 / hd)
        inv = jnp.where(lo, lax.rsqrt(ms_lo + EPS), lax.rsqrt(ms_hi + EPS))
        return (x * inv) * g

    for p in range(B_HEADS // 2):
        cols = slice(p * pw, (p + 1) * pw)
        qn_scr[:, cols] = (pair_norm(q_ref[:, cols].astype(F32), gq) * scale).astype(BF16)
        if use_prev:
            kn_scr[0:blk, cols] = pair_norm(kp_ref[:, cols].astype(F32), gk).astype(BF16)
            vt_scr[p, :, 0:blk] = vp_ref[:, cols].astype(F32).T.astype(BF16)
        kn_scr[koff:koff + tq, cols] = pair_norm(kc_ref[:, cols].astype(F32), gk).astype(BF16)
        vt_scr[p, :, koff:koff + tq] = vc_ref[:, cols].astype(F32).T.astype(BF16)

    lse_scr[...] = jnp.zeros(lse_scr.shape, F32)
    qpos_all = pr_ref[0]
    for j in range(tq // blk):
        qrows = slice(j * blk, (j + 1) * blk)
        own = koff + j * blk
        k0 = own - blk if (win > blk and own >= blk) else own
        if use_prev and k0 < koff:
            kpos = jnp.concatenate([pcp_ref[k0:koff, :], pcc_ref[0:k0 + win - koff, :]], axis=0)
        else:
            kpos = pcc_ref[k0 - koff:k0 - koff + win, :]
        fq = t * tq + j * blk + lax.broadcasted_iota(I32, (1, blk), 1)
        fk = t * tq + (k0 - koff) + lax.broadcasted_iota(I32, (win, 1), 0)
        same = ((fq + cls_len) >> shift) == ((fk + cls_len) >> shift)
        dist = (fq & (cls_len - 1)) - (fk & (cls_len - 1))
        ok = same & (dist >= 0) & (dist <= wk)
        bkt_scr[...] = jnp.where(ok, _t5_bucket(qpos_all[:, qrows] - kpos), N_BUCKETS)

        def bias_rows(r8, carry):
            rows = pl.ds(pl.multiple_of(r8 * 8, 8), 8)
            idx = bkt_scr[rows, :]
            for h in range(B_HEADS):
                tab = jnp.broadcast_to(tab_ref[h:h + 1, :], (8, LANES))
                bias_scr[h // 2, rows, (h % 2) * blk:(h % 2 + 1) * blk] = jnp.take_along_axis(tab, idx, axis=1)
            return carry

        lax.fori_loop(0, win // 8, bias_rows, 0, unroll=8)

        for p in range(B_HEADS // 2):
            cols = slice(p * pw, (p + 1) * pw)
            qp = qn_scr[qrows, cols]
            zero = jnp.zeros_like(qp)
            heads = (jnp.where(lo, qp, zero), jnp.where(lo, zero, qp))
            kw = kn_scr[k0:k0 + win, cols]
            vtw = vt_scr[p, :, k0:k0 + win]
            hp = 2 * blk // win
            outs = []
            for c in range(2 // hp):
                q2 = heads[c] if hp == 1 else jnp.concatenate(heads, axis=0)
                s = _dot_nt(kw, q2) + bias_scr[p, :, c * hp * blk:(c + 1) * hp * blk]
                m = jnp.max(s, axis=0, keepdims=True)
                e = jnp.exp2(s - m)
                l = jnp.sum(e, axis=0, keepdims=True)
                ot = _dot(vtw, e.astype(BF16)) * (1.0 / l)
                lse = m + jnp.log(l) * LOG2E
                for k in range(hp):
                    h = 2 * p + c * hp + k
                    outs.append(ot[:, k * blk:(k + 1) * blk])
                    lse_scr[h:h + 1, :] = lse[:, k * blk:(k + 1) * blk]
            o_ref[qrows, cols] = jnp.where(lo_rows, outs[0], outs[1]).T.astype(o_ref.dtype)
        lse_ref[qrows, :] = lse_scr[...].T


def _win_attention(q_arr, q_col, k_arr, k_col, v_arr, v_col, pos, tab, gq2, gk2,
                   batch, seq, dilation, window, tq=512):
    cls_len = seq // dilation
    wk = window // dilation
    assert cls_len & (cls_len - 1) == 0 and wk <= LANES
    tq = min(tq, seq)
    use_prev = cls_len > tq
    assert use_prev or tq % cls_len == 0
    nt = seq // tq
    w = B_HEADS * B_HEAD_DIM
    sub = tq // LANES
    pos_col = pos.reshape(batch * seq, 1)
    pos_row = pos.reshape(batch * nt, 1, tq)
    prev = lambda b, t: jnp.maximum((b * nt + t) * sub - 1, 0)
    in_specs = [pl.BlockSpec((tq, w), lambda b, t: (b * nt + t, q_col)),
                pl.BlockSpec((tq, w), lambda b, t: (b * nt + t, k_col)),
                pl.BlockSpec((tq, w), lambda b, t: (b * nt + t, v_col))]
    args = [q_arr, k_arr, v_arr]
    if use_prev:
        in_specs += [pl.BlockSpec((LANES, w), lambda b, t: (prev(b, t), k_col)),
                     pl.BlockSpec((LANES, w), lambda b, t: (prev(b, t), v_col))]
        args += [k_arr, v_arr]
    in_specs += [pl.BlockSpec((1, 1, tq), lambda b, t: (b * nt + t, 0, 0)),
                 pl.BlockSpec((tq, 1), lambda b, t: (b * nt + t, 0))]
    args += [pos_row, pos_col]
    if use_prev:
        in_specs.append(pl.BlockSpec((LANES, 1), lambda b, t: (prev(b, t), 0)))
        args.append(pos_col)
    in_specs += [pl.BlockSpec((B_HEADS, LANES), lambda b, t: (0, 0)),
                 pl.BlockSpec((1, LANES), lambda b, t: (0, 0)),
                 pl.BlockSpec((1, LANES), lambda b, t: (0, 0))]
    args += [tab, gq2, gk2]
    nk = tq + (LANES if use_prev else 0)
    win = 2 * LANES if cls_len > LANES else LANES
    body = functools.partial(_win_body, cls_len=cls_len, wk=wk, use_prev=use_prev, tq=tq)
    return pl.pallas_call(
        body,
        grid=(batch, nt),
        in_specs=in_specs,
        out_specs=[pl.BlockSpec((tq, w), lambda b, t: (b * nt + t, 0)),
                   pl.BlockSpec((tq, LANES), lambda b, t: (b * nt + t, 0))],
        out_shape=[jax.ShapeDtypeStruct((batch * seq, w), BF16),
                   jax.ShapeDtypeStruct((batch * seq, LANES), F32)],
        scratch_shapes=[pltpu.VMEM((nk, w), BF16),
                        pltpu.VMEM((B_HEADS // 2, LANES, nk), BF16),
                        pltpu.VMEM((tq, w), BF16),
                        pltpu.VMEM((win, LANES), I32),
                        pltpu.VMEM((B_HEADS // 2, win, 2 * LANES), F32),
                        pltpu.VMEM((LANES, LANES), F32)],
        compiler_params=_params("arbitrary", "arbitrary"),
        name="dilated_attention",
    )(*args)


def _merge_body(o0, o1, o2, l0, l1, l2, e_ref, w_ref, x_ref, gt_ref, out_ref, a_scr):
    @pl.when(pl.program_id(1) == 0)
    def _():
        ls = (l0[...], l1[...], l2[...])
        m = jnp.maximum(jnp.maximum(ls[0], ls[1]), ls[2])
        ws = [jnp.exp2(l - m) for l in ls]
        inv = 1.0 / (ws[0] + ws[1] + ws[2])
        num = None
        for wg, og in zip(ws, (o0, o1, o2)):
            wn = wg * inv
            hi = wn.astype(BF16)
            lo = (wn - hi.astype(F32)).astype(BF16)
            wide = _dot(hi, e_ref[...]) + _dot(lo, e_ref[...])
            term = wide * og[...].astype(F32)
            num = term if num is None else num + term
        a_scr[...] = num.astype(BF16)

    out_ref[...] = x_ref[...] + gt_ref[0] * _dot(a_scr[...], w_ref[...].astype(BF16))


def _merge_out(os_, ls_, w3, layer, x, gt, seq, tm=512, tn=512):
    m, k = os_[0].shape
    d = x.shape[1]
    nb = seq // tm
    head = jnp.arange(k, dtype=I32)[None, :] // B_HEAD_DIM
    expand = (jnp.arange(LANES, dtype=I32)[:, None] == head).astype(BF16)
    row = pl.BlockSpec((tm, k), lambda i, j: (i, 0))
    lrow = pl.BlockSpec((tm, LANES), lambda i, j: (i, 0))
    return pl.pallas_call(
        _merge_body,
        grid=(m // tm, d // tn),
        in_specs=[row] * 3 + [lrow] * 3 + [
            pl.BlockSpec((LANES, k), lambda i, j: (0, 0)),
            pl.BlockSpec((None, k, tn), lambda i, j: (layer, 0, j)),
            pl.BlockSpec((tm, tn), lambda i, j: (i, j)),
            pl.BlockSpec((1, 1, tn), lambda i, j: (i // nb, 0, j))],
        out_specs=pl.BlockSpec((tm, tn), lambda i, j: (i, j)),
        out_shape=jax.ShapeDtypeStruct((m, d), F32),
        scratch_shapes=[pltpu.VMEM((tm, k), BF16)],
        compiler_params=_params("arbitrary", "arbitrary"),
        name="merge_out_proj",
    )(*os_, *ls_, expand, w3, x, gt.reshape(-1, 1, d))


def _router_body(x_ref, g_ref, sc_ref, sh_ref, rh_ref, rl_ref, rb_ref, h_ref, rt_ref):
    h = _norm_mod(x_ref[...], g_ref[...], sc_ref[0], sh_ref[0])
    h_ref[...] = h
    hh = h.astype(BF16)
    hl = (h - hh.astype(F32)).astype(BF16)
    logits = _dot(hh, rh_ref[...]) + _dot(hl, rh_ref[...]) + _dot(hh, rl_ref[...]) + rb_ref[...]
    lane = lax.broadcasted_iota(I32, logits.shape, 1)
    logits = jnp.where(lane < N_EXPERTS, logits, NEG)
    v1 = jnp.max(logits, axis=1, keepdims=True)
    i1 = jnp.min(jnp.where(logits == v1, lane, LANES), axis=1, keepdims=True)
    rest = jnp.where(lane == i1, NEG, logits)
    v2 = jnp.max(rest, axis=1, keepdims=True)
    i2 = jnp.min(jnp.where(rest == v2, lane, LANES), axis=1, keepdims=True)
    e = jnp.exp(v2 - v1)
    g1 = 1.0 / (1.0 + e)
    g2 = e * g1
    rt_ref[...] = jnp.where(lane == 0, i1.astype(F32),
                            jnp.where(lane == 1, i2.astype(F32),
                                      jnp.where(lane == 2, g1, jnp.where(lane == 3, g2, 0.0))))


def _router(x, g, sc, sh, rw, rb, seq, tm=512):
    m, d = x.shape
    nb = seq // tm
    ne = rw.shape[1]
    rw_p = jnp.zeros((d, LANES), F32).at[:, :ne].set(rw)
    rh = rw_p.astype(BF16)
    rl = (rw_p - rh.astype(F32)).astype(BF16)
    rb_p = jnp.zeros((1, LANES), F32).at[0, :ne].set(rb)
    return pl.pallas_call(
        _router_body,
        grid=(m // tm,),
        in_specs=[pl.BlockSpec((tm, d), lambda i: (i, 0)),
                  pl.BlockSpec((1, d), lambda i: (0, 0)),
                  pl.BlockSpec((1, 1, d), lambda i: (i // nb, 0, 0)),
                  pl.BlockSpec((1, 1, d), lambda i: (i // nb, 0, 0)),
                  pl.BlockSpec((d, LANES), lambda i: (0, 0)),
                  pl.BlockSpec((d, LANES), lambda i: (0, 0)),
                  pl.BlockSpec((1, LANES), lambda i: (0, 0))],
        out_specs=[pl.BlockSpec((tm, d), lambda i: (i, 0)),
                   pl.BlockSpec((tm, LANES), lambda i: (i, 0))],
        out_shape=[jax.ShapeDtypeStruct((m, d), F32),
                   jax.ShapeDtypeStruct((m, LANES), F32)],
        compiler_params=_params("arbitrary"),
        name="router_top2",
    )(x, g.reshape(1, d), sc.reshape(-1, 1, d), sh.reshape(-1, 1, d), rh, rl, rb_p)


def _row_copy(src_hbm, idx, buf, r, sem):
    return pltpu.make_async_copy(src_hbm.at[pl.ds(idx, 1), :], buf.at[pl.ds(r, 1), :], sem)


def _gather_body(src_ref, nact_ref, h_hbm, o_ref, buf, sem, *, rows):
    i = pl.program_id(0)
    n_active = nact_ref[0]

    def issue(step):
        slot = step % 2

        def body(r, carry):
            _row_copy(h_hbm, src_ref[step * rows + r], buf.at[slot], r, sem.at[slot]).start()
            return carry

        lax.fori_loop(0, rows, body, 0, unroll=8)

    @pl.when(i == 0)
    def _():
        issue(i)

    @pl.when((i + 1) * rows < n_active)
    def _():
        issue(i + 1)

    slot = i % 2

    @pl.when(i * rows < n_active)
    def _():
        pltpu.make_async_copy(h_hbm.at[pl.ds(0, rows), :], buf.at[slot], sem.at[slot]).wait()
        o_ref[...] = buf[slot].astype(o_ref.dtype)

    @pl.when(i * rows >= n_active)
    def _():
        o_ref[...] = jnp.zeros(o_ref.shape, o_ref.dtype)


def _gather_rows(h, src, n_active, rows):
    r_total = src.shape[0]
    d = h.shape[1]
    return pl.pallas_call(
        functools.partial(_gather_body, rows=rows),
        grid_spec=pltpu.PrefetchScalarGridSpec(
            num_scalar_prefetch=2,
            grid=(r_total // rows,),
            in_specs=[pl.BlockSpec(memory_space=pl.ANY)],
            out_specs=pl.BlockSpec((rows, d), lambda i, s, n: (i, 0)),
            scratch_shapes=[pltpu.VMEM((2, rows, d), F32), pltpu.SemaphoreType.DMA((2,))],
        ),
        out_shape=jax.ShapeDtypeStruct((r_total, d), BF16),
        compiler_params=_params("arbitrary"),
        name="moe_dispatch_gather",
    )(src, n_active, h)


def _moe_body(te_ref, tv_ref, hs_ref, w1_ref, w3_ref, w2_ref, o_ref, *, ncol):
    t = pl.program_id(0)
    f = pl.program_id(1)
    valid = tv_ref[t]
    d = o_ref.shape[1]
    cw = d // ncol

    @pl.when(valid > 0)
    def _():
        h = hs_ref[...]
        a = _dot(h, w1_ref[...].astype(BF16))
        b = _dot(h, w3_ref[...].astype(BF16))
        u = (a * _sigmoid(a) * b).astype(BF16)
        w2 = w2_ref[...].astype(BF16)

        @pl.when(f == 0)
        def _():
            for n in range(ncol):
                o_ref[:, n * cw:(n + 1) * cw] = _dot(u, w2[:, n * cw:(n + 1) * cw])

        @pl.when(f > 0)
        def _():
            for n in range(ncol):
                o_ref[:, n * cw:(n + 1) * cw] += _dot(u, w2[:, n * cw:(n + 1) * cw])

    @pl.when((valid == 0) & (f == 0))
    def _():
        o_ref[...] = jnp.zeros(o_ref.shape, o_ref.dtype)


def _moe_experts(hs, tile_expert, tile_valid, w1, w3, w2, tm, tf=256, ncol=4):
    r_total, d = hs.shape
    n_tiles = r_total // tm
    nf = w1.shape[2] // tf

    def w13_map(t, f, te, tv):
        return (te[t], 0, jnp.where(tv[t] > 0, f, nf - 1))

    def w2_map(t, f, te, tv):
        return (te[t], jnp.where(tv[t] > 0, f, nf - 1), 0)

    return pl.pallas_call(
        functools.partial(_moe_body, ncol=ncol),
        grid_spec=pltpu.PrefetchScalarGridSpec(
            num_scalar_prefetch=2,
            grid=(n_tiles, nf),
            in_specs=[pl.BlockSpec((tm, d), lambda t, f, te, tv: (t, 0)),
                      pl.BlockSpec((None, d, tf), w13_map),
                      pl.BlockSpec((None, d, tf), w13_map),
                      pl.BlockSpec((None, tf, d), w2_map)],
            out_specs=pl.BlockSpec((tm, d), lambda t, f, te, tv: (t, 0)),
        ),
        out_shape=jax.ShapeDtypeStruct((r_total, d), F32),
        compiler_params=_params("arbitrary", "arbitrary"),
        name="moe_experts",
    )(tile_expert, tile_valid, hs, w1, w3, w2)


def _combine_body(p1_ref, p2_ref, y_hbm, x_ref, gt_ref, rt_ref, o_ref, buf_a, buf_b, sem, *, rows):
    base = pl.program_id(0) * rows

    def issue(r, carry):
        _row_copy(y_hbm, p1_ref[base + r], buf_a, r, sem).start()
        _row_copy(y_hbm, p2_ref[base + r], buf_b, r, sem).start()
        return carry

    lax.fori_loop(0, rows, issue, 0, unroll=4)
    pltpu.make_async_copy(y_hbm.at[pl.ds(0, rows), :], buf_a, sem).wait()
    pltpu.make_async_copy(y_hbm.at[pl.ds(0, rows), :], buf_b, sem).wait()
    rt = rt_ref[...]
    g1 = rt[:, 2:3]
    g2 = rt[:, 3:4]
    o_ref[...] = x_ref[...] + gt_ref[0] * (g1 * buf_a[...] + g2 * buf_b[...])


def _combine(ys, p1, p2, x, gt, route, seq, rows=256):
    m, d = x.shape
    nb = seq // rows
    return pl.pallas_call(
        functools.partial(_combine_body, rows=rows),
        grid_spec=pltpu.PrefetchScalarGridSpec(
            num_scalar_prefetch=2,
            grid=(m // rows,),
            in_specs=[pl.BlockSpec(memory_space=pl.ANY),
                      pl.BlockSpec((rows, d), lambda i, a, b: (i, 0)),
                      pl.BlockSpec((1, 1, d), lambda i, a, b: (i // nb, 0, 0)),
                      pl.BlockSpec((rows, LANES), lambda i, a, b: (i, 0))],
            out_specs=pl.BlockSpec((rows, d), lambda i, a, b: (i, 0)),
            scratch_shapes=[pltpu.VMEM((rows, d), F32), pltpu.VMEM((rows, d), F32),
                            pltpu.SemaphoreType.DMA(())],
        ),
        out_shape=jax.ShapeDtypeStruct((m, d), F32),
        compiler_params=_params("arbitrary"),
        name="moe_combine",
    )(p1, p2, ys, x, gt.reshape(-1, 1, d), route)


def _routing_tables(route, tm, n_tiles):
    t = route.shape[0]
    experts = route[:, :2].astype(I32).reshape(-1)
    onehot = (experts[:, None] == jnp.arange(N_EXPERTS, dtype=I32)[None, :]).astype(I32)
    csum = jnp.cumsum(onehot, axis=0)
    rank = jnp.sum(csum * onehot, axis=1) - 1
    counts = csum[-1]
    tiles = (counts + tm - 1) // tm
    tend = jnp.cumsum(tiles)
    tstart = tend - tiles
    slot = tstart[experts] * tm + rank
    token = jnp.arange(2 * t, dtype=I32) // 2
    src = jnp.zeros((n_tiles * tm,), I32).at[slot].set(token)
    tile_id = jnp.arange(n_tiles, dtype=I32)
    te = jnp.sum((tile_id[:, None] >= tend[None, :]).astype(I32), axis=1)
    active = tile_id < tend[-1]
    last_e = jnp.sum((tend[-1] - 1 >= tend).astype(I32))
    te = jnp.where(active, te, last_e)
    tv = jnp.where(active, jnp.clip(counts[te] - (tile_id - tstart[te]) * tm, 0, tm), 0)
    slots = slot.reshape(t, 2)
    n_active = (tend[-1] * tm).astype(I32).reshape(1)
    return src, n_active, te.astype(I32), tv.astype(I32), slots[:, 0], slots[:, 1]


def _bias_table(rel_bias):
    h = rel_bias.shape[1]
    return jnp.full((h, LANES), NEG, F32).at[:, :N_BUCKETS].set(rel_bias.T * LOG2E)


def _to_classes(a, batch, seq, r):
    if r == 1:
        return a
    w = a.shape[-1]
    return a.reshape(batch, seq // r, r, w).transpose(0, 2, 1, 3).reshape(batch * seq, w)


def _from_classes(a, batch, seq, r):
    if r == 1:
        return a
    w = a.shape[-1]
    return a.reshape(batch, r, seq // r, w).transpose(0, 2, 1, 3).reshape(batch * seq, w)


def kernel(x, c, positions, rel_bias, w_mod, b_mod, g_attn, g_ffn, a_w_in, a_w_out, a_g_qn, a_g_kn,
           kv_w_mod, kv_b_mod, kv_g, kv_w, b_g_kn, b_w_q, b_w_out, b_g_qn, ffn_w1, ffn_w3, ffn_w2,
           moe_router, moe_router_b, moe_w1, moe_w3, moe_w2):
    batch, seq, d = x.shape
    m = batch * seq
    x2 = x.reshape(m, d)
    positions = positions.astype(I32)

    c8 = jnp.zeros((8, d), F32).at[:batch].set(c)
    mod0 = _mod_call(c8, w_mod, 0, b_mod)[:batch]
    mod1 = _mod_call(c8, w_mod, 1, b_mod)[:batch]
    kvm = _mod_call(c8, kv_w_mod[None], 0, kv_b_mod[None])[:batch]
    sh1_0, sc1_0, gt1_0, sh2_0, sc2_0, gt2_0 = jnp.split(mod0, 6, axis=-1)
    sh1_1, sc1_1, gt1_1, sh2_1, sc2_1, gt2_1 = jnp.split(mod1, 6, axis=-1)
    kv_sh, kv_sc = jnp.split(kvm, 2, axis=-1)

    tab = _bias_table(rel_bias)

    a_main = A_HEADS * A_HEAD_DIM + 2 * A_KV_HEADS * A_HEAD_DIM + IDX_HEADS * IDX_DIM
    n_tail = IDX_DIM + IDX_HEADS
    w_tail = jnp.zeros((d, LANES), F32).at[:, :n_tail].set(a_w_in[0, :, a_main:a_main + n_tail])
    qkv, tail = _nm_matmul(x2, g_attn[0], sc1_0, sh1_0, a_w_in, 0, a_main, seq, w_tail=w_tail)
    attn = _dsa_attention(qkv, tail, positions, tab,
                          a_g_qn[0].reshape(1, -1), a_g_kn[0].reshape(1, -1), batch, seq)
    x2 = _matmul_residual(attn, a_w_out, 0, x2, gt1_0, seq, tn=512)
    u = _nm_swiglu(x2, g_ffn[0], sc2_0, sh2_0, ffn_w1, ffn_w3, 0, seq)
    x2 = _matmul_residual(u, ffn_w2, 0, x2, gt2_0, seq)

    b_q = len(B_DILATIONS) * B_HEADS * B_HEAD_DIM
    kvall = _nm_matmul(x2, kv_g, kv_sc, kv_sh, kv_w[None], 0, 2 * b_q, seq)
    qall = _nm_matmul(x2, g_attn[1], sc1_1, sh1_1, b_w_q, 0, b_q, seq)
    gq2 = jnp.tile(b_g_qn[0], 2).reshape(1, LANES)
    gk2 = jnp.tile(b_g_kn, 2).reshape(1, LANES)
    w = B_HEADS * B_HEAD_DIM
    ng = len(B_DILATIONS)
    outs, lses = [], []
    for g, (window, r) in enumerate(B_DILATIONS):
        if r == 1:
            q_arr, q_col, k_arr, k_col, v_arr, v_col = qall, g, kvall, g, kvall, ng + g
            pos_g = positions
        else:
            q_arr = _to_classes(qall[:, g * w:(g + 1) * w], batch, seq, r)
            k_arr = _to_classes(kvall[:, g * w:(g + 1) * w], batch, seq, r)
            v_arr = _to_classes(kvall[:, (ng + g) * w:(ng + g + 1) * w], batch, seq, r)
            q_col = k_col = v_col = 0
            pos_g = positions.reshape(batch, seq // r, r).transpose(0, 2, 1).reshape(batch, seq)
        o_g, lse_g = _win_attention(q_arr, q_col, k_arr, k_col, v_arr, v_col, pos_g, tab, gq2, gk2,
                                    batch, seq, r, window)
        outs.append(_from_classes(o_g, batch, seq, r))
        lses.append(_from_classes(lse_g, batch, seq, r))
    x2 = _merge_out(outs, lses, b_w_out, 0, x2, gt1_1, seq)

    h, route = _router(x2, g_ffn[1], sc2_1, sh2_1, moe_router[0], moe_router_b[0], seq)
    share = (2 * m) // N_EXPERTS
    tm = -(-(share * 17 // 32) // 64) * 64
    n_tiles = (2 * m) // tm + N_EXPERTS
    src, n_active, te, tv, p1, p2 = _routing_tables(route, tm, n_tiles)
    hs = _gather_rows(h, src, n_active, tm // 4)
    ys = _moe_experts(hs, te, tv, moe_w1.reshape(moe_w1.shape[1:]), moe_w3.reshape(moe_w3.shape[1:]),
                      moe_w2.reshape(moe_w2.shape[1:]), tm)
    out = _combine(ys, p1, p2, x2, gt2_1, route, seq)
    return out.reshape(batch, seq, d)
```

```python
import functools
import math

import jax
import jax.numpy as jnp
from jax import lax
from jax.experimental import pallas as pl
from jax.experimental.pallas import tpu as pltpu

F32 = jnp.float32
BF16 = jnp.bfloat16
I32 = jnp.int32

EPS = 1e-6
NEG = -1e30
INT_MIN = -(2 ** 31)
LOG2E = 1.0 / math.log(2.0)

A_HEADS, A_KV_HEADS, A_HEAD_DIM = 16, 4, 128
IDX_HEADS, IDX_DIM = 16, 64
TOPK_MAX = 256
B_DILATIONS = ((128, 1), (512, 4), (2048, 16))
B_HEADS, B_HEAD_DIM = 16, 64
N_BUCKETS, MAX_DISTANCE = 32, 2048
N_EXPERTS = 8
LANES = 128

VMEM_LIMIT_BYTES = 56 * 1024 * 1024

_NT = (((1,), (1,)), ((), ()))


def _params(*sem):
    return pltpu.CompilerParams(dimension_semantics=sem, vmem_limit_bytes=VMEM_LIMIT_BYTES)


def _dot(a, b):
    return jnp.dot(a, b, preferred_element_type=F32)


def _dot_nt(a, b):
    return lax.dot_general(a, b, _NT, preferred_element_type=F32)


def _sigmoid(x):
    return 1.0 / (1.0 + jnp.exp(-x))


def _t5_bucket(rel):
    n = jnp.maximum(rel, 0)
    max_exact = N_BUCKETS // 2
    nf = jnp.maximum(n, 1).astype(F32)
    large = max_exact + (jnp.log(nf / max_exact) / math.log(MAX_DISTANCE / max_exact)
                         * (N_BUCKETS - max_exact)).astype(I32)
    large = jnp.minimum(large, N_BUCKETS - 1)
    return jnp.where(n < max_exact, n, large)


def _norm_mod(x, g, sc, sh):
    y = x * lax.rsqrt(jnp.mean(x * x, axis=-1, keepdims=True) + EPS)
    return (y * g) * (1.0 + sc) + sh


def _bias_lookup(tab_ref, heads, bkt):
    rows, cols = bkt.shape
    tabs = [jnp.broadcast_to(tab_ref[h:h + 1, :], (8, LANES)) for h in heads]
    tiles = [[] for _ in tabs]
    for r in range(rows // 8):
        pieces = [[] for _ in tabs]
        for c in range(cols // LANES):
            idx = bkt[r * 8:(r + 1) * 8, c * LANES:(c + 1) * LANES]
            for k, tab in enumerate(tabs):
                pieces[k].append(jnp.take_along_axis(tab, idx, axis=1))
        for k in range(len(tabs)):
            tiles[k].append(pieces[k][0] if len(pieces[k]) == 1 else jnp.concatenate(pieces[k], axis=1))
    return [jnp.concatenate(t, axis=0) for t in tiles]


def _mod_body(c_ref, w_ref, b_ref, o_ref):
    c = c_ref[...]
    cs = c * _sigmoid(c)
    o_ref[...] = _dot(cs.astype(BF16), w_ref[...].astype(BF16)) + b_ref[...]


def _mod_call(c8, w3, layer, b2):
    _, d, n = w3.shape
    tn = 1024
    return pl.pallas_call(
        _mod_body,
        grid=(n // tn,),
        in_specs=[pl.BlockSpec((8, d), lambda j: (0, 0)),
                  pl.BlockSpec((None, d, tn), lambda j: (layer, 0, j)),
                  pl.BlockSpec((None, 1, tn), lambda j: (layer, 0, j))],
        out_specs=pl.BlockSpec((8, tn), lambda j: (0, j)),
        out_shape=jax.ShapeDtypeStruct((8, n), F32),
        compiler_params=_params("arbitrary"),
        name="adaln_mod",
    )(c8, w3, b2.reshape(b2.shape[0], 1, n))


def _nm_body(x_ref, g_ref, sc_ref, sh_ref, w_ref, *rest, has_tail):
    if has_tail:
        wt_ref, o_ref, ot_ref, h_scr = rest
    else:
        o_ref, h_scr = rest

    @pl.when(pl.program_id(1) == 0)
    def _():
        h = _norm_mod(x_ref[...], g_ref[...], sc_ref[0], sh_ref[0]).astype(BF16)
        h_scr[...] = h
        if has_tail:
            ot_ref[...] = _dot(h, wt_ref[...].astype(BF16)).astype(ot_ref.dtype)

    o_ref[...] = _dot(h_scr[...], w_ref[...].astype(BF16)).astype(o_ref.dtype)


def _nm_matmul(x, g, sc, sh, w3, layer, n_cols, seq, w_tail=None, tm=2048, tn=512):
    m, d = x.shape
    tm = min(tm, seq)
    nb = seq // tm
    has_tail = w_tail is not None
    in_specs = [pl.BlockSpec((tm, d), lambda i, j: (i, 0), pipeline_mode=pl.Buffered(1)),
                pl.BlockSpec((1, d), lambda i, j: (0, 0)),
                pl.BlockSpec((1, 1, d), lambda i, j: (i // nb, 0, 0)),
                pl.BlockSpec((1, 1, d), lambda i, j: (i // nb, 0, 0)),
                pl.BlockSpec((None, d, tn), lambda i, j: (layer, 0, j))]
    args = [x, g.reshape(1, d), sc.reshape(-1, 1, d), sh.reshape(-1, 1, d), w3]
    out_specs = [pl.BlockSpec((tm, tn), lambda i, j: (i, j))]
    out_shape = [jax.ShapeDtypeStruct((m, n_cols), BF16)]
    if has_tail:
        in_specs.append(pl.BlockSpec((d, LANES), lambda i, j: (0, 0)))
        args.append(w_tail)
        out_specs.append(pl.BlockSpec((tm, LANES), lambda i, j: (i, 0)))
        out_shape.append(jax.ShapeDtypeStruct((m, LANES), BF16))
    res = pl.pallas_call(
        functools.partial(_nm_body, has_tail=has_tail),
        grid=(m // tm, n_cols // tn),
        in_specs=in_specs,
        out_specs=out_specs,
        out_shape=out_shape,
        scratch_shapes=[pltpu.VMEM((tm, d), BF16)],
        compiler_params=_params("arbitrary", "arbitrary"),
        name="norm_mod_matmul",
    )(*args)
    return res if has_tail else res[0]


def _mmres_body(a_ref, w_ref, x_ref, gt_ref, o_ref):
    o_ref[...] = x_ref[...] + gt_ref[0] * _dot(a_ref[...], w_ref[...].astype(BF16))


def _matmul_residual(a, w3, layer, x, gt, seq, tm=2048, tn=256):
    m, k = a.shape
    d = x.shape[1]
    tm = min(tm, seq)
    nb = seq // tm
    return pl.pallas_call(
        _mmres_body,
        grid=(m // tm, d // tn),
        in_specs=[pl.BlockSpec((tm, k), lambda i, j: (i, 0), pipeline_mode=pl.Buffered(1)),
                  pl.BlockSpec((None, k, tn), lambda i, j: (layer, 0, j)),
                  pl.BlockSpec((tm, tn), lambda i, j: (i, j)),
                  pl.BlockSpec((1, 1, tn), lambda i, j: (i // nb, 0, j))],
        out_specs=pl.BlockSpec((tm, tn), lambda i, j: (i, j)),
        out_shape=jax.ShapeDtypeStruct((m, d), F32),
        compiler_params=_params("arbitrary", "arbitrary"),
        name="matmul_residual",
    )(a, w3, x, gt.reshape(-1, 1, d))


def _nm_swiglu_body(x_ref, g_ref, sc_ref, sh_ref, w1_ref, w3_ref, o_ref, h_scr):
    @pl.when(pl.program_id(1) == 0)
    def _():
        h_scr[...] = _norm_mod(x_ref[...], g_ref[...], sc_ref[0], sh_ref[0]).astype(BF16)

    h = h_scr[...]
    a = _dot(h, w1_ref[...].astype(BF16))
    b = _dot(h, w3_ref[...].astype(BF16))
    o_ref[...] = (a * _sigmoid(a) * b).astype(o_ref.dtype)


def _nm_swiglu(x, g, sc, sh, w1, w3, layer, seq, tm=2048, tf=256):
    m, d = x.shape
    f = w1.shape[2]
    tm = min(tm, seq)
    nb = seq // tm
    return pl.pallas_call(
        _nm_swiglu_body,
        grid=(m // tm, f // tf),
        in_specs=[pl.BlockSpec((tm, d), lambda i, j: (i, 0), pipeline_mode=pl.Buffered(1)),
                  pl.BlockSpec((1, d), lambda i, j: (0, 0)),
                  pl.BlockSpec((1, 1, d), lambda i, j: (i // nb, 0, 0)),
                  pl.BlockSpec((1, 1, d), lambda i, j: (i // nb, 0, 0)),
                  pl.BlockSpec((None, d, tf), lambda i, j: (layer, 0, j)),
                  pl.BlockSpec((None, d, tf), lambda i, j: (layer, 0, j))],
        out_specs=pl.BlockSpec((tm, tf), lambda i, j: (i, j)),
        out_shape=jax.ShapeDtypeStruct((m, f), BF16),
        scratch_shapes=[pltpu.VMEM((tm, d), BF16)],
        compiler_params=_params("arbitrary", "arbitrary"),
        name="norm_mod_swiglu_up",
    )(x, g.reshape(1, d), sc.reshape(-1, 1, d), sh.reshape(-1, 1, d), w1, w3)


def _dsa_body(q_ref, qi_ref, k_ref, v_ref, tq_ref, tk_ref, pr_ref, pc_ref, tab_ref, gq_ref, gk_ref,
              o_ref,
              kn_scr, vt_scr, kke_scr, kko_scr, qst_scr, wib_scr, key_scr, qn_scr, m_scr, l_scr, acc_scr,
              s_buf, cm_buf,
              *, topk, tq, seq):
    i = pl.program_id(1)
    nc = i + 1
    grp = A_HEADS // A_KV_HEADS
    hd = A_HEAD_DIM
    lane = lax.broadcasted_iota(I32, (1, LANES), 1)
    krow = lax.broadcasted_iota(I32, (tq, tq), 0)
    qcol = lax.broadcasted_iota(I32, (tq, tq), 1)

    @pl.when(i == 0)
    def _prepare_keys():
        gk = gk_ref[...]

        def body(r, carry):
            rows = pl.ds(pl.multiple_of(r * tq, tq), tq)
            for kh in range(A_KV_HEADS):
                cols = slice(kh * hd, (kh + 1) * hd)
                kb = k_ref[rows, cols].astype(F32)
                ms = jnp.mean(kb * kb, axis=-1, keepdims=True)
                kn_scr[rows, cols] = ((kb * lax.rsqrt(ms + EPS)) * gk).astype(BF16)
                vt_scr[r, cols, :] = v_ref[rows, cols].astype(F32).T.astype(BF16)
            t = tk_ref[rows, :].astype(F32)
            kke_scr[rows, :] = jnp.where(lane < IDX_DIM, t, 0.0).astype(BF16)
            kko_scr[rows, :] = jnp.where(lane >= IDX_DIM, pltpu.roll(t, IDX_DIM, 1), 0.0).astype(BF16)
            return carry

        lax.fori_loop(0, seq // tq, body, 0)

    for j in range(IDX_HEADS // 2):
        qst_scr[j * tq:(j + 1) * tq, :] = qi_ref[:, j * LANES:(j + 1) * LANES]
    w_scale = (IDX_DIM ** -0.5) * (IDX_HEADS ** -0.5)
    wib_scr[...] = tq_ref[...].astype(F32).T[IDX_DIM:IDX_DIM + IDX_HEADS, :] * w_scale

    def idx_body(c, carry):
        rows = pl.ds(pl.multiple_of(c * tq, tq), tq)
        qst = qst_scr[...]
        re = _dot_nt(kke_scr[rows, :], qst)
        ro = _dot_nt(kko_scr[rows, :], qst)
        acc = jnp.zeros((tq, tq), F32)
        for j in range(IDX_HEADS // 2):
            acc = acc + jnp.maximum(re[:, j * tq:(j + 1) * tq], 0.0) * wib_scr[2 * j:2 * j + 1, :]
            acc = acc + jnp.maximum(ro[:, j * tq:(j + 1) * tq], 0.0) * wib_scr[2 * j + 1:2 * j + 2, :]
        bits = pltpu.bitcast(acc, I32)
        key = bits ^ ((bits >> 31) & 0x7FFFFFFF)
        causal = (c < i) | (krow <= qcol)
        key_scr[c] = jnp.where(causal, key, INT_MIN)
        return carry

    lax.fori_loop(0, nc, idx_body, 0)

    def bit_body(bi, t_u):
        cand_u = t_u | lax.shift_left(jnp.int32(1), 31 - bi)
        cand_s = cand_u ^ INT_MIN

        def cnt_body(c, cnt):
            ge = jnp.where(key_scr[c] >= cand_s, 1, 0)
            return cnt + jnp.sum(ge.reshape(tq // 8, 8, tq), axis=0)

        cnt = lax.fori_loop(0, nc, cnt_body, jnp.zeros((8, tq), I32))
        total = jnp.sum(cnt, axis=0, keepdims=True)
        return jnp.where(total >= topk, cand_u, t_u)

    nbits = jnp.where(nc * tq > topk, 32, 0)
    t_u = lax.fori_loop(0, nbits, bit_body, jnp.zeros((1, tq), I32))
    thr = t_u ^ INT_MIN

    qpos = pr_ref[0, i]

    def bkt_body(c, carry):
        rows = pl.ds(pl.multiple_of(c * tq, tq), tq)
        bkt = _t5_bucket(qpos - pc_ref[rows, :])
        causal = (c < i) | (krow <= qcol)
        sel = (key_scr[c] >= thr) & causal
        key_scr[c] = jnp.where(sel, bkt, N_BUCKETS)
        return carry

    lax.fori_loop(0, nc, bkt_body, 0)

    gq = gq_ref[...]
    scale = (hd ** -0.5) * LOG2E
    for g in range(A_KV_HEADS):
        for hh in range(grp):
            h = g * grp + hh
            qh = q_ref[:, h * hd:(h + 1) * hd].astype(F32)
            ms = jnp.mean(qh * qh, axis=-1, keepdims=True)
            qn_scr[hh * tq:(hh + 1) * tq, :] = (((qh * lax.rsqrt(ms + EPS)) * gq) * scale).astype(BF16)
        m_scr[...] = jnp.full(m_scr.shape, NEG, F32)
        l_scr[...] = jnp.zeros(l_scr.shape, F32)
        acc_scr[...] = jnp.zeros(acc_scr.shape, F32)

        def scores(c, g=g):
            rows = pl.ds(pl.multiple_of(c * tq, tq), tq)
            s = _dot_nt(kn_scr[rows, g * hd:(g + 1) * hd], qn_scr[...])
            bias = _bias_lookup(tab_ref, range(g * grp, (g + 1) * grp), key_scr[c])
            s = jnp.concatenate([s[:, hh * tq:(hh + 1) * tq] + bias[hh] for hh in range(grp)], axis=1)
            s_buf[c % 2] = s
            cm_buf[c % 2] = jnp.max(s, axis=0, keepdims=True)

        def accumulate(c, g=g):
            s = s_buf[c % 2]
            m_old = m_scr[...]
            m_new = jnp.maximum(m_old, cm_buf[c % 2])
            p = jnp.exp2(s - m_new)
            alpha = jnp.exp2(m_old - m_new)
            l_scr[...] = alpha * l_scr[...] + jnp.sum(p, axis=0, keepdims=True)
            acc_scr[...] = alpha * acc_scr[...] + _dot(vt_scr[c, g * hd:(g + 1) * hd, :], p.astype(BF16))
            m_scr[...] = m_new

        def att_body(c, carry):
            accumulate(c)
            scores(c + 1)
            return carry

        scores(0)
        lax.fori_loop(0, nc - 1, att_body, 0)
        accumulate(nc - 1)
        o = acc_scr[...] * (1.0 / l_scr[...])
        for hh in range(grp):
            h = g * grp + hh
            o_ref[:, h * hd:(h + 1) * hd] = o[:, hh * tq:(hh + 1) * tq].T.astype(o_ref.dtype)


def _dsa_attention(qkv, tail, pos, tab, gq, gk, batch, seq, tq=256):
    tq = min(tq, seq)
    nq = seq // tq
    a_q = A_HEADS * A_HEAD_DIM
    a_kv = A_KV_HEADS * A_HEAD_DIM
    a_qi = IDX_HEADS * IDX_DIM
    topk = min(TOPK_MAX, seq // 4)
    grp = A_HEADS // A_KV_HEADS
    body = functools.partial(_dsa_body, topk=topk, tq=tq, seq=seq)
    return pl.pallas_call(
        body,
        grid=(batch, nq),
        in_specs=[
            pl.BlockSpec((tq, a_q), lambda b, i: (b * nq + i, 0)),
            pl.BlockSpec((tq, a_qi), lambda b, i: (b * nq + i, (a_q + 2 * a_kv) // a_qi)),
            pl.BlockSpec((seq, a_kv), lambda b, i: (b, a_q // a_kv)),
            pl.BlockSpec((seq, a_kv), lambda b, i: (b, a_q // a_kv + 1)),
            pl.BlockSpec((tq, LANES), lambda b, i: (b * nq + i, 0)),
            pl.BlockSpec((seq, LANES), lambda b, i: (b, 0)),
            pl.BlockSpec((1, nq, 1, tq), lambda b, i: (b, 0, 0, 0)),
            pl.BlockSpec((seq, 1), lambda b, i: (b, 0)),
            pl.BlockSpec((A_HEADS, LANES), lambda b, i: (0, 0)),
            pl.BlockSpec((1, A_HEAD_DIM), lambda b, i: (0, 0)),
            pl.BlockSpec((1, A_HEAD_DIM), lambda b, i: (0, 0)),
        ],
        out_specs=pl.BlockSpec((tq, a_q), lambda b, i: (b * nq + i, 0)),
        out_shape=jax.ShapeDtypeStruct((batch * seq, a_q), BF16),
        scratch_shapes=[
            pltpu.VMEM((seq, a_kv), BF16),
            pltpu.VMEM((nq, a_kv, tq), BF16),
            pltpu.VMEM((seq, LANES), BF16),
            pltpu.VMEM((seq, LANES), BF16),
            pltpu.VMEM((IDX_HEADS // 2 * tq, LANES), BF16),
            pltpu.VMEM((IDX_HEADS, tq), F32),
            pltpu.VMEM((nq, tq, tq), I32),
            pltpu.VMEM((grp * tq, A_HEAD_DIM), BF16),
            pltpu.VMEM((1, grp * tq), F32),
            pltpu.VMEM((1, grp * tq), F32),
            pltpu.VMEM((A_HEAD_DIM, grp * tq), F32),
            pltpu.VMEM((2, tq, grp * tq), F32),
            pltpu.VMEM((2, 1, grp * tq), F32),
        ],
        compiler_params=_params("arbitrary", "arbitrary"),
        name="dsa_attention",
    )(qkv, qkv, qkv, qkv, tail, tail, pos.reshape(batch, nq, 1, tq), pos.reshape(batch * seq, 1),
      tab, gq, gk)


def _win_body(*refs, cls_len, wk, use_prev, tq):
    if use_prev:
        (q_ref, kc_ref, vc_ref, kp_ref, vp_ref, pr_ref, pcc_ref, pcp_ref, tab_ref, gq_ref, gk_ref,
         o_ref, lse_ref, kn_scr, vt_scr, qn_scr, bkt_scr, bias_scr, lse_scr) = refs
    else:
        (q_ref, kc_ref, vc_ref, pr_ref, pcc_ref, tab_ref, gq_ref, gk_ref,
         o_ref, lse_ref, kn_scr, vt_scr, qn_scr, bkt_scr, bias_scr, lse_scr) = refs
    t = pl.program_id(1)
    hd = B_HEAD_DIM
    pw = 2 * hd
    blk = LANES
    shift = cls_len.bit_length() - 1
    koff = blk if use_prev else 0
    win = 2 * blk if cls_len > blk else blk
    lo = lax.broadcasted_iota(I32, (1, pw), 1) < hd
    lo_rows = lax.broadcasted_iota(I32, (pw, 1), 0) < hd
    gq = gq_ref[...]
    gk = gk_ref[...]
    scale = (hd ** -0.5) * LOG2E

    def pair_norm(x, g):
        sq = x * x
        ms_lo = jnp.sum(jnp.where(lo, sq, 0.0), axis=-1, keepdims=True) * (1.0 / hd)
        ms_hi = jnp.sum(jnp.where(lo, 0.0, sq), axis=-1, keepdims=True) * (1.0 / hd)
        inv = jnp.where(lo, lax.rsqrt(ms_lo + EPS), lax.rsqrt(ms_hi + EPS))
        return (x * inv) * g

    for p in range(B_HEADS // 2):
        cols = slice(p * pw, (p + 1) * pw)
        qn_scr[:, cols] = (pair_norm(q_ref[:, cols].astype(F32), gq) * scale).astype(BF16)
        if use_prev:
            kn_scr[0:blk, cols] = pair_norm(kp_ref[:, cols].astype(F32), gk).astype(BF16)
            vt_scr[p, :, 0:blk] = vp_ref[:, cols].astype(F32).T.astype(BF16)
        kn_scr[koff:koff + tq, cols] = pair_norm(kc_ref[:, cols].astype(F32), gk).astype(BF16)
        vt_scr[p, :, koff:koff + tq] = vc_ref[:, cols].astype(F32).T.astype(BF16)

    lse_scr[...] = jnp.zeros(lse_scr.shape, F32)
    qpos_all = pr_ref[0]
    for j in range(tq // blk):
        qrows = slice(j * blk, (j + 1) * blk)
        own = koff + j * blk
        k0 = own - blk if (win > blk and own >= blk) else own
        if use_prev and k0 < koff:
            kpos = jnp.concatenate([pcp_ref[k0:koff, :], pcc_ref[0:k0 + win - koff, :]], axis=0)
        else:
            kpos = pcc_ref[k0 - koff:k0 - koff + win, :]
        fq = t * tq + j * blk + lax.broadcasted_iota(I32, (1, blk), 1)
        fk = t * tq + (k0 - koff) + lax.broadcasted_iota(I32, (win, 1), 0)
        same = ((fq + cls_len) >> shift) == ((fk + cls_len) >> shift)
        dist = (fq & (cls_len - 1)) - (fk & (cls_len - 1))
        ok = same & (dist >= 0) & (dist <= wk)
        bkt_scr[...] = jnp.where(ok, _t5_bucket(qpos_all[:, qrows] - kpos), N_BUCKETS)

        def bias_rows(r8, carry):
            rows = pl.ds(pl.multiple_of(r8 * 8, 8), 8)
            idx = bkt_scr[rows, :]
            for h in range(B_HEADS):
                tab = jnp.broadcast_to(tab_ref[h:h + 1, :], (8, LANES))
                bias_scr[h // 2, rows, (h % 2) * blk:(h % 2 + 1) * blk] = jnp.take_along_axis(tab, idx, axis=1)
            return carry

        lax.fori_loop(0, win // 8, bias_rows, 0, unroll=8)

        for p in range(B_HEADS // 2):
            cols = slice(p * pw, (p + 1) * pw)
            qp = qn_scr[qrows, cols]
            zero = jnp.zeros_like(qp)
            heads = (jnp.where(lo, qp, zero), jnp.where(lo, zero, qp))
            kw = kn_scr[k0:k0 + win, cols]
            vtw = vt_scr[p, :, k0:k0 + win]
            hp = 2 * blk // win
            outs = []
            for c in range(2 // hp):
                q2 = heads[c] if hp == 1 else jnp.concatenate(heads, axis=0)
                s = _dot_nt(kw, q2) + bias_scr[p, :, c * hp * blk:(c + 1) * hp * blk]
                m = jnp.max(s, axis=0, keepdims=True)
                e = jnp.exp2(s - m)
                l = jnp.sum(e, axis=0, keepdims=True)
                ot = _dot(vtw, e.astype(BF16)) * (1.0 / l)
                lse = m + jnp.log(l) * LOG2E
                for k in range(hp):
                    h = 2 * p + c * hp + k
                    outs.append(ot[:, k * blk:(k + 1) * blk])
                    lse_scr[h:h + 1, :] = lse[:, k * blk:(k + 1) * blk]
            o_ref[qrows, cols] = jnp.where(lo_rows, outs[0], outs[1]).T.astype(o_ref.dtype)
        lse_ref[qrows, :] = lse_scr[...].T


def _win_attention(q_arr, q_col, k_arr, k_col, v_arr, v_col, pos, tab, gq2, gk2,
                   batch, seq, dilation, window, tq=512):
    cls_len = seq // dilation
    wk = window // dilation
    assert cls_len & (cls_len - 1) == 0 and wk <= LANES
    tq = min(tq, seq)
    use_prev = cls_len > tq
    assert use_prev or tq % cls_len == 0
    nt = seq // tq
    w = B_HEADS * B_HEAD_DIM
    sub = tq // LANES
    pos_col = pos.reshape(batch * seq, 1)
    pos_row = pos.reshape(batch * nt, 1, tq)
    prev = lambda b, t: jnp.maximum((b * nt + t) * sub - 1, 0)
    in_specs = [pl.BlockSpec((tq, w), lambda b, t: (b * nt + t, q_col)),
                pl.BlockSpec((tq, w), lambda b, t: (b * nt + t, k_col)),
                pl.BlockSpec((tq, w), lambda b, t: (b * nt + t, v_col))]
    args = [q_arr, k_arr, v_arr]
    if use_prev:
        in_specs += [pl.BlockSpec((LANES, w), lambda b, t: (prev(b, t), k_col)),
                     pl.BlockSpec((LANES, w), lambda b, t: (prev(b, t), v_col))]
        args += [k_arr, v_arr]
    in_specs += [pl.BlockSpec((1, 1, tq), lambda b, t: (b * nt + t, 0, 0)),
                 pl.BlockSpec((tq, 1), lambda b, t: (b * nt + t, 0))]
    args += [pos_row, pos_col]
    if use_prev:
        in_specs.append(pl.BlockSpec((LANES, 1), lambda b, t: (prev(b, t), 0)))
        args.append(pos_col)
    in_specs += [pl.BlockSpec((B_HEADS, LANES), lambda b, t: (0, 0)),
                 pl.BlockSpec((1, LANES), lambda b, t: (0, 0)),
                 pl.BlockSpec((1, LANES), lambda b, t: (0, 0))]
    args += [tab, gq2, gk2]
    nk = tq + (LANES if use_prev else 0)
    win = 2 * LANES if cls_len > LANES else LANES
    body = functools.partial(_win_body, cls_len=cls_len, wk=wk, use_prev=use_prev, tq=tq)
    return pl.pallas_call(
        body,
        grid=(batch, nt),
        in_specs=in_specs,
        out_specs=[pl.BlockSpec((tq, w), lambda b, t: (b * nt + t, 0)),
                   pl.BlockSpec((tq, LANES), lambda b, t: (b * nt + t, 0))],
        out_shape=[jax.ShapeDtypeStruct((batch * seq, w), BF16),
                   jax.ShapeDtypeStruct((batch * seq, LANES), F32)],
        scratch_shapes=[pltpu.VMEM((nk, w), BF16),
                        pltpu.VMEM((B_HEADS // 2, LANES, nk), BF16),
                        pltpu.VMEM((tq, w), BF16),
                        pltpu.VMEM((win, LANES), I32),
                        pltpu.VMEM((B_HEADS // 2, win, 2 * LANES), F32),
                        pltpu.VMEM((LANES, LANES), F32)],
        compiler_params=_params("arbitrary", "arbitrary"),
        name="dilated_attention",
    )(*args)


def _merge_body(o0, o1, o2, l0, l1, l2, e_ref, w_ref, x_ref, gt_ref, out_ref, a_scr):
    @pl.when(pl.program_id(1) == 0)
    def _():
        ls = (l0[...], l1[...], l2[...])
        m = jnp.maximum(jnp.maximum(ls[0], ls[1]), ls[2])
        ws = [jnp.exp2(l - m) for l in ls]
        inv = 1.0 / (ws[0] + ws[1] + ws[2])
        num = None
        for wg, og in zip(ws, (o0, o1, o2)):
            wn = wg * inv
            hi = wn.astype(BF16)
            lo = (wn - hi.astype(F32)).astype(BF16)
            wide = _dot(hi, e_ref[...]) + _dot(lo, e_ref[...])
            term = wide * og[...].astype(F32)
            num = term if num is None else num + term
        a_scr[...] = num.astype(BF16)

    out_ref[...] = x_ref[...] + gt_ref[0] * _dot(a_scr[...], w_ref[...].astype(BF16))


def _merge_out(os_, ls_, w3, layer, x, gt, seq, tm=512, tn=512):
    m, k = os_[0].shape
    d = x.shape[1]
    nb = seq // tm
    head = jnp.arange(k, dtype=I32)[None, :] // B_HEAD_DIM
    expand = (jnp.arange(LANES, dtype=I32)[:, None] == head).astype(BF16)
    row = pl.BlockSpec((tm, k), lambda i, j: (i, 0))
    lrow = pl.BlockSpec((tm, LANES), lambda i, j: (i, 0))
    return pl.pallas_call(
        _merge_body,
        grid=(m // tm, d // tn),
        in_specs=[row] * 3 + [lrow] * 3 + [
            pl.BlockSpec((LANES, k), lambda i, j: (0, 0)),
            pl.BlockSpec((None, k, tn), lambda i, j: (layer, 0, j)),
            pl.BlockSpec((tm, tn), lambda i, j: (i, j)),
            pl.BlockSpec((1, 1, tn), lambda i, j: (i // nb, 0, j))],
        out_specs=pl.BlockSpec((tm, tn), lambda i, j: (i, j)),
        out_shape=jax.ShapeDtypeStruct((m, d), F32),
        scratch_shapes=[pltpu.VMEM((tm, k), BF16)],
        compiler_params=_params("arbitrary", "arbitrary"),
        name="merge_out_proj",
    )(*os_, *ls_, expand, w3, x, gt.reshape(-1, 1, d))


def _router_body(x_ref, g_ref, sc_ref, sh_ref, rh_ref, rl_ref, rb_ref, h_ref, rt_ref):
    h = _norm_mod(x_ref[...], g_ref[...], sc_ref[0], sh_ref[0])
    h_ref[...] = h
    hh = h.astype(BF16)
    hl = (h - hh.astype(F32)).astype(BF16)
    logits = _dot(hh, rh_ref[...]) + _dot(hl, rh_ref[...]) + _dot(hh, rl_ref[...]) + rb_ref[...]
    lane = lax.broadcasted_iota(I32, logits.shape, 1)
    logits = jnp.where(lane < N_EXPERTS, logits, NEG)
    v1 = jnp.max(logits, axis=1, keepdims=True)
    i1 = jnp.min(jnp.where(logits == v1, lane, LANES), axis=1, keepdims=True)
    rest = jnp.where(lane == i1, NEG, logits)
    v2 = jnp.max(rest, axis=1, keepdims=True)
    i2 = jnp.min(jnp.where(rest == v2, lane, LANES), axis=1, keepdims=True)
    e = jnp.exp(v2 - v1)
    g1 = 1.0 / (1.0 + e)
    g2 = e * g1
    rt_ref[...] = jnp.where(lane == 0, i1.astype(F32),
                            jnp.where(lane == 1, i2.astype(F32),
                                      jnp.where(lane == 2, g1, jnp.where(lane == 3, g2, 0.0))))


def _router(x, g, sc, sh, rw, rb, seq, tm=512):
    m, d = x.shape
    nb = seq // tm
    ne = rw.shape[1]
    rw_p = jnp.zeros((d, LANES), F32).at[:, :ne].set(rw)
    rh = rw_p.astype(BF16)
    rl = (rw_p - rh.astype(F32)).astype(BF16)
    rb_p = jnp.zeros((1, LANES), F32).at[0, :ne].set(rb)
    return pl.pallas_call(
        _router_body,
        grid=(m // tm,),
        in_specs=[pl.BlockSpec((tm, d), lambda i: (i, 0)),
                  pl.BlockSpec((1, d), lambda i: (0, 0)),
                  pl.BlockSpec((1, 1, d), lambda i: (i // nb, 0, 0)),
                  pl.BlockSpec((1, 1, d), lambda i: (i // nb, 0, 0)),
                  pl.BlockSpec((d, LANES), lambda i: (0, 0)),
                  pl.BlockSpec((d, LANES), lambda i: (0, 0)),
                  pl.BlockSpec((1, LANES), lambda i: (0, 0))],
        out_specs=[pl.BlockSpec((tm, d), lambda i: (i, 0)),
                   pl.BlockSpec((tm, LANES), lambda i: (i, 0))],
        out_shape=[jax.ShapeDtypeStruct((m, d), F32),
                   jax.ShapeDtypeStruct((m, LANES), F32)],
        compiler_params=_params("arbitrary"),
        name="router_top2",
    )(x, g.reshape(1, d), sc.reshape(-1, 1, d), sh.reshape(-1, 1, d), rh, rl, rb_p)


def _row_copy(src_hbm, idx, buf, r, sem):
    return pltpu.make_async_copy(src_hbm.at[pl.ds(idx, 1), :], buf.at[pl.ds(r, 1), :], sem)


def _gather_body(src_ref, nact_ref, h_hbm, o_ref, buf, sem, *, rows):
    i = pl.program_id(0)
    n_active = nact_ref[0]

    def issue(step):
        slot = step % 2

        def body(r, carry):
            _row_copy(h_hbm, src_ref[step * rows + r], buf.at[slot], r, sem.at[slot]).start()
            return carry

        lax.fori_loop(0, rows, body, 0, unroll=8)

    @pl.when(i == 0)
    def _():
        issue(i)

    @pl.when((i + 1) * rows < n_active)
    def _():
        issue(i + 1)

    slot = i % 2

    @pl.when(i * rows < n_active)
    def _():
        pltpu.make_async_copy(h_hbm.at[pl.ds(0, rows), :], buf.at[slot], sem.at[slot]).wait()
        o_ref[...] = buf[slot].astype(o_ref.dtype)

    @pl.when(i * rows >= n_active)
    def _():
        o_ref[...] = jnp.zeros(o_ref.shape, o_ref.dtype)


def _gather_rows(h, src, n_active, rows):
    r_total = src.shape[0]
    d = h.shape[1]
    return pl.pallas_call(
        functools.partial(_gather_body, rows=rows),
        grid_spec=pltpu.PrefetchScalarGridSpec(
            num_scalar_prefetch=2,
            grid=(r_total // rows,),
            in_specs=[pl.BlockSpec(memory_space=pl.ANY)],
            out_specs=pl.BlockSpec((rows, d), lambda i, s, n: (i, 0)),
            scratch_shapes=[pltpu.VMEM((2, rows, d), F32), pltpu.SemaphoreType.DMA((2,))],
        ),
        out_shape=jax.ShapeDtypeStruct((r_total, d), BF16),
        compiler_params=_params("arbitrary"),
        name="moe_dispatch_gather",
    )(src, n_active, h)


def _moe_body(te_ref, tv_ref, hs_ref, w1_ref, w3_ref, w2_ref, o_ref, *, ncol):
    t = pl.program_id(0)
    f = pl.program_id(1)
    valid = tv_ref[t]
    d = o_ref.shape[1]
    cw = d // ncol

    @pl.when(valid > 0)
    def _():
        h = hs_ref[...]
        a = _dot(h, w1_ref[...].astype(BF16))
        b = _dot(h, w3_ref[...].astype(BF16))
        u = (a * _sigmoid(a) * b).astype(BF16)
        w2 = w2_ref[...].astype(BF16)

        @pl.when(f == 0)
        def _():
            for n in range(ncol):
                o_ref[:, n * cw:(n + 1) * cw] = _dot(u, w2[:, n * cw:(n + 1) * cw])

        @pl.when(f > 0)
        def _():
            for n in range(ncol):
                o_ref[:, n * cw:(n + 1) * cw] += _dot(u, w2[:, n * cw:(n + 1) * cw])

    @pl.when((valid == 0) & (f == 0))
    def _():
        o_ref[...] = jnp.zeros(o_ref.shape, o_ref.dtype)


def _moe_experts(hs, tile_expert, tile_valid, w1, w3, w2, tm, tf=256, ncol=4):
    r_total, d = hs.shape
    n_tiles = r_total // tm
    nf = w1.shape[2] // tf

    def w13_map(t, f, te, tv):
        return (te[t], 0, jnp.where(tv[t] > 0, f, nf - 1))

    def w2_map(t, f, te, tv):
        return (te[t], jnp.where(tv[t] > 0, f, nf - 1), 0)

    return pl.pallas_call(
        functools.partial(_moe_body, ncol=ncol),
        grid_spec=pltpu.PrefetchScalarGridSpec(
            num_scalar_prefetch=2,
            grid=(n_tiles, nf),
            in_specs=[pl.BlockSpec((tm, d), lambda t, f, te, tv: (t, 0)),
                      pl.BlockSpec((None, d, tf), w13_map),
                      pl.BlockSpec((None, d, tf), w13_map),
                      pl.BlockSpec((None, tf, d), w2_map)],
            out_specs=pl.BlockSpec((tm, d), lambda t, f, te, tv: (t, 0)),
        ),
        out_shape=jax.ShapeDtypeStruct((r_total, d), F32),
        compiler_params=_params("arbitrary", "arbitrary"),
        name="moe_experts",
    )(tile_expert, tile_valid, hs, w1, w3, w2)


def _combine_body(p1_ref, p2_ref, y_hbm, x_ref, gt_ref, rt_ref, o_ref, buf_a, buf_b, sem, *, rows):
    base = pl.program_id(0) * rows

    def issue(r, carry):
        _row_copy(y_hbm, p1_ref[base + r], buf_a, r, sem).start()
        _row_copy(y_hbm, p2_ref[base + r], buf_b, r, sem).start()
        return carry

    lax.fori_loop(0, rows, issue, 0, unroll=4)
    pltpu.make_async_copy(y_hbm.at[pl.ds(0, rows), :], buf_a, sem).wait()
    pltpu.make_async_copy(y_hbm.at[pl.ds(0, rows), :], buf_b, sem).wait()
    rt = rt_ref[...]
    g1 = rt[:, 2:3]
    g2 = rt[:, 3:4]
    o_ref[...] = x_ref[...] + gt_ref[0] * (g1 * buf_a[...] + g2 * buf_b[...])


def _combine(ys, p1, p2, x, gt, route, seq, rows=256):
    m, d = x.shape
    nb = seq // rows
    return pl.pallas_call(
        functools.partial(_combine_body, rows=rows),
        grid_spec=pltpu.PrefetchScalarGridSpec(
            num_scalar_prefetch=2,
            grid=(m // rows,),
            in_specs=[pl.BlockSpec(memory_space=pl.ANY),
                      pl.BlockSpec((rows, d), lambda i, a, b: (i, 0)),
                      pl.BlockSpec((1, 1, d), lambda i, a, b: (i // nb, 0, 0)),
                      pl.BlockSpec((rows, LANES), lambda i, a, b: (i, 0))],
            out_specs=pl.BlockSpec((rows, d), lambda i, a, b: (i, 0)),
            scratch_shapes=[pltpu.VMEM((rows, d), F32), pltpu.VMEM((rows, d), F32),
                            pltpu.SemaphoreType.DMA(())],
        ),
        out_shape=jax.ShapeDtypeStruct((m, d), F32),
        compiler_params=_params("arbitrary"),
        name="moe_combine",
    )(p1, p2, ys, x, gt.reshape(-1, 1, d), route)


def _routing_tables(route, tm, n_tiles):
    t = route.shape[0]
    experts = route[:, :2].astype(I32).reshape(-1)
    onehot = (experts[:, None] == jnp.arange(N_EXPERTS, dtype=I32)[None, :]).astype(I32)
    csum = jnp.cumsum(onehot, axis=0)
    rank = jnp.sum(csum * onehot, axis=1) - 1
    counts = csum[-1]
    tiles = (counts + tm - 1) // tm
    tend = jnp.cumsum(tiles)
    tstart = tend - tiles
    slot = tstart[experts] * tm + rank
    token = jnp.arange(2 * t, dtype=I32) // 2
    src = jnp.zeros((n_tiles * tm,), I32).at[slot].set(token)
    tile_id = jnp.arange(n_tiles, dtype=I32)
    te = jnp.sum((tile_id[:, None] >= tend[None, :]).astype(I32), axis=1)
    active = tile_id < tend[-1]
    last_e = jnp.sum((tend[-1] - 1 >= tend).astype(I32))
    te = jnp.where(active, te, last_e)
    tv = jnp.where(active, jnp.clip(counts[te] - (tile_id - tstart[te]) * tm, 0, tm), 0)
    slots = slot.reshape(t, 2)
    n_active = (tend[-1] * tm).astype(I32).reshape(1)
    return src, n_active, te.astype(I32), tv.astype(I32), slots[:, 0], slots[:, 1]


def _bias_table(rel_bias):
    h = rel_bias.shape[1]
    return jnp.full((h, LANES), NEG, F32).at[:, :N_BUCKETS].set(rel_bias.T * LOG2E)


def _to_classes(a, batch, seq, r):
    if r == 1:
        return a
    w = a.shape[-1]
    return a.reshape(batch, seq // r, r, w).transpose(0, 2, 1, 3).reshape(batch * seq, w)


def _from_classes(a, batch, seq, r):
    if r == 1:
        return a
    w = a.shape[-1]
    return a.reshape(batch, r, seq // r, w).transpose(0, 2, 1, 3).reshape(batch * seq, w)


def kernel(x, c, positions, rel_bias, w_mod, b_mod, g_attn, g_ffn, a_w_in, a_w_out, a_g_qn, a_g_kn,
           kv_w_mod, kv_b_mod, kv_g, kv_w, b_g_kn, b_w_q, b_w_out, b_g_qn, ffn_w1, ffn_w3, ffn_w2,
           moe_router, moe_router_b, moe_w1, moe_w3, moe_w2):
    batch, seq, d = x.shape
    m = batch * seq
    x2 = x.reshape(m, d)
    positions = positions.astype(I32)

    c8 = jnp.zeros((8, d), F32).at[:batch].set(c)
    mod0 = _mod_call(c8, w_mod, 0, b_mod)[:batch]
    mod1 = _mod_call(c8, w_mod, 1, b_mod)[:batch]
    kvm = _mod_call(c8, kv_w_mod[None], 0, kv_b_mod[None])[:batch]
    sh1_0, sc1_0, gt1_0, sh2_0, sc2_0, gt2_0 = jnp.split(mod0, 6, axis=-1)
    sh1_1, sc1_1, gt1_1, sh2_1, sc2_1, gt2_1 = jnp.split(mod1, 6, axis=-1)
    kv_sh, kv_sc = jnp.split(kvm, 2, axis=-1)

    tab = _bias_table(rel_bias)

    a_main = A_HEADS * A_HEAD_DIM + 2 * A_KV_HEADS * A_HEAD_DIM + IDX_HEADS * IDX_DIM
    n_tail = IDX_DIM + IDX_HEADS
    w_tail = jnp.zeros((d, LANES), F32).at[:, :n_tail].set(a_w_in[0, :, a_main:a_main + n_tail])
    qkv, tail = _nm_matmul(x2, g_attn[0], sc1_0, sh1_0, a_w_in, 0, a_main, seq, w_tail=w_tail)
    attn = _dsa_attention(qkv, tail, positions, tab,
                          a_g_qn[0].reshape(1, -1), a_g_kn[0].reshape(1, -1), batch, seq)
    x2 = _matmul_residual(attn, a_w_out, 0, x2, gt1_0, seq, tn=512)
    u = _nm_swiglu(x2, g_ffn[0], sc2_0, sh2_0, ffn_w1, ffn_w3, 0, seq)
    x2 = _matmul_residual(u, ffn_w2, 0, x2, gt2_0, seq)

    b_q = len(B_DILATIONS) * B_HEADS * B_HEAD_DIM
    kvall = _nm_matmul(x2, kv_g, kv_sc, kv_sh, kv_w[None], 0, 2 * b_q, seq)
    qall = _nm_matmul(x2, g_attn[1], sc1_1, sh1_1, b_w_q, 0, b_q, seq)
    gq2 = jnp.tile(b_g_qn[0], 2).reshape(1, LANES)
    gk2 = jnp.tile(b_g_kn, 2).reshape(1, LANES)
    w = B_HEADS * B_HEAD_DIM
    ng = len(B_DILATIONS)
    outs, lses = [], []
    for g, (window, r) in enumerate(B_DILATIONS):
        if r == 1:
            q_arr, q_col, k_arr, k_col, v_arr, v_col = qall, g, kvall, g, kvall, ng + g
            pos_g = positions
        else:
            q_arr = _to_classes(qall[:, g * w:(g + 1) * w], batch, seq, r)
            k_arr = _to_classes(kvall[:, g * w:(g + 1) * w], batch, seq, r)
            v_arr = _to_classes(kvall[:, (ng + g) * w:(ng + g + 1) * w], batch, seq, r)
            q_col = k_col = v_col = 0
            pos_g = positions.reshape(batch, seq // r, r).transpose(0, 2, 1).reshape(batch, seq)
        o_g, lse_g = _win_attention(q_arr, q_col, k_arr, k_col, v_arr, v_col, pos_g, tab, gq2, gk2,
                                    batch, seq, r, window)
        outs.append(_from_classes(o_g, batch, seq, r))
        lses.append(_from_classes(lse_g, batch, seq, r))
    x2 = _merge_out(outs, lses, b_w_out, 0, x2, gt1_1, seq)

    h, route = _router(x2, g_ffn[1], sc2_1, sh2_1, moe_router[0], moe_router_b[0], seq)
    share = (2 * m) // N_EXPERTS
    tm = -(-(share * 17 // 32) // 64) * 64
    n_tiles = (2 * m) // tm + N_EXPERTS
    src, n_active, te, tv, p1, p2 = _routing_tables(route, tm, n_tiles)
    hs = _gather_rows(h, src, n_active, tm // 4)
    ys = _moe_experts(hs, te, tv, moe_w1.reshape(moe_w1.shape[1:]), moe_w3.reshape(moe_w3.shape[1:]),
                      moe_w2.reshape(moe_w2.shape[1:]), tm)
    out = _combine(ys, p1, p2, x2, gt2_1, route, seq)
    return out.reshape(batch, seq, d)
```

```python
import functools
import math

import jax
import jax.numpy as jnp
from jax import lax
from jax.experimental import pallas as pl
from jax.experimental.pallas import tpu as pltpu

F32 = jnp.float32
BF16 = jnp.bfloat16
I32 = jnp.int32

EPS = 1e-6
NEG = -1e30
INT_MIN = -(2 ** 31)
LOG2E = 1.0 / math.log(2.0)

A_HEADS, A_KV_HEADS, A_HEAD_DIM = 16, 4, 128
IDX_HEADS, IDX_DIM = 16, 64
TOPK_MAX = 256
B_DILATIONS = ((128, 1), (512, 4), (2048, 16))
B_HEADS, B_HEAD_DIM = 16, 64
N_BUCKETS, MAX_DISTANCE = 32, 2048
N_EXPERTS = 8
LANES = 128

VMEM_LIMIT_BYTES = 56 * 1024 * 1024

_NT = (((1,), (1,)), ((), ()))


def _params(*sem):
    return pltpu.CompilerParams(dimension_semantics=sem, vmem_limit_bytes=VMEM_LIMIT_BYTES)


def _dot(a, b):
    return jnp.dot(a, b, preferred_element_type=F32)


def _dot_nt(a, b):
    return lax.dot_general(a, b, _NT, preferred_element_type=F32)


def _sigmoid(x):
    return 1.0 / (1.0 + jnp.exp(-x))


def _t5_bucket(rel):
    n = jnp.maximum(rel, 0)
    max_exact = N_BUCKETS // 2
    nf = jnp.maximum(n, 1).astype(F32)
    large = max_exact + (jnp.log(nf / max_exact) / math.log(MAX_DISTANCE / max_exact)
                         * (N_BUCKETS - max_exact)).astype(I32)
    large = jnp.minimum(large, N_BUCKETS - 1)
    return jnp.where(n < max_exact, n, large)


def _norm_mod(x, g, sc, sh):
    y = x * lax.rsqrt(jnp.mean(x * x, axis=-1, keepdims=True) + EPS)
    return (y * g) * (1.0 + sc) + sh


def _bias_lookup(tab_ref, heads, bkt):
    rows, cols = bkt.shape
    tabs = [jnp.broadcast_to(tab_ref[h:h + 1, :], (8, LANES)) for h in heads]
    tiles = [[] for _ in tabs]
    for r in range(rows // 8):
        pieces = [[] for _ in tabs]
        for c in range(cols // LANES):
            idx = bkt[r * 8:(r + 1) * 8, c * LANES:(c + 1) * LANES]
            for k, tab in enumerate(tabs):
                pieces[k].append(jnp.take_along_axis(tab, idx, axis=1))
        for k in range(len(tabs)):
            tiles[k].append(pieces[k][0] if len(pieces[k]) == 1 else jnp.concatenate(pieces[k], axis=1))
    return [jnp.concatenate(t, axis=0) for t in tiles]


def _mod_body(c_ref, w_ref, b_ref, o_ref):
    c = c_ref[...]
    cs = c * _sigmoid(c)
    o_ref[...] = _dot(cs.astype(BF16), w_ref[...].astype(BF16)) + b_ref[...]


def _mod_call(c8, w3, layer, b2):
    _, d, n = w3.shape
    tn = 1024
    return pl.pallas_call(
        _mod_body,
        grid=(n // tn,),
        in_specs=[pl.BlockSpec((8, d), lambda j: (0, 0)),
                  pl.BlockSpec((None, d, tn), lambda j: (layer, 0, j)),
                  pl.BlockSpec((None, 1, tn), lambda j: (layer, 0, j))],
        out_specs=pl.BlockSpec((8, tn), lambda j: (0, j)),
        out_shape=jax.ShapeDtypeStruct((8, n), F32),
        compiler_params=_params("arbitrary"),
        name="adaln_mod",
    )(c8, w3, b2.reshape(b2.shape[0], 1, n))


def _nm_body(x_ref, g_ref, sc_ref, sh_ref, w_ref, *rest, has_tail, dilations, group_cols):
    if has_tail:
        wt_ref, o_ref, ot_ref, h_scr = rest
    elif dilations:
        o_ref, h_scr, y_scr = rest
    else:
        o_ref, h_scr = rest

    @pl.when(pl.program_id(1) == 0)
    def _():
        h = _norm_mod(x_ref[...], g_ref[...], sc_ref[0], sh_ref[0]).astype(BF16)
        h_scr[...] = h
        if has_tail:
            ot_ref[...] = _dot(h, wt_ref[...].astype(BF16)).astype(ot_ref.dtype)

    y = _dot(h_scr[...], w_ref[...].astype(BF16))
    if not dilations:
        o_ref[...] = y.astype(o_ref.dtype)
        return
    tm, tn = o_ref.shape
    group = (pl.program_id(1) * tn // group_cols) % len(dilations)
    for k, r in enumerate(dilations):
        @pl.when(group == k)
        def _(r=r):
            if r == 1:
                o_ref[...] = y.astype(o_ref.dtype)
                return
            for c in range(tn // LANES):
                y_scr[c] = y[:, c * LANES:(c + 1) * LANES]
            cls = tm // r
            for rho in range(r):
                for c in range(tn // LANES):
                    o_ref[rho * cls:(rho + 1) * cls, c * LANES:(c + 1) * LANES] = (
                        y_scr[c, pl.ds(rho, cls, stride=r), :].astype(o_ref.dtype))


def _nm_matmul(x, g, sc, sh, w3, layer, n_cols, seq, w_tail=None, dilations=None, group_cols=None,
               tm=2048, tn=512):
    m, d = x.shape
    tm = min(tm, seq)
    nb = seq // tm
    has_tail = w_tail is not None
    assert not dilations or (tm == seq and not has_tail and group_cols % tn == 0)
    in_specs = [pl.BlockSpec((tm, d), lambda i, j: (i, 0), pipeline_mode=pl.Buffered(1)),
                pl.BlockSpec((1, d), lambda i, j: (0, 0)),
                pl.BlockSpec((1, 1, d), lambda i, j: (i // nb, 0, 0)),
                pl.BlockSpec((1, 1, d), lambda i, j: (i // nb, 0, 0)),
                pl.BlockSpec((None, d, tn), lambda i, j: (layer, 0, j))]
    args = [x, g.reshape(1, d), sc.reshape(-1, 1, d), sh.reshape(-1, 1, d), w3]
    out_specs = [pl.BlockSpec((tm, tn), lambda i, j: (i, j))]
    out_shape = [jax.ShapeDtypeStruct((m, n_cols), BF16)]
    if has_tail:
        in_specs.append(pl.BlockSpec((d, LANES), lambda i, j: (0, 0)))
        args.append(w_tail)
        out_specs.append(pl.BlockSpec((tm, LANES), lambda i, j: (i, 0)))
        out_shape.append(jax.ShapeDtypeStruct((m, LANES), BF16))
    scratch = [pltpu.VMEM((tm, d), BF16)]
    if dilations:
        scratch.append(pltpu.VMEM((tn // LANES, tm, LANES), F32))
    res = pl.pallas_call(
        functools.partial(_nm_body, has_tail=has_tail, dilations=dilations, group_cols=group_cols),
        grid=(m // tm, n_cols // tn),
        in_specs=in_specs,
        out_specs=out_specs,
        out_shape=out_shape,
        scratch_shapes=scratch,
        compiler_params=_params("arbitrary", "arbitrary"),
        name="norm_mod_matmul",
    )(*args)
    return res if has_tail else res[0]


def _mmres_body(a_ref, w_ref, x_ref, gt_ref, o_ref):
    o_ref[...] = x_ref[...] + gt_ref[0] * _dot(a_ref[...], w_ref[...].astype(BF16))


def _matmul_residual(a, w3, layer, x, gt, seq, tm=2048, tn=256):
    m, k = a.shape
    d = x.shape[1]
    tm = min(tm, seq)
    nb = seq // tm
    return pl.pallas_call(
        _mmres_body,
        grid=(m // tm, d // tn),
        in_specs=[pl.BlockSpec((tm, k), lambda i, j: (i, 0), pipeline_mode=pl.Buffered(1)),
                  pl.BlockSpec((None, k, tn), lambda i, j: (layer, 0, j)),
                  pl.BlockSpec((tm, tn), lambda i, j: (i, j)),
                  pl.BlockSpec((1, 1, tn), lambda i, j: (i // nb, 0, j))],
        out_specs=pl.BlockSpec((tm, tn), lambda i, j: (i, j)),
        out_shape=jax.ShapeDtypeStruct((m, d), F32),
        compiler_params=_params("arbitrary", "arbitrary"),
        name="matmul_residual",
    )(a, w3, x, gt.reshape(-1, 1, d))


def _nm_swiglu_body(x_ref, g_ref, sc_ref, sh_ref, w1_ref, w3_ref, o_ref, h_scr):
    @pl.when(pl.program_id(1) == 0)
    def _():
        h_scr[...] = _norm_mod(x_ref[...], g_ref[...], sc_ref[0], sh_ref[0]).astype(BF16)

    h = h_scr[...]
    a = _dot(h, w1_ref[...].astype(BF16))
    b = _dot(h, w3_ref[...].astype(BF16))
    o_ref[...] = (a * _sigmoid(a) * b).astype(o_ref.dtype)


def _nm_swiglu(x, g, sc, sh, w1, w3, layer, seq, tm=2048, tf=256):
    m, d = x.shape
    f = w1.shape[2]
    tm = min(tm, seq)
    nb = seq // tm
    return pl.pallas_call(
        _nm_swiglu_body,
        grid=(m // tm, f // tf),
        in_specs=[pl.BlockSpec((tm, d), lambda i, j: (i, 0), pipeline_mode=pl.Buffered(1)),
                  pl.BlockSpec((1, d), lambda i, j: (0, 0)),
                  pl.BlockSpec((1, 1, d), lambda i, j: (i // nb, 0, 0)),
                  pl.BlockSpec((1, 1, d), lambda i, j: (i // nb, 0, 0)),
                  pl.BlockSpec((None, d, tf), lambda i, j: (layer, 0, j)),
                  pl.BlockSpec((None, d, tf), lambda i, j: (layer, 0, j))],
        out_specs=pl.BlockSpec((tm, tf), lambda i, j: (i, j)),
        out_shape=jax.ShapeDtypeStruct((m, f), BF16),
        scratch_shapes=[pltpu.VMEM((tm, d), BF16)],
        compiler_params=_params("arbitrary", "arbitrary"),
        name="norm_mod_swiglu_up",
    )(x, g.reshape(1, d), sc.reshape(-1, 1, d), sh.reshape(-1, 1, d), w1, w3)


def _dsa_body(q_ref, qi_ref, k_ref, v_ref, tq_ref, tk_ref, pr_ref, pc_ref, tab_ref, gq_ref, gk_ref,
              o_ref,
              kn_scr, vt_scr, kke_scr, kko_scr, qst_scr, wib_scr, key_scr, qn_scr, m_scr, l_scr, acc_scr,
              s_buf, cm_buf,
              *, topk, tq, seq):
    i = pl.program_id(1)
    nc = i + 1
    grp = A_HEADS // A_KV_HEADS
    hd = A_HEAD_DIM
    lane = lax.broadcasted_iota(I32, (1, LANES), 1)
    krow = lax.broadcasted_iota(I32, (tq, tq), 0)
    qcol = lax.broadcasted_iota(I32, (tq, tq), 1)

    @pl.when(i == 0)
    def _prepare_keys():
        gk = gk_ref[...]

        def body(r, carry):
            rows = pl.ds(pl.multiple_of(r * tq, tq), tq)
            for kh in range(A_KV_HEADS):
                cols = slice(kh * hd, (kh + 1) * hd)
                kb = k_ref[rows, cols].astype(F32)
                ms = jnp.mean(kb * kb, axis=-1, keepdims=True)
                kn_scr[rows, cols] = ((kb * lax.rsqrt(ms + EPS)) * gk).astype(BF16)
                vt_scr[r, cols, :] = v_ref[rows, cols].astype(F32).T.astype(BF16)
            t = tk_ref[rows, :].astype(F32)
            kke_scr[rows, :] = jnp.where(lane < IDX_DIM, t, 0.0).astype(BF16)
            kko_scr[rows, :] = jnp.where(lane >= IDX_DIM, pltpu.roll(t, IDX_DIM, 1), 0.0).astype(BF16)
            return carry

        lax.fori_loop(0, seq // tq, body, 0)

    for j in range(IDX_HEADS // 2):
        qst_scr[j * tq:(j + 1) * tq, :] = qi_ref[:, j * LANES:(j + 1) * LANES]
    w_scale = (IDX_DIM ** -0.5) * (IDX_HEADS ** -0.5)
    wib_scr[...] = tq_ref[...].astype(F32).T[IDX_DIM:IDX_DIM + IDX_HEADS, :] * w_scale

    def idx_body(c, carry):
        rows = pl.ds(pl.multiple_of(c * tq, tq), tq)
        qst = qst_scr[...]
        re = _dot_nt(kke_scr[rows, :], qst)
        ro = _dot_nt(kko_scr[rows, :], qst)
        acc = jnp.zeros((tq, tq), F32)
        for j in range(IDX_HEADS // 2):
            acc = acc + jnp.maximum(re[:, j * tq:(j + 1) * tq], 0.0) * wib_scr[2 * j:2 * j + 1, :]
            acc = acc + jnp.maximum(ro[:, j * tq:(j + 1) * tq], 0.0) * wib_scr[2 * j + 1:2 * j + 2, :]
        bits = pltpu.bitcast(acc, I32)
        key = bits ^ ((bits >> 31) & 0x7FFFFFFF)
        causal = (c < i) | (krow <= qcol)
        key_scr[c] = jnp.where(causal, key, INT_MIN)
        return carry

    lax.fori_loop(0, nc, idx_body, 0)

    def bit_body(bi, t_u):
        cand_u = t_u | lax.shift_left(jnp.int32(1), 31 - bi)
        cand_s = cand_u ^ INT_MIN

        def cnt_body(c, cnt):
            ge = jnp.where(key_scr[c] >= cand_s, 1, 0)
            return cnt + jnp.sum(ge.reshape(tq // 8, 8, tq), axis=0)

        cnt = lax.fori_loop(0, nc, cnt_body, jnp.zeros((8, tq), I32))
        total = jnp.sum(cnt, axis=0, keepdims=True)
        return jnp.where(total >= topk, cand_u, t_u)

    nbits = jnp.where(nc * tq > topk, 32, 0)
    t_u = lax.fori_loop(0, nbits, bit_body, jnp.zeros((1, tq), I32))
    thr = t_u ^ INT_MIN

    qpos = pr_ref[0, i]

    def bkt_body(c, carry):
        rows = pl.ds(pl.multiple_of(c * tq, tq), tq)
        bkt = _t5_bucket(qpos - pc_ref[rows, :])
        causal = (c < i) | (krow <= qcol)
        sel = (key_scr[c] >= thr) & causal
        key_scr[c] = jnp.where(sel, bkt, N_BUCKETS)
        return carry

    lax.fori_loop(0, nc, bkt_body, 0)

    gq = gq_ref[...]
    scale = (hd ** -0.5) * LOG2E
    for g in range(A_KV_HEADS):
        for hh in range(grp):
            h = g * grp + hh
            qh = q_ref[:, h * hd:(h + 1) * hd].astype(F32)
            ms = jnp.mean(qh * qh, axis=-1, keepdims=True)
            qn_scr[hh * tq:(hh + 1) * tq, :] = (((qh * lax.rsqrt(ms + EPS)) * gq) * scale).astype(BF16)
        m_scr[...] = jnp.full(m_scr.shape, NEG, F32)
        l_scr[...] = jnp.zeros(l_scr.shape, F32)
        acc_scr[...] = jnp.zeros(acc_scr.shape, F32)

        def scores(c, g=g):
            rows = pl.ds(pl.multiple_of(c * tq, tq), tq)
            s = _dot_nt(kn_scr[rows, g * hd:(g + 1) * hd], qn_scr[...])
            bias = _bias_lookup(tab_ref, range(g * grp, (g + 1) * grp), key_scr[c])
            s = jnp.concatenate([s[:, hh * tq:(hh + 1) * tq] + bias[hh] for hh in range(grp)], axis=1)
            s_buf[c % 2] = s
            cm_buf[c % 2] = jnp.max(s, axis=0, keepdims=True)

        def accumulate(c, g=g):
            s = s_buf[c % 2]
            m_old = m_scr[...]
            m_new = jnp.maximum(m_old, cm_buf[c % 2])
            p = jnp.exp2(s - m_new)
            alpha = jnp.exp2(m_old - m_new)
            l_scr[...] = alpha * l_scr[...] + jnp.sum(p, axis=0, keepdims=True)
            acc_scr[...] = alpha * acc_scr[...] + _dot(vt_scr[c, g * hd:(g + 1) * hd, :], p.astype(BF16))
            m_scr[...] = m_new

        def att_body(c, carry):
            accumulate(c)
            scores(c + 1)
            return carry

        scores(0)
        lax.fori_loop(0, nc - 1, att_body, 0)
        accumulate(nc - 1)
        o = acc_scr[...] * (1.0 / l_scr[...])
        for hh in range(grp):
            h = g * grp + hh
            o_ref[:, h * hd:(h + 1) * hd] = o[:, hh * tq:(hh + 1) * tq].T.astype(o_ref.dtype)


def _dsa_attention(qkv, tail, pos, tab, gq, gk, batch, seq, tq=256):
    tq = min(tq, seq)
    nq = seq // tq
    a_q = A_HEADS * A_HEAD_DIM
    a_kv = A_KV_HEADS * A_HEAD_DIM
    a_qi = IDX_HEADS * IDX_DIM
    topk = min(TOPK_MAX, seq // 4)
    grp = A_HEADS // A_KV_HEADS
    body = functools.partial(_dsa_body, topk=topk, tq=tq, seq=seq)
    return pl.pallas_call(
        body,
        grid=(batch, nq),
        in_specs=[
            pl.BlockSpec((tq, a_q), lambda b, i: (b * nq + i, 0)),
            pl.BlockSpec((tq, a_qi), lambda b, i: (b * nq + i, (a_q + 2 * a_kv) // a_qi)),
            pl.BlockSpec((seq, a_kv), lambda b, i: (b, a_q // a_kv)),
            pl.BlockSpec((seq, a_kv), lambda b, i: (b, a_q // a_kv + 1)),
            pl.BlockSpec((tq, LANES), lambda b, i: (b * nq + i, 0)),
            pl.BlockSpec((seq, LANES), lambda b, i: (b, 0)),
            pl.BlockSpec((1, nq, 1, tq), lambda b, i: (b, 0, 0, 0)),
            pl.BlockSpec((seq, 1), lambda b, i: (b, 0)),
            pl.BlockSpec((A_HEADS, LANES), lambda b, i: (0, 0)),
            pl.BlockSpec((1, A_HEAD_DIM), lambda b, i: (0, 0)),
            pl.BlockSpec((1, A_HEAD_DIM), lambda b, i: (0, 0)),
        ],
        out_specs=pl.BlockSpec((tq, a_q), lambda b, i: (b * nq + i, 0)),
        out_shape=jax.ShapeDtypeStruct((batch * seq, a_q), BF16),
        scratch_shapes=[
            pltpu.VMEM((seq, a_kv), BF16),
            pltpu.VMEM((nq, a_kv, tq), BF16),
            pltpu.VMEM((seq, LANES), BF16),
            pltpu.VMEM((seq, LANES), BF16),
            pltpu.VMEM((IDX_HEADS // 2 * tq, LANES), BF16),
            pltpu.VMEM((IDX_HEADS, tq), F32),
            pltpu.VMEM((nq, tq, tq), I32),
            pltpu.VMEM((grp * tq, A_HEAD_DIM), BF16),
            pltpu.VMEM((1, grp * tq), F32),
            pltpu.VMEM((1, grp * tq), F32),
            pltpu.VMEM((A_HEAD_DIM, grp * tq), F32),
            pltpu.VMEM((2, tq, grp * tq), F32),
            pltpu.VMEM((2, 1, grp * tq), F32),
        ],
        compiler_params=_params("arbitrary", "arbitrary"),
        name="dsa_attention",
    )(qkv, qkv, qkv, qkv, tail, tail, pos.reshape(batch, nq, 1, tq), pos.reshape(batch * seq, 1),
      tab, gq, gk)


def _win_body(*refs, cls_len, wk, use_prev, tq):
    if use_prev:
        (q_ref, kc_ref, vc_ref, kp_ref, vp_ref, pr_ref, pcc_ref, pcp_ref, tab_ref, gq_ref, gk_ref,
         o_ref, lse_ref, kn_scr, vt_scr, qn_scr, bkt_scr, bias_scr, lse_scr) = refs
    else:
        (q_ref, kc_ref, vc_ref, pr_ref, pcc_ref, tab_ref, gq_ref, gk_ref,
         o_ref, lse_ref, kn_scr, vt_scr, qn_scr, bkt_scr, bias_scr, lse_scr) = refs
    t = pl.program_id(1)
    hd = B_HEAD_DIM
    pw = 2 * hd
    blk = LANES
    shift = cls_len.bit_length() - 1
    koff = blk if use_prev else 0
    win = 2 * blk if cls_len > blk else blk
    lo = lax.broadcasted_iota(I32, (1, pw), 1) < hd
    lo_rows = lax.broadcasted_iota(I32, (pw, 1), 0) < hd
    gq = gq_ref[...]
    gk = gk_ref[...]
    scale = (hd ** -0.5) * LOG2E

    def pair_norm(x, g):
        sq = x * x
        ms_lo = jnp.sum(jnp.where(lo, sq, 0.0), axis=-1, keepdims=True) * (1.0 / hd)
        ms_hi = jnp.sum(jnp.where(lo, 0.0, sq), axis=-1, keepdims=True) * (1.0 / hd)
        inv = jnp.where(lo, lax.rsqrt(ms_lo + EPS), lax.rsqrt(ms_hi + EPS))
        return (x * inv) * g

    for p in range(B_HEADS // 2):
        cols = slice(p * pw, (p + 1) * pw)
        qn_scr[:, cols] = (pair_norm(q_ref[:, cols].astype(F32), gq) * scale).astype(BF16)
        if use_prev:
            kn_scr[0:blk, cols] = pair_norm(kp_ref[:, cols].astype(F32), gk).astype(BF16)
            vt_scr[p, :, 0:blk] = vp_ref[:, cols].astype(F32).T.astype(BF16)
        kn_scr[koff:koff + tq, cols] = pair_norm(kc_ref[:, cols].astype(F32), gk).astype(BF16)
        vt_scr[p, :, koff:koff + tq] = vc_ref[:, cols].astype(F32).T.astype(BF16)

    lse_scr[...] = jnp.zeros(lse_scr.shape, F32)
    qpos_all = pr_ref[0]
    for j in range(tq // blk):
        qrows = slice(j * blk, (j + 1) * blk)
        own = koff + j * blk
        k0 = own - blk if (win > blk and own >= blk) else own
        if use_prev and k0 < koff:
            kpos = jnp.concatenate([pcp_ref[k0:koff, :], pcc_ref[0:k0 + win - koff, :]], axis=0)
        else:
            kpos = pcc_ref[k0 - koff:k0 - koff + win, :]
        fq = t * tq + j * blk + lax.broadcasted_iota(I32, (1, blk), 1)
        fk = t * tq + (k0 - koff) + lax.broadcasted_iota(I32, (win, 1), 0)
        same = ((fq + cls_len) >> shift) == ((fk + cls_len) >> shift)
        dist = (fq & (cls_len - 1)) - (fk & (cls_len - 1))
        ok = same & (dist >= 0) & (dist <= wk)
        bkt_scr[...] = jnp.where(ok, _t5_bucket(qpos_all[:, qrows] - kpos), N_BUCKETS)

        def bias_rows(r8, carry):
            rows = pl.ds(pl.multiple_of(r8 * 8, 8), 8)
            idx = bkt_scr[rows, :]
            for h in range(B_HEADS):
                tab = jnp.broadcast_to(tab_ref[h:h + 1, :], (8, LANES))
                bias_scr[h // 2, rows, (h % 2) * blk:(h % 2 + 1) * blk] = jnp.take_along_axis(tab, idx, axis=1)
            return carry

        lax.fori_loop(0, win // 8, bias_rows, 0, unroll=8)

        for p in range(B_HEADS // 2):
            cols = slice(p * pw, (p + 1) * pw)
            qp = qn_scr[qrows, cols]
            zero = jnp.zeros_like(qp)
            heads = (jnp.where(lo, qp, zero), jnp.where(lo, zero, qp))
            kw = kn_scr[k0:k0 + win, cols]
            vtw = vt_scr[p, :, k0:k0 + win]
            hp = 2 * blk // win
            outs = []
            for c in range(2 // hp):
                q2 = heads[c] if hp == 1 else jnp.concatenate(heads, axis=0)
                s = _dot_nt(kw, q2) + bias_scr[p, :, c * hp * blk:(c + 1) * hp * blk]
                m = jnp.max(s, axis=0, keepdims=True)
                e = jnp.exp2(s - m)
                l = jnp.sum(e, axis=0, keepdims=True)
                ot = _dot(vtw, e.astype(BF16)) * (1.0 / l)
                lse = m + jnp.log(l) * LOG2E
                for k in range(hp):
                    h = 2 * p + c * hp + k
                    outs.append(ot[:, k * blk:(k + 1) * blk])
                    lse_scr[h:h + 1, :] = lse[:, k * blk:(k + 1) * blk]
            o_ref[qrows, cols] = jnp.where(lo_rows, outs[0], outs[1]).T.astype(o_ref.dtype)
        lse_ref[qrows, :] = lse_scr[...].T


def _win_attention(q_arr, q_col, k_arr, k_col, v_arr, v_col, pos, tab, gq2, gk2,
                   batch, seq, dilation, window, tq=512):
    cls_len = seq // dilation
    wk = window // dilation
    assert cls_len & (cls_len - 1) == 0 and wk <= LANES
    tq = min(tq, seq)
    use_prev = cls_len > tq
    assert use_prev or tq % cls_len == 0
    nt = seq // tq
    w = B_HEADS * B_HEAD_DIM
    sub = tq // LANES
    pos_col = pos.reshape(batch * seq, 1)
    pos_row = pos.reshape(batch * nt, 1, tq)
    prev = lambda b, t: jnp.maximum((b * nt + t) * sub - 1, 0)
    in_specs = [pl.BlockSpec((tq, w), lambda b, t: (b * nt + t, q_col)),
                pl.BlockSpec((tq, w), lambda b, t: (b * nt + t, k_col)),
                pl.BlockSpec((tq, w), lambda b, t: (b * nt + t, v_col))]
    args = [q_arr, k_arr, v_arr]
    if use_prev:
        in_specs += [pl.BlockSpec((LANES, w), lambda b, t: (prev(b, t), k_col)),
                     pl.BlockSpec((LANES, w), lambda b, t: (prev(b, t), v_col))]
        args += [k_arr, v_arr]
    in_specs += [pl.BlockSpec((1, 1, tq), lambda b, t: (b * nt + t, 0, 0)),
                 pl.BlockSpec((tq, 1), lambda b, t: (b * nt + t, 0))]
    args += [pos_row, pos_col]
    if use_prev:
        in_specs.append(pl.BlockSpec((LANES, 1), lambda b, t: (prev(b, t), 0)))
        args.append(pos_col)
    in_specs += [pl.BlockSpec((B_HEADS, LANES), lambda b, t: (0, 0)),
                 pl.BlockSpec((1, LANES), lambda b, t: (0, 0)),
                 pl.BlockSpec((1, LANES), lambda b, t: (0, 0))]
    args += [tab, gq2, gk2]
    nk = tq + (LANES if use_prev else 0)
    win = 2 * LANES if cls_len > LANES else LANES
    body = functools.partial(_win_body, cls_len=cls_len, wk=wk, use_prev=use_prev, tq=tq)
    return pl.pallas_call(
        body,
        grid=(batch, nt),
        in_specs=in_specs,
        out_specs=[pl.BlockSpec((tq, w), lambda b, t: (b * nt + t, 0)),
                   pl.BlockSpec((tq, LANES), lambda b, t: (b * nt + t, 0))],
        out_shape=[jax.ShapeDtypeStruct((batch * seq, w), BF16),
                   jax.ShapeDtypeStruct((batch * seq, LANES), F32)],
        scratch_shapes=[pltpu.VMEM((nk, w), BF16),
                        pltpu.VMEM((B_HEADS // 2, LANES, nk), BF16),
                        pltpu.VMEM((tq, w), BF16),
                        pltpu.VMEM((win, LANES), I32),
                        pltpu.VMEM((B_HEADS // 2, win, 2 * LANES), F32),
                        pltpu.VMEM((LANES, LANES), F32)],
        compiler_params=_params("arbitrary", "arbitrary"),
        name="dilated_attention",
    )(*args)


def _merge_body(*refs, dilations):
    ng = len(dilations)
    o_refs, l_refs = refs[:ng], refs[ng:2 * ng]
    e_ref, w_ref, x_ref, gt_ref, out_ref, a_scr, o_scr, l_scr = refs[2 * ng:]
    tm, k = a_scr.shape

    @pl.when(pl.program_id(1) == 0)
    def _():
        for g, r in enumerate(dilations):
            cls = tm // r
            for rho in range(r):
                l_scr[g, pl.ds(rho, cls, stride=r), :] = l_refs[g][0, rho]
                for c in range(k // LANES):
                    o_scr[g, c, pl.ds(rho, cls, stride=r), :] = (
                        o_refs[g][0, rho, :, c * LANES:(c + 1) * LANES].astype(F32))
        ls = [l_scr[g] for g in range(ng)]
        m = functools.reduce(jnp.maximum, ls)
        ws = [jnp.exp2(l - m) for l in ls]
        inv = 1.0 / functools.reduce(lambda a, b: a + b, ws)
        wides = []
        for wg in ws:
            wn = wg * inv
            hi = wn.astype(BF16)
            lo = (wn - hi.astype(F32)).astype(BF16)
            wides.append(_dot(hi, e_ref[...]) + _dot(lo, e_ref[...]))
        for c in range(k // LANES):
            cols = slice(c * LANES, (c + 1) * LANES)
            num = wides[0][:, cols] * o_scr[0, c]
            for g in range(1, ng):
                num = num + wides[g][:, cols] * o_scr[g, c]
            a_scr[:, cols] = num.astype(BF16)

    out_ref[...] = x_ref[...] + gt_ref[0] * _dot(a_scr[...], w_ref[...].astype(BF16))


def _merge_out(os_, ls_, dilations, w3, layer, x, gt, seq, tm=512, tn=512):
    m, k = os_[0].shape
    d = x.shape[1]
    nb = seq // tm
    batch = m // seq
    head = jnp.arange(k, dtype=I32)[None, :] // B_HEAD_DIM
    expand = (jnp.arange(LANES, dtype=I32)[:, None] == head).astype(BF16)
    ng = len(dilations)
    o_specs = [pl.BlockSpec((1, r, tm // r, k), lambda i, j: (i // nb, 0, i % nb, 0)) for r in dilations]
    l_specs = [pl.BlockSpec((1, r, tm // r, LANES), lambda i, j: (i // nb, 0, i % nb, 0)) for r in dilations]
    o_args = [o.reshape(batch, r, seq // r, k) for o, r in zip(os_, dilations)]
    l_args = [l.reshape(batch, r, seq // r, LANES) for l, r in zip(ls_, dilations)]
    return pl.pallas_call(
        functools.partial(_merge_body, dilations=tuple(dilations)),
        grid=(m // tm, d // tn),
        in_specs=o_specs + l_specs + [
            pl.BlockSpec((LANES, k), lambda i, j: (0, 0)),
            pl.BlockSpec((None, k, tn), lambda i, j: (layer, 0, j)),
            pl.BlockSpec((tm, tn), lambda i, j: (i, j)),
            pl.BlockSpec((1, 1, tn), lambda i, j: (i // nb, 0, j))],
        out_specs=pl.BlockSpec((tm, tn), lambda i, j: (i, j)),
        out_shape=jax.ShapeDtypeStruct((m, d), F32),
        scratch_shapes=[pltpu.VMEM((tm, k), BF16),
                        pltpu.VMEM((ng, k // LANES, tm, LANES), F32),
                        pltpu.VMEM((ng, tm, LANES), F32)],
        compiler_params=_params("arbitrary", "arbitrary"),
        name="merge_out_proj",
    )(*o_args, *l_args, expand, w3, x, gt.reshape(-1, 1, d))


def _router_body(x_ref, g_ref, sc_ref, sh_ref, rh_ref, rl_ref, rb_ref, h_ref, rt_ref):
    h = _norm_mod(x_ref[...], g_ref[...], sc_ref[0], sh_ref[0])
    h_ref[...] = h
    hh = h.astype(BF16)
    hl = (h - hh.astype(F32)).astype(BF16)
    logits = _dot(hh, rh_ref[...]) + _dot(hl, rh_ref[...]) + _dot(hh, rl_ref[...]) + rb_ref[...]
    lane = lax.broadcasted_iota(I32, logits.shape, 1)
    logits = jnp.where(lane < N_EXPERTS, logits, NEG)
    v1 = jnp.max(logits, axis=1, keepdims=True)
    i1 = jnp.min(jnp.where(logits == v1, lane, LANES), axis=1, keepdims=True)
    rest = jnp.where(lane == i1, NEG, logits)
    v2 = jnp.max(rest, axis=1, keepdims=True)
    i2 = jnp.min(jnp.where(rest == v2, lane, LANES), axis=1, keepdims=True)
    e = jnp.exp(v2 - v1)
    g1 = 1.0 / (1.0 + e)
    g2 = e * g1
    rt_ref[...] = jnp.where(lane == 0, i1.astype(F32),
                            jnp.where(lane == 1, i2.astype(F32),
                                      jnp.where(lane == 2, g1, jnp.where(lane == 3, g2, 0.0))))


def _router(x, g, sc, sh, rw, rb, seq, tm=512):
    m, d = x.shape
    nb = seq // tm
    ne = rw.shape[1]
    rw_p = jnp.zeros((d, LANES), F32).at[:, :ne].set(rw)
    rh = rw_p.astype(BF16)
    rl = (rw_p - rh.astype(F32)).astype(BF16)
    rb_p = jnp.zeros((1, LANES), F32).at[0, :ne].set(rb)
    return pl.pallas_call(
        _router_body,
        grid=(m // tm,),
        in_specs=[pl.BlockSpec((tm, d), lambda i: (i, 0)),
                  pl.BlockSpec((1, d), lambda i: (0, 0)),
                  pl.BlockSpec((1, 1, d), lambda i: (i // nb, 0, 0)),
                  pl.BlockSpec((1, 1, d), lambda i: (i // nb, 0, 0)),
                  pl.BlockSpec((d, LANES), lambda i: (0, 0)),
                  pl.BlockSpec((d, LANES), lambda i: (0, 0)),
                  pl.BlockSpec((1, LANES), lambda i: (0, 0))],
        out_specs=[pl.BlockSpec((tm, d), lambda i: (i, 0)),
                   pl.BlockSpec((tm, LANES), lambda i: (i, 0))],
        out_shape=[jax.ShapeDtypeStruct((m, d), F32),
                   jax.ShapeDtypeStruct((m, LANES), F32)],
        compiler_params=_params("arbitrary"),
        name="router_top2",
    )(x, g.reshape(1, d), sc.reshape(-1, 1, d), sh.reshape(-1, 1, d), rh, rl, rb_p)


def _row_copy(src_hbm, idx, buf, r, sem):
    return pltpu.make_async_copy(src_hbm.at[pl.ds(idx, 1), :], buf.at[pl.ds(r, 1), :], sem)


def _gather_body(src_ref, nact_ref, h_hbm, o_ref, buf, sem, *, rows):
    i = pl.program_id(0)
    n_active = nact_ref[0]

    def issue(step):
        slot = step % 2

        def body(r, carry):
            _row_copy(h_hbm, src_ref[step * rows + r], buf.at[slot], r, sem.at[slot]).start()
            return carry

        lax.fori_loop(0, rows, body, 0, unroll=8)

    @pl.when(i == 0)
    def _():
        issue(i)

    @pl.when((i + 1) * rows < n_active)
    def _():
        issue(i + 1)

    slot = i % 2

    @pl.when(i * rows < n_active)
    def _():
        pltpu.make_async_copy(h_hbm.at[pl.ds(0, rows), :], buf.at[slot], sem.at[slot]).wait()
        o_ref[...] = buf[slot].astype(o_ref.dtype)

    @pl.when(i * rows >= n_active)
    def _():
        o_ref[...] = jnp.zeros(o_ref.shape, o_ref.dtype)


def _gather_rows(h, src, n_active, rows):
    r_total = src.shape[0]
    d = h.shape[1]
    return pl.pallas_call(
        functools.partial(_gather_body, rows=rows),
        grid_spec=pltpu.PrefetchScalarGridSpec(
            num_scalar_prefetch=2,
            grid=(r_total // rows,),
            in_specs=[pl.BlockSpec(memory_space=pl.ANY)],
            out_specs=pl.BlockSpec((rows, d), lambda i, s, n: (i, 0)),
            scratch_shapes=[pltpu.VMEM((2, rows, d), F32), pltpu.SemaphoreType.DMA((2,))],
        ),
        out_shape=jax.ShapeDtypeStruct((r_total, d), BF16),
        compiler_params=_params("arbitrary"),
        name="moe_dispatch_gather",
    )(src, n_active, h)


def _moe_body(te_ref, tv_ref, hs_ref, w1_ref, w3_ref, w2_ref, o_ref, *, ncol):
    t = pl.program_id(0)
    f = pl.program_id(1)
    valid = tv_ref[t]
    d = o_ref.shape[1]
    cw = d // ncol

    @pl.when(valid > 0)
    def _():
        h = hs_ref[...]
        a = _dot(h, w1_ref[...].astype(BF16))
        b = _dot(h, w3_ref[...].astype(BF16))
        u = (a * _sigmoid(a) * b).astype(BF16)
        w2 = w2_ref[...].astype(BF16)

        @pl.when(f == 0)
        def _():
            for n in range(ncol):
                o_ref[:, n * cw:(n + 1) * cw] = _dot(u, w2[:, n * cw:(n + 1) * cw])

        @pl.when(f > 0)
        def _():
            for n in range(ncol):
                o_ref[:, n * cw:(n + 1) * cw] += _dot(u, w2[:, n * cw:(n + 1) * cw])

    @pl.when((valid == 0) & (f == 0))
    def _():
        o_ref[...] = jnp.zeros(o_ref.shape, o_ref.dtype)


def _moe_experts(hs, tile_expert, tile_valid, w1, w3, w2, tm, tf=256, ncol=4):
    r_total, d = hs.shape
    n_tiles = r_total // tm
    nf = w1.shape[2] // tf

    def w13_map(t, f, te, tv):
        return (te[t], 0, jnp.where(tv[t] > 0, f, nf - 1))

    def w2_map(t, f, te, tv):
        return (te[t], jnp.where(tv[t] > 0, f, nf - 1), 0)

    return pl.pallas_call(
        functools.partial(_moe_body, ncol=ncol),
        grid_spec=pltpu.PrefetchScalarGridSpec(
            num_scalar_prefetch=2,
            grid=(n_tiles, nf),
            in_specs=[pl.BlockSpec((tm, d), lambda t, f, te, tv: (t, 0)),
                      pl.BlockSpec((None, d, tf), w13_map),
                      pl.BlockSpec((None, d, tf), w13_map),
                      pl.BlockSpec((None, tf, d), w2_map)],
            out_specs=pl.BlockSpec((tm, d), lambda t, f, te, tv: (t, 0)),
        ),
        out_shape=jax.ShapeDtypeStruct((r_total, d), F32),
        compiler_params=_params("arbitrary", "arbitrary"),
        name="moe_experts",
    )(tile_expert, tile_valid, hs, w1, w3, w2)


def _combine_body(p1_ref, p2_ref, y_hbm, x_ref, gt_ref, rt_ref, o_ref, buf_a, buf_b, sem, *, rows):
    base = pl.program_id(0) * rows

    def issue(r, carry):
        _row_copy(y_hbm, p1_ref[base + r], buf_a, r, sem).start()
        _row_copy(y_hbm, p2_ref[base + r], buf_b, r, sem).start()
        return carry

    lax.fori_loop(0, rows, issue, 0, unroll=4)
    pltpu.make_async_copy(y_hbm.at[pl.ds(0, rows), :], buf_a, sem).wait()
    pltpu.make_async_copy(y_hbm.at[pl.ds(0, rows), :], buf_b, sem).wait()
    rt = rt_ref[...]
    g1 = rt[:, 2:3]
    g2 = rt[:, 3:4]
    o_ref[...] = x_ref[...] + gt_ref[0] * (g1 * buf_a[...] + g2 * buf_b[...])


def _combine(ys, p1, p2, x, gt, route, seq, rows=256):
    m, d = x.shape
    nb = seq // rows
    return pl.pallas_call(
        functools.partial(_combine_body, rows=rows),
        grid_spec=pltpu.PrefetchScalarGridSpec(
            num_scalar_prefetch=2,
            grid=(m // rows,),
            in_specs=[pl.BlockSpec(memory_space=pl.ANY),
                      pl.BlockSpec((rows, d), lambda i, a, b: (i, 0)),
                      pl.BlockSpec((1, 1, d), lambda i, a, b: (i // nb, 0, 0)),
                      pl.BlockSpec((rows, LANES), lambda i, a, b: (i, 0))],
            out_specs=pl.BlockSpec((rows, d), lambda i, a, b: (i, 0)),
            scratch_shapes=[pltpu.VMEM((rows, d), F32), pltpu.VMEM((rows, d), F32),
                            pltpu.SemaphoreType.DMA(())],
        ),
        out_shape=jax.ShapeDtypeStruct((m, d), F32),
        compiler_params=_params("arbitrary"),
        name="moe_combine",
    )(p1, p2, ys, x, gt.reshape(-1, 1, d), route)


def _routing_tables(route, tm, n_tiles):
    t = route.shape[0]
    experts = route[:, :2].astype(I32).reshape(-1)
    onehot = (experts[:, None] == jnp.arange(N_EXPERTS, dtype=I32)[None, :]).astype(I32)
    csum = jnp.cumsum(onehot, axis=0)
    rank = jnp.sum(csum * onehot, axis=1) - 1
    counts = csum[-1]
    tiles = (counts + tm - 1) // tm
    tend = jnp.cumsum(tiles)
    tstart = tend - tiles
    slot = tstart[experts] * tm + rank
    token = jnp.arange(2 * t, dtype=I32) // 2
    src = jnp.zeros((n_tiles * tm,), I32).at[slot].set(token)
    tile_id = jnp.arange(n_tiles, dtype=I32)
    te = jnp.sum((tile_id[:, None] >= tend[None, :]).astype(I32), axis=1)
    active = tile_id < tend[-1]
    last_e = jnp.sum((tend[-1] - 1 >= tend).astype(I32))
    te = jnp.where(active, te, last_e)
    tv = jnp.where(active, jnp.clip(counts[te] - (tile_id - tstart[te]) * tm, 0, tm), 0)
    slots = slot.reshape(t, 2)
    n_active = (tend[-1] * tm).astype(I32).reshape(1)
    return src, n_active, te.astype(I32), tv.astype(I32), slots[:, 0], slots[:, 1]


def _bias_table(rel_bias):
    h = rel_bias.shape[1]
    return jnp.full((h, LANES), NEG, F32).at[:, :N_BUCKETS].set(rel_bias.T * LOG2E)


def kernel(x, c, positions, rel_bias, w_mod, b_mod, g_attn, g_ffn, a_w_in, a_w_out, a_g_qn, a_g_kn,
           kv_w_mod, kv_b_mod, kv_g, kv_w, b_g_kn, b_w_q, b_w_out, b_g_qn, ffn_w1, ffn_w3, ffn_w2,
           moe_router, moe_router_b, moe_w1, moe_w3, moe_w2):
    batch, seq, d = x.shape
    m = batch * seq
    x2 = x.reshape(m, d)
    positions = positions.astype(I32)

    c8 = jnp.zeros((8, d), F32).at[:batch].set(c)
    mod0 = _mod_call(c8, w_mod, 0, b_mod)[:batch]
    mod1 = _mod_call(c8, w_mod, 1, b_mod)[:batch]
    kvm = _mod_call(c8, kv_w_mod[None], 0, kv_b_mod[None])[:batch]
    sh1_0, sc1_0, gt1_0, sh2_0, sc2_0, gt2_0 = jnp.split(mod0, 6, axis=-1)
    sh1_1, sc1_1, gt1_1, sh2_1, sc2_1, gt2_1 = jnp.split(mod1, 6, axis=-1)
    kv_sh, kv_sc = jnp.split(kvm, 2, axis=-1)

    tab = _bias_table(rel_bias)

    a_main = A_HEADS * A_HEAD_DIM + 2 * A_KV_HEADS * A_HEAD_DIM + IDX_HEADS * IDX_DIM
    n_tail = IDX_DIM + IDX_HEADS
    w_tail = jnp.zeros((d, LANES), F32).at[:, :n_tail].set(a_w_in[0, :, a_main:a_main + n_tail])
    qkv, tail = _nm_matmul(x2, g_attn[0], sc1_0, sh1_0, a_w_in, 0, a_main, seq, w_tail=w_tail)
    attn = _dsa_attention(qkv, tail, positions, tab,
                          a_g_qn[0].reshape(1, -1), a_g_kn[0].reshape(1, -1), batch, seq)
    x2 = _matmul_residual(attn, a_w_out, 0, x2, gt1_0, seq, tn=512)
    u = _nm_swiglu(x2, g_ffn[0], sc2_0, sh2_0, ffn_w1, ffn_w3, 0, seq)
    x2 = _matmul_residual(u, ffn_w2, 0, x2, gt2_0, seq)

    b_q = len(B_DILATIONS) * B_HEADS * B_HEAD_DIM
    dil = tuple(r for _, r in B_DILATIONS)
    gcols = B_HEADS * B_HEAD_DIM
    kvall = _nm_matmul(x2, kv_g, kv_sc, kv_sh, kv_w[None], 0, 2 * b_q, seq,
                       dilations=dil, group_cols=gcols, tn=256)
    qall = _nm_matmul(x2, g_attn[1], sc1_1, sh1_1, b_w_q, 0, b_q, seq,
                      dilations=dil, group_cols=gcols, tn=256)
    gq2 = jnp.tile(b_g_qn[0], 2).reshape(1, LANES)
    gk2 = jnp.tile(b_g_kn, 2).reshape(1, LANES)
    w = B_HEADS * B_HEAD_DIM
    ng = len(B_DILATIONS)
    outs, lses = [], []
    for g, (window, r) in enumerate(B_DILATIONS):
        pos_g = positions.reshape(batch, seq // r, r).transpose(0, 2, 1).reshape(batch, seq)
        o_g, lse_g = _win_attention(qall, g, kvall, g, kvall, ng + g, pos_g, tab, gq2, gk2,
                                    batch, seq, r, window)
        outs.append(o_g)
        lses.append(lse_g)
    x2 = _merge_out(outs, lses, dil, b_w_out, 0, x2, gt1_1, seq)

    h, route = _router(x2, g_ffn[1], sc2_1, sh2_1, moe_router[0], moe_router_b[0], seq)
    share = (2 * m) // N_EXPERTS
    tm = -(-(share * 17 // 32) // 64) * 64
    n_tiles = (2 * m) // tm + N_EXPERTS
    src, n_active, te, tv, p1, p2 = _routing_tables(route, tm, n_tiles)
    hs = _gather_rows(h, src, n_active, tm // 4)
    ys = _moe_experts(hs, te, tv, moe_w1.reshape(moe_w1.shape[1:]), moe_w3.reshape(moe_w3.shape[1:]),
                      moe_w2.reshape(moe_w2.shape[1:]), tm)
    out = _combine(ys, p1, p2, x2, gt2_1, route, seq)
    return out.reshape(batch, seq, d)
```

```python
import functools
import math

import jax
import jax.numpy as jnp
from jax import lax
from jax.experimental import pallas as pl
from jax.experimental.pallas import tpu as pltpu

F32 = jnp.float32
BF16 = jnp.bfloat16
I32 = jnp.int32

EPS = 1e-6
NEG = -1e30
INT_MIN = -(2 ** 31)
LOG2E = 1.0 / math.log(2.0)

A_HEADS, A_KV_HEADS, A_HEAD_DIM = 16, 4, 128
IDX_HEADS, IDX_DIM = 16, 64
TOPK_MAX = 256
B_DILATIONS = ((128, 1), (512, 4), (2048, 16))
B_HEADS, B_HEAD_DIM = 16, 64
N_BUCKETS, MAX_DISTANCE = 32, 2048
N_EXPERTS = 8
LANES = 128

VMEM_LIMIT_BYTES = 56 * 1024 * 1024

_NT = (((1,), (1,)), ((), ()))


def _params(*sem):
    return pltpu.CompilerParams(dimension_semantics=sem, vmem_limit_bytes=VMEM_LIMIT_BYTES)


def _dot(a, b):
    return jnp.dot(a, b, preferred_element_type=F32)


def _dot_nt(a, b):
    return lax.dot_general(a, b, _NT, preferred_element_type=F32)


def _sigmoid(x):
    return 1.0 / (1.0 + jnp.exp(-x))


def _t5_bucket(rel):
    n = jnp.maximum(rel, 0)
    max_exact = N_BUCKETS // 2
    nf = jnp.maximum(n, 1).astype(F32)
    large = max_exact + (jnp.log(nf / max_exact) / math.log(MAX_DISTANCE / max_exact)
                         * (N_BUCKETS - max_exact)).astype(I32)
    large = jnp.minimum(large, N_BUCKETS - 1)
    return jnp.where(n < max_exact, n, large)


def _norm_mod(x, g, sc, sh):
    y = x * lax.rsqrt(jnp.mean(x * x, axis=-1, keepdims=True) + EPS)
    return (y * g) * (1.0 + sc) + sh


def _bias_lookup(tab_ref, heads, bkt):
    rows, cols = bkt.shape
    tabs = [jnp.broadcast_to(tab_ref[h:h + 1, :], (8, LANES)) for h in heads]
    tiles = [[] for _ in tabs]
    for r in range(rows // 8):
        pieces = [[] for _ in tabs]
        for c in range(cols // LANES):
            idx = bkt[r * 8:(r + 1) * 8, c * LANES:(c + 1) * LANES]
            for k, tab in enumerate(tabs):
                pieces[k].append(jnp.take_along_axis(tab, idx, axis=1))
        for k in range(len(tabs)):
            tiles[k].append(pieces[k][0] if len(pieces[k]) == 1 else jnp.concatenate(pieces[k], axis=1))
    return [jnp.concatenate(t, axis=0) for t in tiles]


def _mod_body(c_ref, w_ref, b_ref, o_ref):
    c = c_ref[...]
    cs = c * _sigmoid(c)
    o_ref[...] = _dot(cs.astype(BF16), w_ref[...].astype(BF16)) + b_ref[...]


def _mod_call(c8, w3, layer, b2):
    _, d, n = w3.shape
    tn = 1024
    return pl.pallas_call(
        _mod_body,
        grid=(n // tn,),
        in_specs=[pl.BlockSpec((8, d), lambda j: (0, 0)),
                  pl.BlockSpec((None, d, tn), lambda j: (layer, 0, j)),
                  pl.BlockSpec((None, 1, tn), lambda j: (layer, 0, j))],
        out_specs=pl.BlockSpec((8, tn), lambda j: (0, j)),
        out_shape=jax.ShapeDtypeStruct((8, n), F32),
        compiler_params=_params("arbitrary"),
        name="adaln_mod",
    )(c8, w3, b2.reshape(b2.shape[0], 1, n))


def _nm_body(x_ref, g_ref, sc_ref, sh_ref, w_ref, *rest, has_tail, dilations, group_cols):
    if has_tail:
        wt_ref, o_ref, ot_ref, h_scr = rest
    elif dilations:
        o_ref, h_scr, y_scr = rest
    else:
        o_ref, h_scr = rest

    @pl.when(pl.program_id(1) == 0)
    def _():
        h = _norm_mod(x_ref[...], g_ref[...], sc_ref[0], sh_ref[0]).astype(BF16)
        h_scr[...] = h
        if has_tail:
            ot_ref[...] = _dot(h, wt_ref[...].astype(BF16)).astype(ot_ref.dtype)

    y = _dot(h_scr[...], w_ref[...].astype(BF16))
    if not dilations:
        o_ref[...] = y.astype(o_ref.dtype)
        return
    tm, tn = o_ref.shape
    group = (pl.program_id(1) * tn // group_cols) % len(dilations)
    for k, r in enumerate(dilations):
        @pl.when(group == k)
        def _(r=r):
            if r == 1:
                o_ref[...] = y.astype(o_ref.dtype)
                return
            for c in range(tn // LANES):
                y_scr[c] = y[:, c * LANES:(c + 1) * LANES]
            cls = tm // r
            for rho in range(r):
                for c in range(tn // LANES):
                    o_ref[rho * cls:(rho + 1) * cls, c * LANES:(c + 1) * LANES] = (
                        y_scr[c, pl.ds(rho, cls, stride=r), :].astype(o_ref.dtype))


def _nm_matmul(x, g, sc, sh, w3, layer, n_cols, seq, w_tail=None, dilations=None, group_cols=None,
               tm=2048, tn=512):
    m, d = x.shape
    tm = min(tm, seq)
    nb = seq // tm
    has_tail = w_tail is not None
    assert not dilations or (tm == seq and not has_tail and group_cols % tn == 0)
    in_specs = [pl.BlockSpec((tm, d), lambda i, j: (i, 0), pipeline_mode=pl.Buffered(1)),
                pl.BlockSpec((1, d), lambda i, j: (0, 0)),
                pl.BlockSpec((1, 1, d), lambda i, j: (i // nb, 0, 0)),
                pl.BlockSpec((1, 1, d), lambda i, j: (i // nb, 0, 0)),
                pl.BlockSpec((None, d, tn), lambda i, j: (layer, 0, j))]
    args = [x, g.reshape(1, d), sc.reshape(-1, 1, d), sh.reshape(-1, 1, d), w3]
    out_specs = [pl.BlockSpec((tm, tn), lambda i, j: (i, j))]
    out_shape = [jax.ShapeDtypeStruct((m, n_cols), BF16)]
    if has_tail:
        in_specs.append(pl.BlockSpec((d, LANES), lambda i, j: (0, 0)))
        args.append(w_tail)
        out_specs.append(pl.BlockSpec((tm, LANES), lambda i, j: (i, 0)))
        out_shape.append(jax.ShapeDtypeStruct((m, LANES), BF16))
    scratch = [pltpu.VMEM((tm, d), BF16)]
    if dilations:
        scratch.append(pltpu.VMEM((tn // LANES, tm, LANES), F32))
    res = pl.pallas_call(
        functools.partial(_nm_body, has_tail=has_tail, dilations=dilations, group_cols=group_cols),
        grid=(m // tm, n_cols // tn),
        in_specs=in_specs,
        out_specs=out_specs,
        out_shape=out_shape,
        scratch_shapes=scratch,
        compiler_params=_params("arbitrary", "arbitrary"),
        name="norm_mod_matmul",
    )(*args)
    return res if has_tail else res[0]


def _mmres_body(a_ref, w_ref, x_ref, gt_ref, o_ref):
    o_ref[...] = x_ref[...] + gt_ref[0] * _dot(a_ref[...], w_ref[...].astype(BF16))


def _matmul_residual(a, w3, layer, x, gt, seq, tm=2048, tn=256):
    m, k = a.shape
    d = x.shape[1]
    tm = min(tm, seq)
    nb = seq // tm
    return pl.pallas_call(
        _mmres_body,
        grid=(m // tm, d // tn),
        in_specs=[pl.BlockSpec((tm, k), lambda i, j: (i, 0), pipeline_mode=pl.Buffered(1)),
                  pl.BlockSpec((None, k, tn), lambda i, j: (layer, 0, j)),
                  pl.BlockSpec((tm, tn), lambda i, j: (i, j)),
                  pl.BlockSpec((1, 1, tn), lambda i, j: (i // nb, 0, j))],
        out_specs=pl.BlockSpec((tm, tn), lambda i, j: (i, j)),
        out_shape=jax.ShapeDtypeStruct((m, d), F32),
        compiler_params=_params("arbitrary", "arbitrary"),
        name="matmul_residual",
    )(a, w3, x, gt.reshape(-1, 1, d))


def _nm_swiglu_body(x_ref, g_ref, sc_ref, sh_ref, w1_ref, w3_ref, o_ref, h_scr):
    @pl.when(pl.program_id(1) == 0)
    def _():
        h_scr[...] = _norm_mod(x_ref[...], g_ref[...], sc_ref[0], sh_ref[0]).astype(BF16)

    h = h_scr[...]
    a = _dot(h, w1_ref[...].astype(BF16))
    b = _dot(h, w3_ref[...].astype(BF16))
    o_ref[...] = (a * _sigmoid(a) * b).astype(o_ref.dtype)


def _nm_swiglu(x, g, sc, sh, w1, w3, layer, seq, tm=2048, tf=256):
    m, d = x.shape
    f = w1.shape[2]
    tm = min(tm, seq)
    nb = seq // tm
    return pl.pallas_call(
        _nm_swiglu_body,
        grid=(m // tm, f // tf),
        in_specs=[pl.BlockSpec((tm, d), lambda i, j: (i, 0), pipeline_mode=pl.Buffered(1)),
                  pl.BlockSpec((1, d), lambda i, j: (0, 0)),
                  pl.BlockSpec((1, 1, d), lambda i, j: (i // nb, 0, 0)),
                  pl.BlockSpec((1, 1, d), lambda i, j: (i // nb, 0, 0)),
                  pl.BlockSpec((None, d, tf), lambda i, j: (layer, 0, j)),
                  pl.BlockSpec((None, d, tf), lambda i, j: (layer, 0, j))],
        out_specs=pl.BlockSpec((tm, tf), lambda i, j: (i, j)),
        out_shape=jax.ShapeDtypeStruct((m, f), BF16),
        scratch_shapes=[pltpu.VMEM((tm, d), BF16)],
        compiler_params=_params("arbitrary", "arbitrary"),
        name="norm_mod_swiglu_up",
    )(x, g.reshape(1, d), sc.reshape(-1, 1, d), sh.reshape(-1, 1, d), w1, w3)


def _dsa_body(q_ref, qi_ref, k_ref, v_ref, tq_ref, tk_ref, pr_ref, pc_ref, tab_ref, gq_ref, gk_ref,
              o_ref,
              kn_scr, vt_scr, kke_scr, kko_scr, qst_scr, wib_scr, key_scr, qn_scr, m_scr, l_scr, acc_scr,
              s_buf, cm_buf,
              *, topk, tq, seq):
    i = pl.program_id(1)
    nc = i + 1
    grp = A_HEADS // A_KV_HEADS
    hd = A_HEAD_DIM
    lane = lax.broadcasted_iota(I32, (1, LANES), 1)
    krow = lax.broadcasted_iota(I32, (tq, tq), 0)
    qcol = lax.broadcasted_iota(I32, (tq, tq), 1)

    @pl.when(i == 0)
    def _prepare_keys():
        gk = gk_ref[...]

        def body(r, carry):
            rows = pl.ds(pl.multiple_of(r * tq, tq), tq)
            for kh in range(A_KV_HEADS):
                cols = slice(kh * hd, (kh + 1) * hd)
                kb = k_ref[rows, cols].astype(F32)
                ms = jnp.mean(kb * kb, axis=-1, keepdims=True)
                kn_scr[rows, cols] = ((kb * lax.rsqrt(ms + EPS)) * gk).astype(BF16)
                vt_scr[r, cols, :] = v_ref[rows, cols].astype(F32).T.astype(BF16)
            t = tk_ref[rows, :].astype(F32)
            kke_scr[rows, :] = jnp.where(lane < IDX_DIM, t, 0.0).astype(BF16)
            kko_scr[rows, :] = jnp.where(lane >= IDX_DIM, pltpu.roll(t, IDX_DIM, 1), 0.0).astype(BF16)
            return carry

        lax.fori_loop(0, seq // tq, body, 0)

    for j in range(IDX_HEADS // 2):
        qst_scr[j * tq:(j + 1) * tq, :] = qi_ref[:, j * LANES:(j + 1) * LANES]
    w_scale = (IDX_DIM ** -0.5) * (IDX_HEADS ** -0.5)
    wib_scr[...] = tq_ref[...].astype(F32).T[IDX_DIM:IDX_DIM + IDX_HEADS, :] * w_scale

    def idx_body(c, carry):
        rows = pl.ds(pl.multiple_of(c * tq, tq), tq)
        qst = qst_scr[...]
        re = _dot_nt(kke_scr[rows, :], qst)
        ro = _dot_nt(kko_scr[rows, :], qst)
        acc = jnp.zeros((tq, tq), F32)
        for j in range(IDX_HEADS // 2):
            acc = acc + jnp.maximum(re[:, j * tq:(j + 1) * tq], 0.0) * wib_scr[2 * j:2 * j + 1, :]
            acc = acc + jnp.maximum(ro[:, j * tq:(j + 1) * tq], 0.0) * wib_scr[2 * j + 1:2 * j + 2, :]
        bits = pltpu.bitcast(acc, I32)
        key = bits ^ ((bits >> 31) & 0x7FFFFFFF)
        causal = (c < i) | (krow <= qcol)
        key_scr[c] = jnp.where(causal, key, INT_MIN)
        return carry

    lax.fori_loop(0, nc, idx_body, 0)

    def bit_body(bi, t_u):
        cand_u = t_u | lax.shift_left(jnp.int32(1), 31 - bi)
        cand_s = cand_u ^ INT_MIN

        def cnt_body(c, cnt):
            ge = jnp.where(key_scr[c] >= cand_s, 1, 0)
            return cnt + jnp.sum(ge.reshape(tq // 8, 8, tq), axis=0)

        cnt = lax.fori_loop(0, nc, cnt_body, jnp.zeros((8, tq), I32))
        total = jnp.sum(cnt, axis=0, keepdims=True)
        return jnp.where(total >= topk, cand_u, t_u)

    nbits = jnp.where(nc * tq > topk, 32, 0)
    t_u = lax.fori_loop(0, nbits, bit_body, jnp.zeros((1, tq), I32))
    thr = t_u ^ INT_MIN

    qpos = pr_ref[0, i]

    def bkt_body(c, carry):
        rows = pl.ds(pl.multiple_of(c * tq, tq), tq)
        bkt = _t5_bucket(qpos - pc_ref[rows, :])
        causal = (c < i) | (krow <= qcol)
        sel = (key_scr[c] >= thr) & causal
        key_scr[c] = jnp.where(sel, bkt, N_BUCKETS)
        return carry

    lax.fori_loop(0, nc, bkt_body, 0)

    gq = gq_ref[...]
    scale = (hd ** -0.5) * LOG2E
    for g in range(A_KV_HEADS):
        for hh in range(grp):
            h = g * grp + hh
            qh = q_ref[:, h * hd:(h + 1) * hd].astype(F32)
            ms = jnp.mean(qh * qh, axis=-1, keepdims=True)
            qn_scr[hh * tq:(hh + 1) * tq, :] = (((qh * lax.rsqrt(ms + EPS)) * gq) * scale).astype(BF16)
        m_scr[...] = jnp.full(m_scr.shape, NEG, F32)
        l_scr[...] = jnp.zeros(l_scr.shape, F32)
        acc_scr[...] = jnp.zeros(acc_scr.shape, F32)

        def scores(c, g=g):
            rows = pl.ds(pl.multiple_of(c * tq, tq), tq)
            s = _dot_nt(kn_scr[rows, g * hd:(g + 1) * hd], qn_scr[...])
            bias = _bias_lookup(tab_ref, range(g * grp, (g + 1) * grp), key_scr[c])
            s = jnp.concatenate([s[:, hh * tq:(hh + 1) * tq] + bias[hh] for hh in range(grp)], axis=1)
            s_buf[c % 2] = s
            cm_buf[c % 2] = jnp.max(s, axis=0, keepdims=True)

        def accumulate(c, g=g):
            s = s_buf[c % 2]
            m_old = m_scr[...]
            m_new = jnp.maximum(m_old, cm_buf[c % 2])
            p = jnp.exp2(s - m_new)
            alpha = jnp.exp2(m_old - m_new)
            l_scr[...] = alpha * l_scr[...] + jnp.sum(p, axis=0, keepdims=True)
            acc_scr[...] = alpha * acc_scr[...] + _dot(vt_scr[c, g * hd:(g + 1) * hd, :], p.astype(BF16))
            m_scr[...] = m_new

        def att_body(c, carry):
            accumulate(c)
            scores(c + 1)
            return carry

        scores(0)
        lax.fori_loop(0, nc - 1, att_body, 0)
        accumulate(nc - 1)
        o = acc_scr[...] * (1.0 / l_scr[...])
        for hh in range(grp):
            h = g * grp + hh
            o_ref[:, h * hd:(h + 1) * hd] = o[:, hh * tq:(hh + 1) * tq].T.astype(o_ref.dtype)


def _dsa_attention(qkv, tail, pos, tab, gq, gk, batch, seq, tq=256):
    tq = min(tq, seq)
    nq = seq // tq
    a_q = A_HEADS * A_HEAD_DIM
    a_kv = A_KV_HEADS * A_HEAD_DIM
    a_qi = IDX_HEADS * IDX_DIM
    topk = min(TOPK_MAX, seq // 4)
    grp = A_HEADS // A_KV_HEADS
    body = functools.partial(_dsa_body, topk=topk, tq=tq, seq=seq)
    return pl.pallas_call(
        body,
        grid=(batch, nq),
        in_specs=[
            pl.BlockSpec((tq, a_q), lambda b, i: (b * nq + i, 0)),
            pl.BlockSpec((tq, a_qi), lambda b, i: (b * nq + i, (a_q + 2 * a_kv) // a_qi)),
            pl.BlockSpec((seq, a_kv), lambda b, i: (b, a_q // a_kv)),
            pl.BlockSpec((seq, a_kv), lambda b, i: (b, a_q // a_kv + 1)),
            pl.BlockSpec((tq, LANES), lambda b, i: (b * nq + i, 0)),
            pl.BlockSpec((seq, LANES), lambda b, i: (b, 0)),
            pl.BlockSpec((1, nq, 1, tq), lambda b, i: (b, 0, 0, 0)),
            pl.BlockSpec((seq, 1), lambda b, i: (b, 0)),
            pl.BlockSpec((A_HEADS, LANES), lambda b, i: (0, 0)),
            pl.BlockSpec((1, A_HEAD_DIM), lambda b, i: (0, 0)),
            pl.BlockSpec((1, A_HEAD_DIM), lambda b, i: (0, 0)),
        ],
        out_specs=pl.BlockSpec((tq, a_q), lambda b, i: (b * nq + i, 0)),
        out_shape=jax.ShapeDtypeStruct((batch * seq, a_q), BF16),
        scratch_shapes=[
            pltpu.VMEM((seq, a_kv), BF16),
            pltpu.VMEM((nq, a_kv, tq), BF16),
            pltpu.VMEM((seq, LANES), BF16),
            pltpu.VMEM((seq, LANES), BF16),
            pltpu.VMEM((IDX_HEADS // 2 * tq, LANES), BF16),
            pltpu.VMEM((IDX_HEADS, tq), F32),
            pltpu.VMEM((nq, tq, tq), I32),
            pltpu.VMEM((grp * tq, A_HEAD_DIM), BF16),
            pltpu.VMEM((1, grp * tq), F32),
            pltpu.VMEM((1, grp * tq), F32),
            pltpu.VMEM((A_HEAD_DIM, grp * tq), F32),
            pltpu.VMEM((2, tq, grp * tq), F32),
            pltpu.VMEM((2, 1, grp * tq), F32),
        ],
        compiler_params=_params("arbitrary", "arbitrary"),
        name="dsa_attention",
    )(qkv, qkv, qkv, qkv, tail, tail, pos.reshape(batch, nq, 1, tq), pos.reshape(batch * seq, 1),
      tab, gq, gk)


def _win_body(*refs, cls_len, wk, use_prev, tq):
    if use_prev:
        (q_ref, kc_ref, vc_ref, kp_ref, vp_ref, pr_ref, pcc_ref, pcp_ref, tab_ref, gq_ref, gk_ref,
         o_ref, lse_ref, kn_scr, vt_scr, qn_scr, bkt_scr, bias_scr, lse_scr) = refs
    else:
        (q_ref, kc_ref, vc_ref, pr_ref, pcc_ref, tab_ref, gq_ref, gk_ref,
         o_ref, lse_ref, kn_scr, vt_scr, qn_scr, bkt_scr, bias_scr, lse_scr) = refs
    t = pl.program_id(1)
    hd = B_HEAD_DIM
    pw = 2 * hd
    blk = LANES
    shift = cls_len.bit_length() - 1
    koff = blk if use_prev else 0
    win = 2 * blk if cls_len > blk else blk
    lo = lax.broadcasted_iota(I32, (1, pw), 1) < hd
    lo_rows = lax.broadcasted_iota(I32, (pw, 1), 0) < hd
    gq = gq_ref[...]
    gk = gk_ref[...]
    scale = (hd ** -0.5) * LOG2E

    def pair_norm(x, g):
        sq = x * x
        ms_lo = jnp.sum(jnp.where(lo, sq, 0.0), axis=-1, keepdims=True) * (1.0 / hd)
        ms_hi = jnp.sum(jnp.where(lo, 0.0, sq), axis=-1, keepdims=True) * (1.0 / hd)
        inv = jnp.where(lo, lax.rsqrt(ms_lo + EPS), lax.rsqrt(ms_hi + EPS))
        return (x * inv) * g

    for p in range(B_HEADS // 2):
        cols = slice(p * pw, (p + 1) * pw)
        qn_scr[:, cols] = (pair_norm(q_ref[:, cols].astype(F32), gq) * scale).astype(BF16)
        if use_prev:
            kn_scr[0:blk, cols] = pair_norm(kp_ref[:, cols].astype(F32), gk).astype(BF16)
            vt_scr[p, :, 0:blk] = vp_ref[:, cols].astype(F32).T.astype(BF16)
        kn_scr[koff:koff + tq, cols] = pair_norm(kc_ref[:, cols].astype(F32), gk).astype(BF16)
        vt_scr[p, :, koff:koff + tq] = vc_ref[:, cols].astype(F32).T.astype(BF16)

    lse_scr[...] = jnp.zeros(lse_scr.shape, F32)
    qpos_all = pr_ref[0]
    for j in range(tq // blk):
        qrows = slice(j * blk, (j + 1) * blk)
        own = koff + j * blk
        k0 = own - blk if (win > blk and own >= blk) else own
        if use_prev and k0 < koff:
            kpos = jnp.concatenate([pcp_ref[k0:koff, :], pcc_ref[0:k0 + win - koff, :]], axis=0)
        else:
            kpos = pcc_ref[k0 - koff:k0 - koff + win, :]
        fq = t * tq + j * blk + lax.broadcasted_iota(I32, (1, blk), 1)
        fk = t * tq + (k0 - koff) + lax.broadcasted_iota(I32, (win, 1), 0)
        same = ((fq + cls_len) >> shift) == ((fk + cls_len) >> shift)
        dist = (fq & (cls_len - 1)) - (fk & (cls_len - 1))
        ok = same & (dist >= 0) & (dist <= wk)
        bkt_scr[...] = jnp.where(ok, _t5_bucket(qpos_all[:, qrows] - kpos), N_BUCKETS)

        def bias_rows(r8, carry):
            rows = pl.ds(pl.multiple_of(r8 * 8, 8), 8)
            idx = bkt_scr[rows, :]
            for h in range(B_HEADS):
                tab = jnp.broadcast_to(tab_ref[h:h + 1, :], (8, LANES))
                bias_scr[h // 2, rows, (h % 2) * blk:(h % 2 + 1) * blk] = jnp.take_along_axis(tab, idx, axis=1)
            return carry

        lax.fori_loop(0, win // 8, bias_rows, 0, unroll=8)

        for p in range(B_HEADS // 2):
            cols = slice(p * pw, (p + 1) * pw)
            qp = qn_scr[qrows, cols]
            zero = jnp.zeros_like(qp)
            heads = (jnp.where(lo, qp, zero), jnp.where(lo, zero, qp))
            kw = kn_scr[k0:k0 + win, cols]
            vtw = vt_scr[p, :, k0:k0 + win]
            hp = 2 * blk // win
            outs = []
            for c in range(2 // hp):
                q2 = heads[c] if hp == 1 else jnp.concatenate(heads, axis=0)
                s = _dot_nt(kw, q2) + bias_scr[p, :, c * hp * blk:(c + 1) * hp * blk]
                m = jnp.max(s, axis=0, keepdims=True)
                e = jnp.exp2(s - m)
                l = jnp.sum(e, axis=0, keepdims=True)
                ot = _dot(vtw, e.astype(BF16)) * (1.0 / l)
                lse = m + jnp.log(l) * LOG2E
                for k in range(hp):
                    h = 2 * p + c * hp + k
                    outs.append(ot[:, k * blk:(k + 1) * blk])
                    lse_scr[h:h + 1, :] = lse[:, k * blk:(k + 1) * blk]
            o_ref[qrows, cols] = jnp.where(lo_rows, outs[0], outs[1]).T.astype(o_ref.dtype)
        lse_ref[qrows, :] = lse_scr[...].T


def _win_attention(q_arr, q_col, k_arr, k_col, v_arr, v_col, pos, tab, gq2, gk2,
                   batch, seq, dilation, window, tq=512):
    cls_len = seq // dilation
    wk = window // dilation
    assert cls_len & (cls_len - 1) == 0 and wk <= LANES
    tq = min(tq, seq)
    use_prev = cls_len > tq
    assert use_prev or tq % cls_len == 0
    nt = seq // tq
    w = B_HEADS * B_HEAD_DIM
    sub = tq // LANES
    pos_col = pos.reshape(batch * seq, 1)
    pos_row = pos.reshape(batch * nt, 1, tq)
    prev = lambda b, t: jnp.maximum((b * nt + t) * sub - 1, 0)
    in_specs = [pl.BlockSpec((tq, w), lambda b, t: (b * nt + t, q_col)),
                pl.BlockSpec((tq, w), lambda b, t: (b * nt + t, k_col)),
                pl.BlockSpec((tq, w), lambda b, t: (b * nt + t, v_col))]
    args = [q_arr, k_arr, v_arr]
    if use_prev:
        in_specs += [pl.BlockSpec((LANES, w), lambda b, t: (prev(b, t), k_col)),
                     pl.BlockSpec((LANES, w), lambda b, t: (prev(b, t), v_col))]
        args += [k_arr, v_arr]
    in_specs += [pl.BlockSpec((1, 1, tq), lambda b, t: (b * nt + t, 0, 0)),
                 pl.BlockSpec((tq, 1), lambda b, t: (b * nt + t, 0))]
    args += [pos_row, pos_col]
    if use_prev:
        in_specs.append(pl.BlockSpec((LANES, 1), lambda b, t: (prev(b, t), 0)))
        args.append(pos_col)
    in_specs += [pl.BlockSpec((B_HEADS, LANES), lambda b, t: (0, 0)),
                 pl.BlockSpec((1, LANES), lambda b, t: (0, 0)),
                 pl.BlockSpec((1, LANES), lambda b, t: (0, 0))]
    args += [tab, gq2, gk2]
    nk = tq + (LANES if use_prev else 0)
    win = 2 * LANES if cls_len > LANES else LANES
    body = functools.partial(_win_body, cls_len=cls_len, wk=wk, use_prev=use_prev, tq=tq)
    return pl.pallas_call(
        body,
        grid=(batch, nt),
        in_specs=in_specs,
        out_specs=[pl.BlockSpec((tq, w), lambda b, t: (b * nt + t, 0)),
                   pl.BlockSpec((tq, LANES), lambda b, t: (b * nt + t, 0))],
        out_shape=[jax.ShapeDtypeStruct((batch * seq, w), BF16),
                   jax.ShapeDtypeStruct((batch * seq, LANES), F32)],
        scratch_shapes=[pltpu.VMEM((nk, w), BF16),
                        pltpu.VMEM((B_HEADS // 2, LANES, nk), BF16),
                        pltpu.VMEM((tq, w), BF16),
                        pltpu.VMEM((win, LANES), I32),
                        pltpu.VMEM((B_HEADS // 2, win, 2 * LANES), F32),
                        pltpu.VMEM((LANES, LANES), F32)],
        compiler_params=_params("arbitrary", "arbitrary"),
        name="dilated_attention",
    )(*args)


def _merge_body(*refs, dilations):
    ng = len(dilations)
    o_refs, l_refs = refs[:ng], refs[ng:2 * ng]
    e_ref, w_ref, x_ref, gt_ref, out_ref, a_scr, o_scr, l_scr = refs[2 * ng:]
    tm, k = a_scr.shape

    @pl.when(pl.program_id(1) == 0)
    def _():
        for g, r in enumerate(dilations):
            cls = tm // r
            for rho in range(r):
                l_scr[g, pl.ds(rho, cls, stride=r), :] = l_refs[g][0, rho]
                for c in range(k // LANES):
                    o_scr[g, c, pl.ds(rho, cls, stride=r), :] = (
                        o_refs[g][0, rho, :, c * LANES:(c + 1) * LANES].astype(F32))
        ls = [l_scr[g] for g in range(ng)]
        m = functools.reduce(jnp.maximum, ls)
        ws = [jnp.exp2(l - m) for l in ls]
        inv = 1.0 / functools.reduce(lambda a, b: a + b, ws)
        wides = []
        for wg in ws:
            wn = wg * inv
            hi = wn.astype(BF16)
            lo = (wn - hi.astype(F32)).astype(BF16)
            wides.append(_dot(hi, e_ref[...]) + _dot(lo, e_ref[...]))
        for c in range(k // LANES):
            cols = slice(c * LANES, (c + 1) * LANES)
            num = wides[0][:, cols] * o_scr[0, c]
            for g in range(1, ng):
                num = num + wides[g][:, cols] * o_scr[g, c]
            a_scr[:, cols] = num.astype(BF16)

    out_ref[...] = x_ref[...] + gt_ref[0] * _dot(a_scr[...], w_ref[...].astype(BF16))


def _merge_out(os_, ls_, dilations, w3, layer, x, gt, seq, tm=512, tn=512):
    m, k = os_[0].shape
    d = x.shape[1]
    nb = seq // tm
    batch = m // seq
    head = jnp.arange(k, dtype=I32)[None, :] // B_HEAD_DIM
    expand = (jnp.arange(LANES, dtype=I32)[:, None] == head).astype(BF16)
    ng = len(dilations)
    o_specs = [pl.BlockSpec((1, r, tm // r, k), lambda i, j: (i // nb, 0, i % nb, 0)) for r in dilations]
    l_specs = [pl.BlockSpec((1, r, tm // r, LANES), lambda i, j: (i // nb, 0, i % nb, 0)) for r in dilations]
    o_args = [o.reshape(batch, r, seq // r, k) for o, r in zip(os_, dilations)]
    l_args = [l.reshape(batch, r, seq // r, LANES) for l, r in zip(ls_, dilations)]
    return pl.pallas_call(
        functools.partial(_merge_body, dilations=tuple(dilations)),
        grid=(m // tm, d // tn),
        in_specs=o_specs + l_specs + [
            pl.BlockSpec((LANES, k), lambda i, j: (0, 0)),
            pl.BlockSpec((None, k, tn), lambda i, j: (layer, 0, j)),
            pl.BlockSpec((tm, tn), lambda i, j: (i, j)),
            pl.BlockSpec((1, 1, tn), lambda i, j: (i // nb, 0, j))],
        out_specs=pl.BlockSpec((tm, tn), lambda i, j: (i, j)),
        out_shape=jax.ShapeDtypeStruct((m, d), F32),
        scratch_shapes=[pltpu.VMEM((tm, k), BF16),
                        pltpu.VMEM((ng, k // LANES, tm, LANES), F32),
                        pltpu.VMEM((ng, tm, LANES), F32)],
        compiler_params=_params("arbitrary", "arbitrary"),
        name="merge_out_proj",
    )(*o_args, *l_args, expand, w3, x, gt.reshape(-1, 1, d))


def _router_body(x_ref, g_ref, sc_ref, sh_ref, rh_ref, rl_ref, rb_ref, h_ref, rt_ref):
    h = _norm_mod(x_ref[...], g_ref[...], sc_ref[0], sh_ref[0])
    hh = h.astype(BF16)
    bits = pltpu.bitcast(hh.astype(F32), I32)
    half = h.shape[1] // 2
    h_ref[...] = lax.shift_right_logical(bits[:, :half], 16) | (bits[:, half:] & -65536)
    hl = (h - hh.astype(F32)).astype(BF16)
    logits = _dot(hh, rh_ref[...]) + _dot(hl, rh_ref[...]) + _dot(hh, rl_ref[...]) + rb_ref[...]
    lane = lax.broadcasted_iota(I32, logits.shape, 1)
    logits = jnp.where(lane < N_EXPERTS, logits, NEG)
    v1 = jnp.max(logits, axis=1, keepdims=True)
    i1 = jnp.min(jnp.where(logits == v1, lane, LANES), axis=1, keepdims=True)
    rest = jnp.where(lane == i1, NEG, logits)
    v2 = jnp.max(rest, axis=1, keepdims=True)
    i2 = jnp.min(jnp.where(rest == v2, lane, LANES), axis=1, keepdims=True)
    e = jnp.exp(v2 - v1)
    g1 = 1.0 / (1.0 + e)
    g2 = e * g1
    rt_ref[...] = jnp.where(lane == 0, i1.astype(F32),
                            jnp.where(lane == 1, i2.astype(F32),
                                      jnp.where(lane == 2, g1, jnp.where(lane == 3, g2, 0.0))))


def _router(x, g, sc, sh, rw, rb, seq, tm=512):
    m, d = x.shape
    nb = seq // tm
    ne = rw.shape[1]
    rw_p = jnp.zeros((d, LANES), F32).at[:, :ne].set(rw)
    rh = rw_p.astype(BF16)
    rl = (rw_p - rh.astype(F32)).astype(BF16)
    rb_p = jnp.zeros((1, LANES), F32).at[0, :ne].set(rb)
    return pl.pallas_call(
        _router_body,
        grid=(m // tm,),
        in_specs=[pl.BlockSpec((tm, d), lambda i: (i, 0)),
                  pl.BlockSpec((1, d), lambda i: (0, 0)),
                  pl.BlockSpec((1, 1, d), lambda i: (i // nb, 0, 0)),
                  pl.BlockSpec((1, 1, d), lambda i: (i // nb, 0, 0)),
                  pl.BlockSpec((d, LANES), lambda i: (0, 0)),
                  pl.BlockSpec((d, LANES), lambda i: (0, 0)),
                  pl.BlockSpec((1, LANES), lambda i: (0, 0))],
        out_specs=[pl.BlockSpec((tm, d // 2), lambda i: (i, 0)),
                   pl.BlockSpec((tm, LANES), lambda i: (i, 0))],
        out_shape=[jax.ShapeDtypeStruct((m, d // 2), I32),
                   jax.ShapeDtypeStruct((m, LANES), F32)],
        compiler_params=_params("arbitrary"),
        name="router_top2",
    )(x, g.reshape(1, d), sc.reshape(-1, 1, d), sh.reshape(-1, 1, d), rh, rl, rb_p)


def _row_copy(src_hbm, idx, buf, r, sem):
    return pltpu.make_async_copy(src_hbm.at[pl.ds(idx, 1), :], buf.at[pl.ds(r, 1), :], sem)


def _gather_body(src_ref, nact_ref, h_hbm, o_hbm, zero_buf, sem, *, rows):
    n_chunks = o_hbm.shape[0] // rows
    n_act = nact_ref[0] // rows
    zero_buf[...] = jnp.zeros(zero_buf.shape, zero_buf.dtype)

    def issue(k, carry):
        def body(r, c):
            _row_copy(h_hbm, src_ref[k * rows + r], o_hbm, k * rows + r, sem).start()
            return c

        lax.fori_loop(0, rows, body, 0, unroll=8)
        return carry

    lax.fori_loop(0, n_act, issue, 0)

    def fill(k, carry):
        pltpu.make_async_copy(zero_buf, o_hbm.at[pl.ds(k * rows, rows), :], sem).start()
        return carry

    lax.fori_loop(n_act, n_chunks, fill, 0)

    def drain(k, carry):
        pltpu.make_async_copy(zero_buf, o_hbm.at[pl.ds(k * rows, rows), :], sem).wait()
        return carry

    lax.fori_loop(0, n_chunks, drain, 0)


def _gather_rows(h, src, n_active, rows):
    r_total = src.shape[0]
    w = h.shape[1]
    return pl.pallas_call(
        functools.partial(_gather_body, rows=rows),
        grid_spec=pltpu.PrefetchScalarGridSpec(
            num_scalar_prefetch=2,
            grid=(1,),
            in_specs=[pl.BlockSpec(memory_space=pl.ANY)],
            out_specs=pl.BlockSpec(memory_space=pl.ANY),
            scratch_shapes=[pltpu.VMEM((rows, w), h.dtype), pltpu.SemaphoreType.DMA(())],
        ),
        out_shape=jax.ShapeDtypeStruct((r_total, w), h.dtype),
        compiler_params=_params("arbitrary"),
        name="moe_dispatch_gather",
    )(src, n_active, h)


def _moe_body(te_ref, tv_ref, hs_ref, w1_ref, w3_ref, w2_ref, o_ref, h_scr, *, ncol):
    t = pl.program_id(0)
    f = pl.program_id(1)
    valid = tv_ref[t]
    d = o_ref.shape[1]
    cw = d // ncol

    @pl.when((valid > 0) & (f == 0))
    def _():
        words = hs_ref[...]
        h_scr[:, :d // 2] = pltpu.bitcast(words << 16, F32).astype(BF16)
        h_scr[:, d // 2:] = pltpu.bitcast(words & -65536, F32).astype(BF16)

    @pl.when(valid > 0)
    def _():
        h = h_scr[...]
        a = _dot(h, w1_ref[...].astype(BF16))
        b = _dot(h, w3_ref[...].astype(BF16))
        u = (a * _sigmoid(a) * b).astype(BF16)
        w2 = w2_ref[...].astype(BF16)

        @pl.when(f == 0)
        def _():
            for n in range(ncol):
                o_ref[:, n * cw:(n + 1) * cw] = _dot(u, w2[:, n * cw:(n + 1) * cw])

        @pl.when(f > 0)
        def _():
            for n in range(ncol):
                o_ref[:, n * cw:(n + 1) * cw] += _dot(u, w2[:, n * cw:(n + 1) * cw])

    @pl.when((valid == 0) & (f == 0))
    def _():
        o_ref[...] = jnp.zeros(o_ref.shape, o_ref.dtype)


def _moe_experts(hs, tile_expert, tile_valid, w1, w3, w2, tm, tf=256, ncol=4):
    r_total = hs.shape[0]
    d = w1.shape[1]
    n_tiles = r_total // tm
    nf = w1.shape[2] // tf

    def w13_map(t, f, te, tv):
        return (te[t], 0, jnp.where(tv[t] > 0, f, nf - 1))

    def w2_map(t, f, te, tv):
        return (te[t], jnp.where(tv[t] > 0, f, nf - 1), 0)

    return pl.pallas_call(
        functools.partial(_moe_body, ncol=ncol),
        grid_spec=pltpu.PrefetchScalarGridSpec(
            num_scalar_prefetch=2,
            grid=(n_tiles, nf),
            in_specs=[pl.BlockSpec((tm, d // 2), lambda t, f, te, tv: (t, 0)),
                      pl.BlockSpec((None, d, tf), w13_map),
                      pl.BlockSpec((None, d, tf), w13_map),
                      pl.BlockSpec((None, tf, d), w2_map)],
            out_specs=pl.BlockSpec((tm, d), lambda t, f, te, tv: (t, 0)),
            scratch_shapes=[pltpu.VMEM((tm, d), BF16)],
        ),
        out_shape=jax.ShapeDtypeStruct((r_total, d), F32),
        compiler_params=_params("arbitrary", "arbitrary"),
        name="moe_experts",
    )(tile_expert, tile_valid, hs, w1, w3, w2)


def _combine_body(p1_ref, p2_ref, y_hbm, x_ref, gt_ref, rt_ref, o_ref, buf_a, buf_b, sem, *, rows):
    base = pl.program_id(0) * rows

    def issue(r, carry):
        _row_copy(y_hbm, p1_ref[base + r], buf_a, r, sem).start()
        _row_copy(y_hbm, p2_ref[base + r], buf_b, r, sem).start()
        return carry

    lax.fori_loop(0, rows, issue, 0, unroll=4)
    pltpu.make_async_copy(y_hbm.at[pl.ds(0, rows), :], buf_a, sem).wait()
    pltpu.make_async_copy(y_hbm.at[pl.ds(0, rows), :], buf_b, sem).wait()
    rt = rt_ref[...]
    g1 = rt[:, 2:3]
    g2 = rt[:, 3:4]
    o_ref[...] = x_ref[...] + gt_ref[0] * (g1 * buf_a[...] + g2 * buf_b[...])


def _combine(ys, p1, p2, x, gt, route, seq, rows=256):
    m, d = x.shape
    nb = seq // rows
    return pl.pallas_call(
        functools.partial(_combine_body, rows=rows),
        grid_spec=pltpu.PrefetchScalarGridSpec(
            num_scalar_prefetch=2,
            grid=(m // rows,),
            in_specs=[pl.BlockSpec(memory_space=pl.ANY),
                      pl.BlockSpec((rows, d), lambda i, a, b: (i, 0)),
                      pl.BlockSpec((1, 1, d), lambda i, a, b: (i // nb, 0, 0)),
                      pl.BlockSpec((rows, LANES), lambda i, a, b: (i, 0))],
            out_specs=pl.BlockSpec((rows, d), lambda i, a, b: (i, 0)),
            scratch_shapes=[pltpu.VMEM((rows, d), F32), pltpu.VMEM((rows, d), F32),
                            pltpu.SemaphoreType.DMA(())],
        ),
        out_shape=jax.ShapeDtypeStruct((m, d), F32),
        compiler_params=_params("arbitrary"),
        name="moe_combine",
    )(p1, p2, ys, x, gt.reshape(-1, 1, d), route)


def _routing_tables(route, tm, n_tiles):
    t = route.shape[0]
    experts = route[:, :2].astype(I32).reshape(-1)
    onehot = (experts[:, None] == jnp.arange(N_EXPERTS, dtype=I32)[None, :]).astype(I32)
    csum = jnp.cumsum(onehot, axis=0)
    rank = jnp.sum(csum * onehot, axis=1) - 1
    counts = csum[-1]
    tiles = (counts + tm - 1) // tm
    tend = jnp.cumsum(tiles)
    tstart = tend - tiles
    slot = tstart[experts] * tm + rank
    token = jnp.arange(2 * t, dtype=I32) // 2
    src = jnp.zeros((n_tiles * tm,), I32).at[slot].set(token)
    tile_id = jnp.arange(n_tiles, dtype=I32)
    te = jnp.sum((tile_id[:, None] >= tend[None, :]).astype(I32), axis=1)
    active = tile_id < tend[-1]
    last_e = jnp.sum((tend[-1] - 1 >= tend).astype(I32))
    te = jnp.where(active, te, last_e)
    tv = jnp.where(active, jnp.clip(counts[te] - (tile_id - tstart[te]) * tm, 0, tm), 0)
    slots = slot.reshape(t, 2)
    n_active = (tend[-1] * tm).astype(I32).reshape(1)
    return src, n_active, te.astype(I32), tv.astype(I32), slots[:, 0], slots[:, 1]


def _bias_table(rel_bias):
    h = rel_bias.shape[1]
    return jnp.full((h, LANES), NEG, F32).at[:, :N_BUCKETS].set(rel_bias.T * LOG2E)


def kernel(x, c, positions, rel_bias, w_mod, b_mod, g_attn, g_ffn, a_w_in, a_w_out, a_g_qn, a_g_kn,
           kv_w_mod, kv_b_mod, kv_g, kv_w, b_g_kn, b_w_q, b_w_out, b_g_qn, ffn_w1, ffn_w3, ffn_w2,
           moe_router, moe_router_b, moe_w1, moe_w3, moe_w2):
    batch, seq, d = x.shape
    m = batch * seq
    x2 = x.reshape(m, d)
    positions = positions.astype(I32)

    c8 = jnp.zeros((8, d), F32).at[:batch].set(c)
    mod0 = _mod_call(c8, w_mod, 0, b_mod)[:batch]
    mod1 = _mod_call(c8, w_mod, 1, b_mod)[:batch]
    kvm = _mod_call(c8, kv_w_mod[None], 0, kv_b_mod[None])[:batch]
    sh1_0, sc1_0, gt1_0, sh2_0, sc2_0, gt2_0 = jnp.split(mod0, 6, axis=-1)
    sh1_1, sc1_1, gt1_1, sh2_1, sc2_1, gt2_1 = jnp.split(mod1, 6, axis=-1)
    kv_sh, kv_sc = jnp.split(kvm, 2, axis=-1)

    tab = _bias_table(rel_bias)

    a_main = A_HEADS * A_HEAD_DIM + 2 * A_KV_HEADS * A_HEAD_DIM + IDX_HEADS * IDX_DIM
    n_tail = IDX_DIM + IDX_HEADS
    w_tail = jnp.zeros((d, LANES), F32).at[:, :n_tail].set(a_w_in[0, :, a_main:a_main + n_tail])
    qkv, tail = _nm_matmul(x2, g_attn[0], sc1_0, sh1_0, a_w_in, 0, a_main, seq, w_tail=w_tail)
    attn = _dsa_attention(qkv, tail, positions, tab,
                          a_g_qn[0].reshape(1, -1), a_g_kn[0].reshape(1, -1), batch, seq)
    x2 = _matmul_residual(attn, a_w_out, 0, x2, gt1_0, seq, tn=512)
    u = _nm_swiglu(x2, g_ffn[0], sc2_0, sh2_0, ffn_w1, ffn_w3, 0, seq)
    x2 = _matmul_residual(u, ffn_w2, 0, x2, gt2_0, seq)

    b_q = len(B_DILATIONS) * B_HEADS * B_HEAD_DIM
    dil = tuple(r for _, r in B_DILATIONS)
    gcols = B_HEADS * B_HEAD_DIM
    kvall = _nm_matmul(x2, kv_g, kv_sc, kv_sh, kv_w[None], 0, 2 * b_q, seq,
                       dilations=dil, group_cols=gcols, tn=256)
    qall = _nm_matmul(x2, g_attn[1], sc1_1, sh1_1, b_w_q, 0, b_q, seq,
                      dilations=dil, group_cols=gcols, tn=256)
    gq2 = jnp.tile(b_g_qn[0], 2).reshape(1, LANES)
    gk2 = jnp.tile(b_g_kn, 2).reshape(1, LANES)
    w = B_HEADS * B_HEAD_DIM
    ng = len(B_DILATIONS)
    outs, lses = [], []
    for g, (window, r) in enumerate(B_DILATIONS):
        pos_g = positions.reshape(batch, seq // r, r).transpose(0, 2, 1).reshape(batch, seq)
        o_g, lse_g = _win_attention(qall, g, kvall, g, kvall, ng + g, pos_g, tab, gq2, gk2,
                                    batch, seq, r, window)
        outs.append(o_g)
        lses.append(lse_g)
    x2 = _merge_out(outs, lses, dil, b_w_out, 0, x2, gt1_1, seq)

    h, route = _router(x2, g_ffn[1], sc2_1, sh2_1, moe_router[0], moe_router_b[0], seq)
    share = (2 * m) // N_EXPERTS
    tm = -(-(share * 17 // 32) // 64) * 64
    n_tiles = (2 * m) // tm + N_EXPERTS
    src, n_active, te, tv, p1, p2 = _routing_tables(route, tm, n_tiles)
    hs = _gather_rows(h, src, n_active, tm // 4)
    ys = _moe_experts(hs, te, tv, moe_w1.reshape(moe_w1.shape[1:]), moe_w3.reshape(moe_w3.shape[1:]),
                      moe_w2.reshape(moe_w2.shape[1:]), tm)
    out = _combine(ys, p1, p2, x2, gt2_1, route, seq)
    return out.reshape(batch, seq, d)
```

```python
import functools
import math

import jax
import jax.numpy as jnp
from jax import lax
from jax.experimental import pallas as pl
from jax.experimental.pallas import tpu as pltpu

F32 = jnp.float32
BF16 = jnp.bfloat16
I32 = jnp.int32

EPS = 1e-6
NEG = -1e30
INT_MIN = -(2 ** 31)
LOG2E = 1.0 / math.log(2.0)

A_HEADS, A_KV_HEADS, A_HEAD_DIM = 16, 4, 128
IDX_HEADS, IDX_DIM = 16, 64
TOPK_MAX = 256
B_DILATIONS = ((128, 1), (512, 4), (2048, 16))
B_HEADS, B_HEAD_DIM = 16, 64
N_BUCKETS, MAX_DISTANCE = 32, 2048
N_EXPERTS = 8
LANES = 128

VMEM_LIMIT_BYTES = 56 * 1024 * 1024

_NT = (((1,), (1,)), ((), ()))


def _params(*sem):
    return pltpu.CompilerParams(dimension_semantics=sem, vmem_limit_bytes=VMEM_LIMIT_BYTES)


def _dot(a, b):
    return jnp.dot(a, b, preferred_element_type=F32)


def _dot_nt(a, b):
    return lax.dot_general(a, b, _NT, preferred_element_type=F32)


def _sigmoid(x):
    return 1.0 / (1.0 + jnp.exp(-x))


def _t5_bucket(rel):
    n = jnp.maximum(rel, 0)
    max_exact = N_BUCKETS // 2
    nf = jnp.maximum(n, 1).astype(F32)
    large = max_exact + (jnp.log(nf / max_exact) / math.log(MAX_DISTANCE / max_exact)
                         * (N_BUCKETS - max_exact)).astype(I32)
    large = jnp.minimum(large, N_BUCKETS - 1)
    return jnp.where(n < max_exact, n, large)


def _norm_mod(x, g, sc, sh):
    y = x * lax.rsqrt(jnp.mean(x * x, axis=-1, keepdims=True) + EPS)
    return (y * g) * (1.0 + sc) + sh


def _bias_lookup(tab_ref, heads, bkt):
    rows, cols = bkt.shape
    tabs = [jnp.broadcast_to(tab_ref[h:h + 1, :], (8, LANES)) for h in heads]
    tiles = [[] for _ in tabs]
    for r in range(rows // 8):
        pieces = [[] for _ in tabs]
        for c in range(cols // LANES):
            idx = bkt[r * 8:(r + 1) * 8, c * LANES:(c + 1) * LANES]
            for k, tab in enumerate(tabs):
                pieces[k].append(jnp.take_along_axis(tab, idx, axis=1))
        for k in range(len(tabs)):
            tiles[k].append(pieces[k][0] if len(pieces[k]) == 1 else jnp.concatenate(pieces[k], axis=1))
    return [jnp.concatenate(t, axis=0) for t in tiles]


def _mod_body(c_ref, w_ref, b_ref, o_ref):
    c = c_ref[...]
    cs = c * _sigmoid(c)
    o_ref[...] = _dot(cs.astype(BF16), w_ref[...].astype(BF16)) + b_ref[...]


def _mod_call(c8, w3, layer, b2):
    _, d, n = w3.shape
    tn = 1024
    return pl.pallas_call(
        _mod_body,
        grid=(n // tn,),
        in_specs=[pl.BlockSpec((8, d), lambda j: (0, 0)),
                  pl.BlockSpec((None, d, tn), lambda j: (layer, 0, j)),
                  pl.BlockSpec((None, 1, tn), lambda j: (layer, 0, j))],
        out_specs=pl.BlockSpec((8, tn), lambda j: (0, j)),
        out_shape=jax.ShapeDtypeStruct((8, n), F32),
        compiler_params=_params("arbitrary"),
        name="adaln_mod",
    )(c8, w3, b2.reshape(b2.shape[0], 1, n))


def _nm_body(x_ref, g_ref, sc_ref, sh_ref, w_ref, *rest, has_tail, dilations, group_cols):
    if has_tail:
        wt_ref, o_ref, ot_ref, h_scr = rest
    elif dilations:
        o_ref, h_scr, y_scr = rest
    else:
        o_ref, h_scr = rest

    @pl.when(pl.program_id(1) == 0)
    def _():
        h = _norm_mod(x_ref[...], g_ref[...], sc_ref[0], sh_ref[0]).astype(BF16)
        h_scr[...] = h
        if has_tail:
            ot_ref[...] = _dot(h, wt_ref[...].astype(BF16)).astype(ot_ref.dtype)

    y = _dot(h_scr[...], w_ref[...].astype(BF16))
    if not dilations:
        o_ref[...] = y.astype(o_ref.dtype)
        return
    tm, tn = o_ref.shape
    group = (pl.program_id(1) * tn // group_cols) % len(dilations)
    for k, r in enumerate(dilations):
        @pl.when(group == k)
        def _(r=r):
            if r == 1:
                o_ref[...] = y.astype(o_ref.dtype)
                return
            for c in range(tn // LANES):
                y_scr[c] = y[:, c * LANES:(c + 1) * LANES]
            cls = tm // r
            for rho in range(r):
                for c in range(tn // LANES):
                    o_ref[rho * cls:(rho + 1) * cls, c * LANES:(c + 1) * LANES] = (
                        y_scr[c, pl.ds(rho, cls, stride=r), :].astype(o_ref.dtype))


def _nm_matmul(x, g, sc, sh, w3, layer, n_cols, seq, w_tail=None, dilations=None, group_cols=None,
               tm=2048, tn=512):
    m, d = x.shape
    tm = min(tm, seq)
    nb = seq // tm
    has_tail = w_tail is not None
    assert not dilations or (tm == seq and not has_tail and group_cols % tn == 0)
    in_specs = [pl.BlockSpec((tm, d), lambda i, j: (i, 0), pipeline_mode=pl.Buffered(1)),
                pl.BlockSpec((1, d), lambda i, j: (0, 0)),
                pl.BlockSpec((1, 1, d), lambda i, j: (i // nb, 0, 0)),
                pl.BlockSpec((1, 1, d), lambda i, j: (i // nb, 0, 0)),
                pl.BlockSpec((None, d, tn), lambda i, j: (layer, 0, j))]
    args = [x, g.reshape(1, d), sc.reshape(-1, 1, d), sh.reshape(-1, 1, d), w3]
    out_specs = [pl.BlockSpec((tm, tn), lambda i, j: (i, j))]
    out_shape = [jax.ShapeDtypeStruct((m, n_cols), BF16)]
    if has_tail:
        in_specs.append(pl.BlockSpec((d, LANES), lambda i, j: (0, 0)))
        args.append(w_tail)
        out_specs.append(pl.BlockSpec((tm, LANES), lambda i, j: (i, 0)))
        out_shape.append(jax.ShapeDtypeStruct((m, LANES), BF16))
    scratch = [pltpu.VMEM((tm, d), BF16)]
    if dilations:
        scratch.append(pltpu.VMEM((tn // LANES, tm, LANES), F32))
    res = pl.pallas_call(
        functools.partial(_nm_body, has_tail=has_tail, dilations=dilations, group_cols=group_cols),
        grid=(m // tm, n_cols // tn),
        in_specs=in_specs,
        out_specs=out_specs,
        out_shape=out_shape,
        scratch_shapes=scratch,
        compiler_params=_params("arbitrary", "arbitrary"),
        name="norm_mod_matmul",
    )(*args)
    return res if has_tail else res[0]


def _mmres_body(a_ref, w_ref, x_ref, gt_ref, o_ref):
    o_ref[...] = x_ref[...] + gt_ref[0] * _dot(a_ref[...], w_ref[...].astype(BF16))


def _matmul_residual(a, w3, layer, x, gt, seq, tm=2048, tn=256):
    m, k = a.shape
    d = x.shape[1]
    tm = min(tm, seq)
    nb = seq // tm
    return pl.pallas_call(
        _mmres_body,
        grid=(m // tm, d // tn),
        in_specs=[pl.BlockSpec((tm, k), lambda i, j: (i, 0), pipeline_mode=pl.Buffered(1)),
                  pl.BlockSpec((None, k, tn), lambda i, j: (layer, 0, j)),
                  pl.BlockSpec((tm, tn), lambda i, j: (i, j)),
                  pl.BlockSpec((1, 1, tn), lambda i, j: (i // nb, 0, j))],
        out_specs=pl.BlockSpec((tm, tn), lambda i, j: (i, j)),
        out_shape=jax.ShapeDtypeStruct((m, d), F32),
        compiler_params=_params("arbitrary", "arbitrary"),
        name="matmul_residual",
    )(a, w3, x, gt.reshape(-1, 1, d))


def _nm_swiglu_body(x_ref, g_ref, sc_ref, sh_ref, w1_ref, w3_ref, o_ref, h_scr):
    @pl.when(pl.program_id(1) == 0)
    def _():
        h_scr[...] = _norm_mod(x_ref[...], g_ref[...], sc_ref[0], sh_ref[0]).astype(BF16)

    h = h_scr[...]
    a = _dot(h, w1_ref[...].astype(BF16))
    b = _dot(h, w3_ref[...].astype(BF16))
    o_ref[...] = (a * _sigmoid(a) * b).astype(o_ref.dtype)


def _nm_swiglu(x, g, sc, sh, w1, w3, layer, seq, tm=2048, tf=256):
    m, d = x.shape
    f = w1.shape[2]
    tm = min(tm, seq)
    nb = seq // tm
    return pl.pallas_call(
        _nm_swiglu_body,
        grid=(m // tm, f // tf),
        in_specs=[pl.BlockSpec((tm, d), lambda i, j: (i, 0), pipeline_mode=pl.Buffered(1)),
                  pl.BlockSpec((1, d), lambda i, j: (0, 0)),
                  pl.BlockSpec((1, 1, d), lambda i, j: (i // nb, 0, 0)),
                  pl.BlockSpec((1, 1, d), lambda i, j: (i // nb, 0, 0)),
                  pl.BlockSpec((None, d, tf), lambda i, j: (layer, 0, j)),
                  pl.BlockSpec((None, d, tf), lambda i, j: (layer, 0, j))],
        out_specs=pl.BlockSpec((tm, tf), lambda i, j: (i, j)),
        out_shape=jax.ShapeDtypeStruct((m, f), BF16),
        scratch_shapes=[pltpu.VMEM((tm, d), BF16)],
        compiler_params=_params("arbitrary", "arbitrary"),
        name="norm_mod_swiglu_up",
    )(x, g.reshape(1, d), sc.reshape(-1, 1, d), sh.reshape(-1, 1, d), w1, w3)


def _dsa_body(q_ref, qi_ref, k_ref, v_ref, tq_ref, tk_ref, pr_ref, pc_ref, tab_ref, gq_ref, gk_ref,
              o_ref,
              kn_scr, vt_scr, kke_scr, kko_scr, qst_scr, wib_scr, key_scr, qn_scr, m_scr, l_scr, acc_scr,
              s_buf, cm_buf,
              *, topk, tq, seq):
    i = pl.program_id(1)
    nc = i + 1
    grp = A_HEADS // A_KV_HEADS
    hd = A_HEAD_DIM
    lane = lax.broadcasted_iota(I32, (1, LANES), 1)
    krow = lax.broadcasted_iota(I32, (tq, tq), 0)
    qcol = lax.broadcasted_iota(I32, (tq, tq), 1)

    @pl.when(i == 0)
    def _prepare_keys():
        gk = gk_ref[...]

        def body(r, carry):
            rows = pl.ds(pl.multiple_of(r * tq, tq), tq)
            for kh in range(A_KV_HEADS):
                cols = slice(kh * hd, (kh + 1) * hd)
                kb = k_ref[rows, cols].astype(F32)
                ms = jnp.mean(kb * kb, axis=-1, keepdims=True)
                kn_scr[rows, cols] = ((kb * lax.rsqrt(ms + EPS)) * gk).astype(BF16)
                vt_scr[r, cols, :] = v_ref[rows, cols].astype(F32).T.astype(BF16)
            t = tk_ref[rows, :].astype(F32)
            kke_scr[rows, :] = jnp.where(lane < IDX_DIM, t, 0.0).astype(BF16)
            kko_scr[rows, :] = jnp.where(lane >= IDX_DIM, pltpu.roll(t, IDX_DIM, 1), 0.0).astype(BF16)
            return carry

        lax.fori_loop(0, seq // tq, body, 0)

    for j in range(IDX_HEADS // 2):
        qst_scr[j * tq:(j + 1) * tq, :] = qi_ref[:, j * LANES:(j + 1) * LANES]
    w_scale = (IDX_DIM ** -0.5) * (IDX_HEADS ** -0.5)
    wib_scr[...] = tq_ref[...].astype(F32).T[IDX_DIM:IDX_DIM + IDX_HEADS, :] * w_scale

    def idx_body(c, carry):
        rows = pl.ds(pl.multiple_of(c * tq, tq), tq)
        qst = qst_scr[...]
        re = _dot_nt(kke_scr[rows, :], qst)
        ro = _dot_nt(kko_scr[rows, :], qst)
        acc = jnp.zeros((tq, tq), F32)
        for j in range(IDX_HEADS // 2):
            acc = acc + jnp.maximum(re[:, j * tq:(j + 1) * tq], 0.0) * wib_scr[2 * j:2 * j + 1, :]
            acc = acc + jnp.maximum(ro[:, j * tq:(j + 1) * tq], 0.0) * wib_scr[2 * j + 1:2 * j + 2, :]
        bits = pltpu.bitcast(acc, I32)
        key = bits ^ ((bits >> 31) & 0x7FFFFFFF)
        causal = (c < i) | (krow <= qcol)
        key_scr[c] = jnp.where(causal, key, INT_MIN)
        return carry

    lax.fori_loop(0, nc, idx_body, 0)

    def bit_body(bi, t_u):
        cand_u = t_u | lax.shift_left(jnp.int32(1), 31 - bi)
        cand_s = cand_u ^ INT_MIN

        def cnt_body(c, cnt):
            ge = jnp.where(key_scr[c] >= cand_s, 1, 0)
            return cnt + jnp.sum(ge.reshape(tq // 8, 8, tq), axis=0)

        cnt = lax.fori_loop(0, nc, cnt_body, jnp.zeros((8, tq), I32))
        total = jnp.sum(cnt, axis=0, keepdims=True)
        return jnp.where(total >= topk, cand_u, t_u)

    nbits = jnp.where(nc * tq > topk, 32, 0)
    t_u = lax.fori_loop(0, nbits, bit_body, jnp.zeros((1, tq), I32))
    thr = t_u ^ INT_MIN

    qpos = pr_ref[0, i]

    def bkt_body(c, carry):
        rows = pl.ds(pl.multiple_of(c * tq, tq), tq)
        bkt = _t5_bucket(qpos - pc_ref[rows, :])
        causal = (c < i) | (krow <= qcol)
        sel = (key_scr[c] >= thr) & causal
        key_scr[c] = jnp.where(sel, bkt, N_BUCKETS)
        return carry

    lax.fori_loop(0, nc, bkt_body, 0)

    gq = gq_ref[...]
    scale = (hd ** -0.5) * LOG2E
    for g in range(A_KV_HEADS):
        for hh in range(grp):
            h = g * grp + hh
            qh = q_ref[:, h * hd:(h + 1) * hd].astype(F32)
            ms = jnp.mean(qh * qh, axis=-1, keepdims=True)
            qn_scr[hh * tq:(hh + 1) * tq, :] = (((qh * lax.rsqrt(ms + EPS)) * gq) * scale).astype(BF16)
        m_scr[...] = jnp.full(m_scr.shape, NEG, F32)
        l_scr[...] = jnp.zeros(l_scr.shape, F32)
        acc_scr[...] = jnp.zeros(acc_scr.shape, F32)

        def scores(c, g=g):
            rows = pl.ds(pl.multiple_of(c * tq, tq), tq)
            s = _dot_nt(kn_scr[rows, g * hd:(g + 1) * hd], qn_scr[...])
            bias = _bias_lookup(tab_ref, range(g * grp, (g + 1) * grp), key_scr[c])
            s = jnp.concatenate([s[:, hh * tq:(hh + 1) * tq] + bias[hh] for hh in range(grp)], axis=1)
            s_buf[c % 2] = s
            cm_buf[c % 2] = jnp.max(s, axis=0, keepdims=True)

        def accumulate(c, g=g):
            s = s_buf[c % 2]
            m_old = m_scr[...]
            m_new = jnp.maximum(m_old, cm_buf[c % 2])
            p = jnp.exp2(s - m_new)
            alpha = jnp.exp2(m_old - m_new)
            l_scr[...] = alpha * l_scr[...] + jnp.sum(p, axis=0, keepdims=True)
            acc_scr[...] = alpha * acc_scr[...] + _dot(vt_scr[c, g * hd:(g + 1) * hd, :], p.astype(BF16))
            m_scr[...] = m_new

        def att_body(c, carry):
            accumulate(c)
            scores(c + 1)
            return carry

        scores(0)
        lax.fori_loop(0, nc - 1, att_body, 0)
        accumulate(nc - 1)
        o = acc_scr[...] * (1.0 / l_scr[...])
        for hh in range(grp):
            h = g * grp + hh
            o_ref[:, h * hd:(h + 1) * hd] = o[:, hh * tq:(hh + 1) * tq].T.astype(o_ref.dtype)


def _dsa_attention(qkv, tail, pos, tab, gq, gk, batch, seq, tq=256):
    tq = min(tq, seq)
    nq = seq // tq
    a_q = A_HEADS * A_HEAD_DIM
    a_kv = A_KV_HEADS * A_HEAD_DIM
    a_qi = IDX_HEADS * IDX_DIM
    topk = min(TOPK_MAX, seq // 4)
    grp = A_HEADS // A_KV_HEADS
    body = functools.partial(_dsa_body, topk=topk, tq=tq, seq=seq)
    return pl.pallas_call(
        body,
        grid=(batch, nq),
        in_specs=[
            pl.BlockSpec((tq, a_q), lambda b, i: (b * nq + i, 0)),
            pl.BlockSpec((tq, a_qi), lambda b, i: (b * nq + i, (a_q + 2 * a_kv) // a_qi)),
            pl.BlockSpec((seq, a_kv), lambda b, i: (b, a_q // a_kv)),
            pl.BlockSpec((seq, a_kv), lambda b, i: (b, a_q // a_kv + 1)),
            pl.BlockSpec((tq, LANES), lambda b, i: (b * nq + i, 0)),
            pl.BlockSpec((seq, LANES), lambda b, i: (b, 0)),
            pl.BlockSpec((1, nq, 1, tq), lambda b, i: (b, 0, 0, 0)),
            pl.BlockSpec((seq, 1), lambda b, i: (b, 0)),
            pl.BlockSpec((A_HEADS, LANES), lambda b, i: (0, 0)),
            pl.BlockSpec((1, A_HEAD_DIM), lambda b, i: (0, 0)),
            pl.BlockSpec((1, A_HEAD_DIM), lambda b, i: (0, 0)),
        ],
        out_specs=pl.BlockSpec((tq, a_q), lambda b, i: (b * nq + i, 0)),
        out_shape=jax.ShapeDtypeStruct((batch * seq, a_q), BF16),
        scratch_shapes=[
            pltpu.VMEM((seq, a_kv), BF16),
            pltpu.VMEM((nq, a_kv, tq), BF16),
            pltpu.VMEM((seq, LANES), BF16),
            pltpu.VMEM((seq, LANES), BF16),
            pltpu.VMEM((IDX_HEADS // 2 * tq, LANES), BF16),
            pltpu.VMEM((IDX_HEADS, tq), F32),
            pltpu.VMEM((nq, tq, tq), I32),
            pltpu.VMEM((grp * tq, A_HEAD_DIM), BF16),
            pltpu.VMEM((1, grp * tq), F32),
            pltpu.VMEM((1, grp * tq), F32),
            pltpu.VMEM((A_HEAD_DIM, grp * tq), F32),
            pltpu.VMEM((2, tq, grp * tq), F32),
            pltpu.VMEM((2, 1, grp * tq), F32),
        ],
        compiler_params=_params("arbitrary", "arbitrary"),
        name="dsa_attention",
    )(qkv, qkv, qkv, qkv, tail, tail, pos.reshape(batch, nq, 1, tq), pos.reshape(batch * seq, 1),
      tab, gq, gk)


def _win_body(*refs, cls_len, wk, use_prev, tq):
    if use_prev:
        (q_ref, kc_ref, vc_ref, kp_ref, vp_ref, pr_ref, pcc_ref, pcp_ref, tab_ref, gq_ref, gk_ref,
         o_ref, lse_ref, kn_scr, vt_scr, qn_scr, bkt_scr, bias_scr, lse_scr) = refs
    else:
        (q_ref, kc_ref, vc_ref, pr_ref, pcc_ref, tab_ref, gq_ref, gk_ref,
         o_ref, lse_ref, kn_scr, vt_scr, qn_scr, bkt_scr, bias_scr, lse_scr) = refs
    t = pl.program_id(1)
    hd = B_HEAD_DIM
    pw = 2 * hd
    blk = LANES
    shift = cls_len.bit_length() - 1
    koff = blk if use_prev else 0
    win = 2 * blk if cls_len > blk else blk
    lo = lax.broadcasted_iota(I32, (1, pw), 1) < hd
    lo_rows = lax.broadcasted_iota(I32, (pw, 1), 0) < hd
    gq = gq_ref[...]
    gk = gk_ref[...]
    scale = (hd ** -0.5) * LOG2E

    def pair_norm(x, g):
        sq = x * x
        ms_lo = jnp.sum(jnp.where(lo, sq, 0.0), axis=-1, keepdims=True) * (1.0 / hd)
        ms_hi = jnp.sum(jnp.where(lo, 0.0, sq), axis=-1, keepdims=True) * (1.0 / hd)
        inv = jnp.where(lo, lax.rsqrt(ms_lo + EPS), lax.rsqrt(ms_hi + EPS))
        return (x * inv) * g

    for p in range(B_HEADS // 2):
        cols = slice(p * pw, (p + 1) * pw)
        qn_scr[:, cols] = (pair_norm(q_ref[:, cols].astype(F32), gq) * scale).astype(BF16)
        if use_prev:
            kn_scr[0:blk, cols] = pair_norm(kp_ref[:, cols].astype(F32), gk).astype(BF16)
            vt_scr[p, :, 0:blk] = vp_ref[:, cols].astype(F32).T.astype(BF16)
        kn_scr[koff:koff + tq, cols] = pair_norm(kc_ref[:, cols].astype(F32), gk).astype(BF16)
        vt_scr[p, :, koff:koff + tq] = vc_ref[:, cols].astype(F32).T.astype(BF16)

    lse_scr[...] = jnp.zeros(lse_scr.shape, F32)
    qpos_all = pr_ref[0]
    for j in range(tq // blk):
        qrows = slice(j * blk, (j + 1) * blk)
        own = koff + j * blk
        k0 = own - blk if (win > blk and own >= blk) else own
        if use_prev and k0 < koff:
            kpos = jnp.concatenate([pcp_ref[k0:koff, :], pcc_ref[0:k0 + win - koff, :]], axis=0)
        else:
            kpos = pcc_ref[k0 - koff:k0 - koff + win, :]
        fq = t * tq + j * blk + lax.broadcasted_iota(I32, (1, blk), 1)
        fk = t * tq + (k0 - koff) + lax.broadcasted_iota(I32, (win, 1), 0)
        same = ((fq + cls_len) >> shift) == ((fk + cls_len) >> shift)
        dist = (fq & (cls_len - 1)) - (fk & (cls_len - 1))
        ok = same & (dist >= 0) & (dist <= wk)
        bkt_scr[...] = jnp.where(ok, _t5_bucket(qpos_all[:, qrows] - kpos), N_BUCKETS)

        def bias_rows(r8, carry):
            rows = pl.ds(pl.multiple_of(r8 * 8, 8), 8)
            idx = bkt_scr[rows, :]
            for h in range(B_HEADS):
                tab = jnp.broadcast_to(tab_ref[h:h + 1, :], (8, LANES))
                bias_scr[h // 2, rows, (h % 2) * blk:(h % 2 + 1) * blk] = jnp.take_along_axis(tab, idx, axis=1)
            return carry

        lax.fori_loop(0, win // 8, bias_rows, 0, unroll=8)

        for p in range(B_HEADS // 2):
            cols = slice(p * pw, (p + 1) * pw)
            qp = qn_scr[qrows, cols]
            zero = jnp.zeros_like(qp)
            heads = (jnp.where(lo, qp, zero), jnp.where(lo, zero, qp))
            kw = kn_scr[k0:k0 + win, cols]
            vtw = vt_scr[p, :, k0:k0 + win]
            hp = 2 * blk // win
            outs = []
            for c in range(2 // hp):
                q2 = heads[c] if hp == 1 else jnp.concatenate(heads, axis=0)
                s = _dot_nt(kw, q2) + bias_scr[p, :, c * hp * blk:(c + 1) * hp * blk]
                m = jnp.max(s, axis=0, keepdims=True)
                e = jnp.exp2(s - m)
                l = jnp.sum(e, axis=0, keepdims=True)
                ot = _dot(vtw, e.astype(BF16)) * (1.0 / l)
                lse = m + jnp.log(l) * LOG2E
                for k in range(hp):
                    h = 2 * p + c * hp + k
                    outs.append(ot[:, k * blk:(k + 1) * blk])
                    lse_scr[h:h + 1, :] = lse[:, k * blk:(k + 1) * blk]
            o_ref[qrows, cols] = jnp.where(lo_rows, outs[0], outs[1]).T.astype(o_ref.dtype)
        lse_ref[qrows, :] = lse_scr[...].T


def _win_attention(q_arr, q_col, k_arr, k_col, v_arr, v_col, pos, tab, gq2, gk2,
                   batch, seq, dilation, window, tq=512):
    cls_len = seq // dilation
    wk = window // dilation
    assert cls_len & (cls_len - 1) == 0 and wk <= LANES
    tq = min(tq, seq)
    use_prev = cls_len > tq
    assert use_prev or tq % cls_len == 0
    nt = seq // tq
    w = B_HEADS * B_HEAD_DIM
    sub = tq // LANES
    pos_col = pos.reshape(batch * seq, 1)
    pos_row = pos.reshape(batch * nt, 1, tq)
    prev = lambda b, t: jnp.maximum((b * nt + t) * sub - 1, 0)
    in_specs = [pl.BlockSpec((tq, w), lambda b, t: (b * nt + t, q_col)),
                pl.BlockSpec((tq, w), lambda b, t: (b * nt + t, k_col)),
                pl.BlockSpec((tq, w), lambda b, t: (b * nt + t, v_col))]
    args = [q_arr, k_arr, v_arr]
    if use_prev:
        in_specs += [pl.BlockSpec((LANES, w), lambda b, t: (prev(b, t), k_col)),
                     pl.BlockSpec((LANES, w), lambda b, t: (prev(b, t), v_col))]
        args += [k_arr, v_arr]
    in_specs += [pl.BlockSpec((1, 1, tq), lambda b, t: (b * nt + t, 0, 0)),
                 pl.BlockSpec((tq, 1), lambda b, t: (b * nt + t, 0))]
    args += [pos_row, pos_col]
    if use_prev:
        in_specs.append(pl.BlockSpec((LANES, 1), lambda b, t: (prev(b, t), 0)))
        args.append(pos_col)
    in_specs += [pl.BlockSpec((B_HEADS, LANES), lambda b, t: (0, 0)),
                 pl.BlockSpec((1, LANES), lambda b, t: (0, 0)),
                 pl.BlockSpec((1, LANES), lambda b, t: (0, 0))]
    args += [tab, gq2, gk2]
    nk = tq + (LANES if use_prev else 0)
    win = 2 * LANES if cls_len > LANES else LANES
    body = functools.partial(_win_body, cls_len=cls_len, wk=wk, use_prev=use_prev, tq=tq)
    return pl.pallas_call(
        body,
        grid=(batch, nt),
        in_specs=in_specs,
        out_specs=[pl.BlockSpec((tq, w), lambda b, t: (b * nt + t, 0)),
                   pl.BlockSpec((tq, LANES), lambda b, t: (b * nt + t, 0))],
        out_shape=[jax.ShapeDtypeStruct((batch * seq, w), BF16),
                   jax.ShapeDtypeStruct((batch * seq, LANES), F32)],
        scratch_shapes=[pltpu.VMEM((nk, w), BF16),
                        pltpu.VMEM((B_HEADS // 2, LANES, nk), BF16),
                        pltpu.VMEM((tq, w), BF16),
                        pltpu.VMEM((win, LANES), I32),
                        pltpu.VMEM((B_HEADS // 2, win, 2 * LANES), F32),
                        pltpu.VMEM((LANES, LANES), F32)],
        compiler_params=_params("arbitrary", "arbitrary"),
        name="dilated_attention",
    )(*args)


def _merge_body(*refs, dilations):
    ng = len(dilations)
    o_refs, l_refs = refs[:ng], refs[ng:2 * ng]
    e_ref, w_ref, x_ref, gt_ref, out_ref, a_scr, o_scr, l_scr = refs[2 * ng:]
    tm, k = a_scr.shape

    @pl.when(pl.program_id(1) == 0)
    def _():
        for g, r in enumerate(dilations):
            cls = tm // r
            for rho in range(r):
                l_scr[g, pl.ds(rho, cls, stride=r), :] = l_refs[g][0, rho]
                for c in range(k // LANES):
                    o_scr[g, c, pl.ds(rho, cls, stride=r), :] = (
                        o_refs[g][0, rho, :, c * LANES:(c + 1) * LANES].astype(F32))
        ls = [l_scr[g] for g in range(ng)]
        m = functools.reduce(jnp.maximum, ls)
        ws = [jnp.exp2(l - m) for l in ls]
        inv = 1.0 / functools.reduce(lambda a, b: a + b, ws)
        wides = []
        for wg in ws:
            wn = wg * inv
            hi = wn.astype(BF16)
            lo = (wn - hi.astype(F32)).astype(BF16)
            wides.append(_dot(hi, e_ref[...]) + _dot(lo, e_ref[...]))
        for c in range(k // LANES):
            cols = slice(c * LANES, (c + 1) * LANES)
            num = wides[0][:, cols] * o_scr[0, c]
            for g in range(1, ng):
                num = num + wides[g][:, cols] * o_scr[g, c]
            a_scr[:, cols] = num.astype(BF16)

    out_ref[...] = x_ref[...] + gt_ref[0] * _dot(a_scr[...], w_ref[...].astype(BF16))


def _merge_out(os_, ls_, dilations, w3, layer, x, gt, seq, tm=512, tn=512):
    m, k = os_[0].shape
    d = x.shape[1]
    nb = seq // tm
    batch = m // seq
    head = jnp.arange(k, dtype=I32)[None, :] // B_HEAD_DIM
    expand = (jnp.arange(LANES, dtype=I32)[:, None] == head).astype(BF16)
    ng = len(dilations)
    o_specs = [pl.BlockSpec((1, r, tm // r, k), lambda i, j: (i // nb, 0, i % nb, 0)) for r in dilations]
    l_specs = [pl.BlockSpec((1, r, tm // r, LANES), lambda i, j: (i // nb, 0, i % nb, 0)) for r in dilations]
    o_args = [o.reshape(batch, r, seq // r, k) for o, r in zip(os_, dilations)]
    l_args = [l.reshape(batch, r, seq // r, LANES) for l, r in zip(ls_, dilations)]
    return pl.pallas_call(
        functools.partial(_merge_body, dilations=tuple(dilations)),
        grid=(m // tm, d // tn),
        in_specs=o_specs + l_specs + [
            pl.BlockSpec((LANES, k), lambda i, j: (0, 0)),
            pl.BlockSpec((None, k, tn), lambda i, j: (layer, 0, j)),
            pl.BlockSpec((tm, tn), lambda i, j: (i, j)),
            pl.BlockSpec((1, 1, tn), lambda i, j: (i // nb, 0, j))],
        out_specs=pl.BlockSpec((tm, tn), lambda i, j: (i, j)),
        out_shape=jax.ShapeDtypeStruct((m, d), F32),
        scratch_shapes=[pltpu.VMEM((tm, k), BF16),
                        pltpu.VMEM((ng, k // LANES, tm, LANES), F32),
                        pltpu.VMEM((ng, tm, LANES), F32)],
        compiler_params=_params("arbitrary", "arbitrary"),
        name="merge_out_proj",
    )(*o_args, *l_args, expand, w3, x, gt.reshape(-1, 1, d))


def _router_body(x_ref, g_ref, sc_ref, sh_ref, rh_ref, rl_ref, rb_ref, h_ref, rt_ref, cnt_ref, cnt_scr):
    h = _norm_mod(x_ref[...], g_ref[...], sc_ref[0], sh_ref[0])
    hh = h.astype(BF16)
    bits = pltpu.bitcast(hh.astype(F32), I32)
    half = h.shape[1] // 2
    h_ref[...] = lax.shift_right_logical(bits[:, :half], 16) | (bits[:, half:] & -65536)
    hl = (h - hh.astype(F32)).astype(BF16)
    logits = _dot(hh, rh_ref[...]) + _dot(hl, rh_ref[...]) + _dot(hh, rl_ref[...]) + rb_ref[...]
    lane = lax.broadcasted_iota(I32, logits.shape, 1)
    logits = jnp.where(lane < N_EXPERTS, logits, NEG)
    v1 = jnp.max(logits, axis=1, keepdims=True)
    i1 = jnp.min(jnp.where(logits == v1, lane, LANES), axis=1, keepdims=True)
    rest = jnp.where(lane == i1, NEG, logits)
    v2 = jnp.max(rest, axis=1, keepdims=True)
    i2 = jnp.min(jnp.where(rest == v2, lane, LANES), axis=1, keepdims=True)
    e = jnp.exp(v2 - v1)
    g1 = 1.0 / (1.0 + e)
    g2 = e * g1

    @pl.when(pl.program_id(0) == 0)
    def _():
        cnt_scr[...] = jnp.zeros(cnt_scr.shape, F32)

    tm = logits.shape[0]
    pick1 = lane == i1
    pick2 = lane == i2
    chosen = jnp.where(pick1 | pick2, 1.0, 0.0)
    earlier = (lax.broadcasted_iota(I32, (tm, tm), 1) < lax.broadcasted_iota(I32, (tm, tm), 0))
    before = _dot(jnp.where(earlier, 1.0, 0.0).astype(BF16), chosen.astype(BF16)) + cnt_scr[...]
    r1 = jnp.sum(jnp.where(pick1, before, 0.0), axis=1, keepdims=True)
    r2 = jnp.sum(jnp.where(pick2, before, 0.0), axis=1, keepdims=True)
    cnt_scr[...] = cnt_scr[...] + jnp.sum(chosen, axis=0, keepdims=True)
    cnt_ref[...] = cnt_scr[...]

    vals = (i1.astype(F32), i2.astype(F32), g1, g2, r1, r2)
    out = jnp.zeros(logits.shape, F32)
    for k, v in enumerate(vals):
        out = jnp.where(lane == k, v, out)
    rt_ref[...] = out


def _router(x, g, sc, sh, rw, rb, seq, tm=512):
    m, d = x.shape
    nb = seq // tm
    ne = rw.shape[1]
    rw_p = jnp.zeros((d, LANES), F32).at[:, :ne].set(rw)
    rh = rw_p.astype(BF16)
    rl = (rw_p - rh.astype(F32)).astype(BF16)
    rb_p = jnp.zeros((1, LANES), F32).at[0, :ne].set(rb)
    return pl.pallas_call(
        _router_body,
        grid=(m // tm,),
        in_specs=[pl.BlockSpec((tm, d), lambda i: (i, 0)),
                  pl.BlockSpec((1, d), lambda i: (0, 0)),
                  pl.BlockSpec((1, 1, d), lambda i: (i // nb, 0, 0)),
                  pl.BlockSpec((1, 1, d), lambda i: (i // nb, 0, 0)),
                  pl.BlockSpec((d, LANES), lambda i: (0, 0)),
                  pl.BlockSpec((d, LANES), lambda i: (0, 0)),
                  pl.BlockSpec((1, LANES), lambda i: (0, 0))],
        out_specs=[pl.BlockSpec((tm, d // 2), lambda i: (i, 0)),
                   pl.BlockSpec((tm, LANES), lambda i: (i, 0)),
                   pl.BlockSpec((1, LANES), lambda i: (0, 0))],
        out_shape=[jax.ShapeDtypeStruct((m, d // 2), I32),
                   jax.ShapeDtypeStruct((m, LANES), F32),
                   jax.ShapeDtypeStruct((1, LANES), F32)],
        scratch_shapes=[pltpu.VMEM((1, LANES), F32)],
        compiler_params=_params("arbitrary"),
        name="router_top2",
    )(x, g.reshape(1, d), sc.reshape(-1, 1, d), sh.reshape(-1, 1, d), rh, rl, rb_p)


def _row_copy(src_hbm, idx, buf, r, sem):
    return pltpu.make_async_copy(src_hbm.at[pl.ds(idx, 1), :], buf.at[pl.ds(r, 1), :], sem)


def _gather_body(src_ref, nact_ref, h_hbm, o_ref, buf, sem, *, rows):
    i = pl.program_id(0)
    n_active = nact_ref[0]

    def issue(step):
        slot = step % 2

        def body(r, carry):
            _row_copy(h_hbm, src_ref[step * rows + r], buf.at[slot], r, sem.at[slot]).start()
            return carry

        lax.fori_loop(0, rows, body, 0, unroll=8)

    @pl.when(i == 0)
    def _():
        issue(i)

    @pl.when((i + 1) * rows < n_active)
    def _():
        issue(i + 1)

    slot = i % 2

    @pl.when(i * rows < n_active)
    def _():
        pltpu.make_async_copy(h_hbm.at[pl.ds(0, rows), :], buf.at[slot], sem.at[slot]).wait()
        o_ref[...] = buf[slot]

    @pl.when(i * rows >= n_active)
    def _():
        o_ref[...] = jnp.zeros(o_ref.shape, o_ref.dtype)


def _gather_rows(h, src, n_active, rows):
    r_total = src.shape[0]
    w = h.shape[1]
    return pl.pallas_call(
        functools.partial(_gather_body, rows=rows),
        grid_spec=pltpu.PrefetchScalarGridSpec(
            num_scalar_prefetch=2,
            grid=(r_total // rows,),
            in_specs=[pl.BlockSpec(memory_space=pl.ANY)],
            out_specs=pl.BlockSpec((rows, w), lambda i, s, n: (i, 0)),
            scratch_shapes=[pltpu.VMEM((2, rows, w), h.dtype), pltpu.SemaphoreType.DMA((2,))],
        ),
        out_shape=jax.ShapeDtypeStruct((r_total, w), h.dtype),
        compiler_params=_params("arbitrary"),
        name="moe_dispatch_gather",
    )(src, n_active, h)


def _moe_body(te_ref, tv_ref, hs_ref, w1_ref, w3_ref, w2_ref, o_ref, h_scr, *, ncol):
    t = pl.program_id(0)
    f = pl.program_id(1)
    valid = tv_ref[t]
    d = o_ref.shape[1]
    cw = d // ncol

    @pl.when((valid > 0) & (f == 0))
    def _():
        words = hs_ref[...]
        h_scr[:, :d // 2] = pltpu.bitcast(words << 16, F32).astype(BF16)
        h_scr[:, d // 2:] = pltpu.bitcast(words & -65536, F32).astype(BF16)

    @pl.when(valid > 0)
    def _():
        h = h_scr[...]
        a = _dot(h, w1_ref[...].astype(BF16))
        b = _dot(h, w3_ref[...].astype(BF16))
        u = (a * _sigmoid(a) * b).astype(BF16)
        w2 = w2_ref[...].astype(BF16)

        @pl.when(f == 0)
        def _():
            for n in range(ncol):
                o_ref[:, n * cw:(n + 1) * cw] = _dot(u, w2[:, n * cw:(n + 1) * cw])

        @pl.when(f > 0)
        def _():
            for n in range(ncol):
                o_ref[:, n * cw:(n + 1) * cw] += _dot(u, w2[:, n * cw:(n + 1) * cw])

    @pl.when((valid == 0) & (f == 0))
    def _():
        o_ref[...] = jnp.zeros(o_ref.shape, o_ref.dtype)


def _moe_experts(hs, tile_expert, tile_valid, w1, w3, w2, tm, tf=256, ncol=4):
    r_total = hs.shape[0]
    d = w1.shape[1]
    n_tiles = r_total // tm
    nf = w1.shape[2] // tf

    def w13_map(t, f, te, tv):
        return (te[t], 0, jnp.where(tv[t] > 0, f, nf - 1))

    def w2_map(t, f, te, tv):
        return (te[t], jnp.where(tv[t] > 0, f, nf - 1), 0)

    return pl.pallas_call(
        functools.partial(_moe_body, ncol=ncol),
        grid_spec=pltpu.PrefetchScalarGridSpec(
            num_scalar_prefetch=2,
            grid=(n_tiles, nf),
            in_specs=[pl.BlockSpec((tm, d // 2), lambda t, f, te, tv: (t, 0)),
                      pl.BlockSpec((None, d, tf), w13_map),
                      pl.BlockSpec((None, d, tf), w13_map),
                      pl.BlockSpec((None, tf, d), w2_map)],
            out_specs=pl.BlockSpec((tm, d), lambda t, f, te, tv: (t, 0)),
            scratch_shapes=[pltpu.VMEM((tm, d), BF16)],
        ),
        out_shape=jax.ShapeDtypeStruct((r_total, d), F32),
        compiler_params=_params("arbitrary", "arbitrary"),
        name="moe_experts",
    )(tile_expert, tile_valid, hs, w1, w3, w2)


def _combine_body(p1_ref, p2_ref, y_hbm, x_ref, gt_ref, rt_ref, o_ref, buf_a, buf_b, sem, *, rows):
    base = pl.program_id(0) * rows

    def issue(r, carry):
        _row_copy(y_hbm, p1_ref[base + r], buf_a, r, sem).start()
        _row_copy(y_hbm, p2_ref[base + r], buf_b, r, sem).start()
        return carry

    lax.fori_loop(0, rows, issue, 0, unroll=4)
    pltpu.make_async_copy(y_hbm.at[pl.ds(0, rows), :], buf_a, sem).wait()
    pltpu.make_async_copy(y_hbm.at[pl.ds(0, rows), :], buf_b, sem).wait()
    rt = rt_ref[...]
    g1 = rt[:, 2:3]
    g2 = rt[:, 3:4]
    o_ref[...] = x_ref[...] + gt_ref[0] * (g1 * buf_a[...] + g2 * buf_b[...])


def _combine(ys, p1, p2, x, gt, route, seq, rows=256):
    m, d = x.shape
    nb = seq // rows
    return pl.pallas_call(
        functools.partial(_combine_body, rows=rows),
        grid_spec=pltpu.PrefetchScalarGridSpec(
            num_scalar_prefetch=2,
            grid=(m // rows,),
            in_specs=[pl.BlockSpec(memory_space=pl.ANY),
                      pl.BlockSpec((rows, d), lambda i, a, b: (i, 0)),
                      pl.BlockSpec((1, 1, d), lambda i, a, b: (i // nb, 0, 0)),
                      pl.BlockSpec((rows, LANES), lambda i, a, b: (i, 0))],
            out_specs=pl.BlockSpec((rows, d), lambda i, a, b: (i, 0)),
            scratch_shapes=[pltpu.VMEM((rows, d), F32), pltpu.VMEM((rows, d), F32),
                            pltpu.SemaphoreType.DMA(())],
        ),
        out_shape=jax.ShapeDtypeStruct((m, d), F32),
        compiler_params=_params("arbitrary"),
        name="moe_combine",
    )(p1, p2, ys, x, gt.reshape(-1, 1, d), route)


def _routing_tables(route, counts, tm, n_tiles):
    t = route.shape[0]
    experts = route[:, :2].astype(I32).reshape(-1)
    rank = route[:, 4:6].astype(I32).reshape(-1)
    counts = counts[0, :N_EXPERTS].astype(I32)
    tiles = (counts + tm - 1) // tm
    tend = jnp.cumsum(tiles)
    tstart = tend - tiles
    slot = tstart[experts] * tm + rank
    token = jnp.arange(2 * t, dtype=I32) // 2
    src = jnp.zeros((n_tiles * tm,), I32).at[slot].set(token)
    tile_id = jnp.arange(n_tiles, dtype=I32)
    te = jnp.sum((tile_id[:, None] >= tend[None, :]).astype(I32), axis=1)
    active = tile_id < tend[-1]
    last_e = jnp.sum((tend[-1] - 1 >= tend).astype(I32))
    te = jnp.where(active, te, last_e)
    tv = jnp.where(active, jnp.clip(counts[te] - (tile_id - tstart[te]) * tm, 0, tm), 0)
    slots = slot.reshape(t, 2)
    n_active = (tend[-1] * tm).astype(I32).reshape(1)
    return src, n_active, te.astype(I32), tv.astype(I32), slots[:, 0], slots[:, 1]


def _bias_table(rel_bias):
    h = rel_bias.shape[1]
    return jnp.full((h, LANES), NEG, F32).at[:, :N_BUCKETS].set(rel_bias.T * LOG2E)


def kernel(x, c, positions, rel_bias, w_mod, b_mod, g_attn, g_ffn, a_w_in, a_w_out, a_g_qn, a_g_kn,
           kv_w_mod, kv_b_mod, kv_g, kv_w, b_g_kn, b_w_q, b_w_out, b_g_qn, ffn_w1, ffn_w3, ffn_w2,
           moe_router, moe_router_b, moe_w1, moe_w3, moe_w2):
    batch, seq, d = x.shape
    m = batch * seq
    x2 = x.reshape(m, d)
    positions = positions.astype(I32)

    c8 = jnp.zeros((8, d), F32).at[:batch].set(c)
    mod0 = _mod_call(c8, w_mod, 0, b_mod)[:batch]
    mod1 = _mod_call(c8, w_mod, 1, b_mod)[:batch]
    kvm = _mod_call(c8, kv_w_mod[None], 0, kv_b_mod[None])[:batch]
    sh1_0, sc1_0, gt1_0, sh2_0, sc2_0, gt2_0 = jnp.split(mod0, 6, axis=-1)
    sh1_1, sc1_1, gt1_1, sh2_1, sc2_1, gt2_1 = jnp.split(mod1, 6, axis=-1)
    kv_sh, kv_sc = jnp.split(kvm, 2, axis=-1)

    tab = _bias_table(rel_bias)

    a_main = A_HEADS * A_HEAD_DIM + 2 * A_KV_HEADS * A_HEAD_DIM + IDX_HEADS * IDX_DIM
    n_tail = IDX_DIM + IDX_HEADS
    w_tail = jnp.zeros((d, LANES), F32).at[:, :n_tail].set(a_w_in[0, :, a_main:a_main + n_tail])
    qkv, tail = _nm_matmul(x2, g_attn[0], sc1_0, sh1_0, a_w_in, 0, a_main, seq, w_tail=w_tail)
    attn = _dsa_attention(qkv, tail, positions, tab,
                          a_g_qn[0].reshape(1, -1), a_g_kn[0].reshape(1, -1), batch, seq)
    x2 = _matmul_residual(attn, a_w_out, 0, x2, gt1_0, seq, tn=512)
    u = _nm_swiglu(x2, g_ffn[0], sc2_0, sh2_0, ffn_w1, ffn_w3, 0, seq)
    x2 = _matmul_residual(u, ffn_w2, 0, x2, gt2_0, seq)

    b_q = len(B_DILATIONS) * B_HEADS * B_HEAD_DIM
    dil = tuple(r for _, r in B_DILATIONS)
    gcols = B_HEADS * B_HEAD_DIM
    kvall = _nm_matmul(x2, kv_g, kv_sc, kv_sh, kv_w[None], 0, 2 * b_q, seq,
                       dilations=dil, group_cols=gcols, tn=256)
    qall = _nm_matmul(x2, g_attn[1], sc1_1, sh1_1, b_w_q, 0, b_q, seq,
                      dilations=dil, group_cols=gcols, tn=256)
    gq2 = jnp.tile(b_g_qn[0], 2).reshape(1, LANES)
    gk2 = jnp.tile(b_g_kn, 2).reshape(1, LANES)
    w = B_HEADS * B_HEAD_DIM
    ng = len(B_DILATIONS)
    outs, lses = [], []
    for g, (window, r) in enumerate(B_DILATIONS):
        pos_g = positions.reshape(batch, seq // r, r).transpose(0, 2, 1).reshape(batch, seq)
        o_g, lse_g = _win_attention(qall, g, kvall, g, kvall, ng + g, pos_g, tab, gq2, gk2,
                                    batch, seq, r, window)
        outs.append(o_g)
        lses.append(lse_g)
    x2 = _merge_out(outs, lses, dil, b_w_out, 0, x2, gt1_1, seq)

    h, route, counts = _router(x2, g_ffn[1], sc2_1, sh2_1, moe_router[0], moe_router_b[0], seq)
    share = (2 * m) // N_EXPERTS
    tm = -(-(share * 17 // 32) // 64) * 64
    n_tiles = (2 * m) // tm + N_EXPERTS
    src, n_active, te, tv, p1, p2 = _routing_tables(route, counts, tm, n_tiles)
    hs = _gather_rows(h, src, n_active, tm // 4)
    ys = _moe_experts(hs, te, tv, moe_w1.reshape(moe_w1.shape[1:]), moe_w3.reshape(moe_w3.shape[1:]),
                      moe_w2.reshape(moe_w2.shape[1:]), tm)
    out = _combine(ys, p1, p2, x2, gt2_1, route, seq)
    return out.reshape(batch, seq, d)
```

```python
import functools
import math

import jax
import jax.numpy as jnp
from jax import lax
from jax.experimental import pallas as pl
from jax.experimental.pallas import tpu as pltpu

F32 = jnp.float32
BF16 = jnp.bfloat16
I32 = jnp.int32

EPS = 1e-6
NEG = -1e30
INT_MIN = -(2 ** 31)
LOG2E = 1.0 / math.log(2.0)

A_HEADS, A_KV_HEADS, A_HEAD_DIM = 16, 4, 128
IDX_HEADS, IDX_DIM = 16, 64
TOPK_MAX = 256
B_DILATIONS = ((128, 1), (512, 4), (2048, 16))
B_HEADS, B_HEAD_DIM = 16, 64
N_BUCKETS, MAX_DISTANCE = 32, 2048
N_EXPERTS = 8
LANES = 128

VMEM_LIMIT_BYTES = 56 * 1024 * 1024

_NT = (((1,), (1,)), ((), ()))


def _params(*sem):
    return pltpu.CompilerParams(dimension_semantics=sem, vmem_limit_bytes=VMEM_LIMIT_BYTES)


def _dot(a, b):
    return jnp.dot(a, b, preferred_element_type=F32)


def _dot_nt(a, b):
    return lax.dot_general(a, b, _NT, preferred_element_type=F32)


def _sigmoid(x):
    return 1.0 / (1.0 + jnp.exp(-x))


def _t5_bucket(rel):
    n = jnp.maximum(rel, 0)
    max_exact = N_BUCKETS // 2
    nf = jnp.maximum(n, 1).astype(F32)
    large = max_exact + (jnp.log(nf / max_exact) / math.log(MAX_DISTANCE / max_exact)
                         * (N_BUCKETS - max_exact)).astype(I32)
    large = jnp.minimum(large, N_BUCKETS - 1)
    return jnp.where(n < max_exact, n, large)


def _norm_mod(x, g, sc, sh):
    y = x * lax.rsqrt(jnp.mean(x * x, axis=-1, keepdims=True) + EPS)
    return (y * g) * (1.0 + sc) + sh


def _bias_lookup(tab_ref, heads, bkt):
    rows, cols = bkt.shape
    tabs = [jnp.broadcast_to(tab_ref[h:h + 1, :], (8, LANES)) for h in heads]
    tiles = [[] for _ in tabs]
    for r in range(rows // 8):
        pieces = [[] for _ in tabs]
        for c in range(cols // LANES):
            idx = bkt[r * 8:(r + 1) * 8, c * LANES:(c + 1) * LANES]
            for k, tab in enumerate(tabs):
                pieces[k].append(jnp.take_along_axis(tab, idx, axis=1))
        for k in range(len(tabs)):
            tiles[k].append(pieces[k][0] if len(pieces[k]) == 1 else jnp.concatenate(pieces[k], axis=1))
    return [jnp.concatenate(t, axis=0) for t in tiles]


def _mod_body(c_ref, w_ref, b_ref, o_ref):
    c = c_ref[...]
    cs = c * _sigmoid(c)
    o_ref[...] = _dot(cs.astype(BF16), w_ref[...].astype(BF16)) + b_ref[...]


def _mod_call(c8, w3, layer, b2):
    _, d, n = w3.shape
    tn = 1024
    return pl.pallas_call(
        _mod_body,
        grid=(n // tn,),
        in_specs=[pl.BlockSpec((8, d), lambda j: (0, 0)),
                  pl.BlockSpec((None, d, tn), lambda j: (layer, 0, j)),
                  pl.BlockSpec((None, 1, tn), lambda j: (layer, 0, j))],
        out_specs=pl.BlockSpec((8, tn), lambda j: (0, j)),
        out_shape=jax.ShapeDtypeStruct((8, n), F32),
        compiler_params=_params("arbitrary"),
        name="adaln_mod",
    )(c8, w3, b2.reshape(b2.shape[0], 1, n))


def _nm_body(x_ref, g_ref, sc_ref, sh_ref, w_ref, *rest, has_tail, dilations, group_cols):
    if has_tail:
        wt_ref, o_ref, ot_ref, h_scr = rest
    elif dilations:
        o_ref, h_scr, y_scr = rest
    else:
        o_ref, h_scr = rest

    @pl.when(pl.program_id(1) == 0)
    def _():
        h = _norm_mod(x_ref[...], g_ref[...], sc_ref[0], sh_ref[0]).astype(BF16)
        h_scr[...] = h
        if has_tail:
            ot_ref[...] = _dot(h, wt_ref[...].astype(BF16)).astype(ot_ref.dtype)

    y = _dot(h_scr[...], w_ref[...].astype(BF16))
    if not dilations:
        o_ref[...] = y.astype(o_ref.dtype)
        return
    tm, tn = o_ref.shape
    group = (pl.program_id(1) * tn // group_cols) % len(dilations)
    for k, r in enumerate(dilations):
        @pl.when(group == k)
        def _(r=r):
            if r == 1:
                o_ref[...] = y.astype(o_ref.dtype)
                return
            for c in range(tn // LANES):
                y_scr[c] = y[:, c * LANES:(c + 1) * LANES]
            cls = tm // r
            for rho in range(r):
                for c in range(tn // LANES):
                    o_ref[rho * cls:(rho + 1) * cls, c * LANES:(c + 1) * LANES] = (
                        y_scr[c, pl.ds(rho, cls, stride=r), :].astype(o_ref.dtype))


def _nm_matmul(x, g, sc, sh, w3, layer, n_cols, seq, w_tail=None, dilations=None, group_cols=None,
               tm=2048, tn=512):
    m, d = x.shape
    tm = min(tm, seq)
    nb = seq // tm
    has_tail = w_tail is not None
    assert not dilations or (tm == seq and not has_tail and group_cols % tn == 0)
    in_specs = [pl.BlockSpec((tm, d), lambda i, j: (i, 0), pipeline_mode=pl.Buffered(1)),
                pl.BlockSpec((1, d), lambda i, j: (0, 0)),
                pl.BlockSpec((1, 1, d), lambda i, j: (i // nb, 0, 0)),
                pl.BlockSpec((1, 1, d), lambda i, j: (i // nb, 0, 0)),
                pl.BlockSpec((None, d, tn), lambda i, j: (layer, 0, j))]
    args = [x, g.reshape(1, d), sc.reshape(-1, 1, d), sh.reshape(-1, 1, d), w3]
    out_specs = [pl.BlockSpec((tm, tn), lambda i, j: (i, j))]
    out_shape = [jax.ShapeDtypeStruct((m, n_cols), BF16)]
    if has_tail:
        in_specs.append(pl.BlockSpec((d, LANES), lambda i, j: (0, 0)))
        args.append(w_tail)
        out_specs.append(pl.BlockSpec((tm, LANES), lambda i, j: (i, 0)))
        out_shape.append(jax.ShapeDtypeStruct((m, LANES), BF16))
    scratch = [pltpu.VMEM((tm, d), BF16)]
    if dilations:
        scratch.append(pltpu.VMEM((tn // LANES, tm, LANES), F32))
    res = pl.pallas_call(
        functools.partial(_nm_body, has_tail=has_tail, dilations=dilations, group_cols=group_cols),
        grid=(m // tm, n_cols // tn),
        in_specs=in_specs,
        out_specs=out_specs,
        out_shape=out_shape,
        scratch_shapes=scratch,
        compiler_params=_params("arbitrary", "arbitrary"),
        name="norm_mod_matmul",
    )(*args)
    return res if has_tail else res[0]


def _mmres_body(a_ref, w_ref, x_ref, gt_ref, o_ref):
    o_ref[...] = x_ref[...] + gt_ref[0] * _dot(a_ref[...], w_ref[...].astype(BF16))


def _matmul_residual(a, w3, layer, x, gt, seq, tm=2048, tn=256):
    m, k = a.shape
    d = x.shape[1]
    tm = min(tm, seq)
    nb = seq // tm
    return pl.pallas_call(
        _mmres_body,
        grid=(m // tm, d // tn),
        in_specs=[pl.BlockSpec((tm, k), lambda i, j: (i, 0), pipeline_mode=pl.Buffered(1)),
                  pl.BlockSpec((None, k, tn), lambda i, j: (layer, 0, j)),
                  pl.BlockSpec((tm, tn), lambda i, j: (i, j)),
                  pl.BlockSpec((1, 1, tn), lambda i, j: (i // nb, 0, j))],
        out_specs=pl.BlockSpec((tm, tn), lambda i, j: (i, j)),
        out_shape=jax.ShapeDtypeStruct((m, d), F32),
        compiler_params=_params("arbitrary", "arbitrary"),
        name="matmul_residual",
    )(a, w3, x, gt.reshape(-1, 1, d))


def _nm_swiglu_body(x_ref, g_ref, sc_ref, sh_ref, w1_ref, w3_ref, o_ref, h_scr):
    @pl.when(pl.program_id(1) == 0)
    def _():
        h_scr[...] = _norm_mod(x_ref[...], g_ref[...], sc_ref[0], sh_ref[0]).astype(BF16)

    h = h_scr[...]
    a = _dot(h, w1_ref[...].astype(BF16))
    b = _dot(h, w3_ref[...].astype(BF16))
    o_ref[...] = (a * _sigmoid(a) * b).astype(o_ref.dtype)


def _nm_swiglu(x, g, sc, sh, w1, w3, layer, seq, tm=2048, tf=256):
    m, d = x.shape
    f = w1.shape[2]
    tm = min(tm, seq)
    nb = seq // tm
    return pl.pallas_call(
        _nm_swiglu_body,
        grid=(m // tm, f // tf),
        in_specs=[pl.BlockSpec((tm, d), lambda i, j: (i, 0), pipeline_mode=pl.Buffered(1)),
                  pl.BlockSpec((1, d), lambda i, j: (0, 0)),
                  pl.BlockSpec((1, 1, d), lambda i, j: (i // nb, 0, 0)),
                  pl.BlockSpec((1, 1, d), lambda i, j: (i // nb, 0, 0)),
                  pl.BlockSpec((None, d, tf), lambda i, j: (layer, 0, j)),
                  pl.BlockSpec((None, d, tf), lambda i, j: (layer, 0, j))],
        out_specs=pl.BlockSpec((tm, tf), lambda i, j: (i, j)),
        out_shape=jax.ShapeDtypeStruct((m, f), BF16),
        scratch_shapes=[pltpu.VMEM((tm, d), BF16)],
        compiler_params=_params("arbitrary", "arbitrary"),
        name="norm_mod_swiglu_up",
    )(x, g.reshape(1, d), sc.reshape(-1, 1, d), sh.reshape(-1, 1, d), w1, w3)


def _dsa_body(q_ref, qi_ref, k_ref, v_ref, tq_ref, tk_ref, pr_ref, pc_ref, tab_ref, gq_ref, gk_ref,
              o_ref,
              kn_scr, vt_scr, kke_scr, kko_scr, qst_scr, wib_scr, key_scr, qn_scr, m_scr, l_scr, acc_scr,
              s_buf, cm_buf,
              *, topk, tq, seq):
    i = pl.program_id(1)
    nc = i + 1
    grp = A_HEADS // A_KV_HEADS
    hd = A_HEAD_DIM
    lane = lax.broadcasted_iota(I32, (1, LANES), 1)
    krow = lax.broadcasted_iota(I32, (tq, tq), 0)
    qcol = lax.broadcasted_iota(I32, (tq, tq), 1)

    @pl.when(i == 0)
    def _prepare_keys():
        gk = gk_ref[...]

        def body(r, carry):
            rows = pl.ds(pl.multiple_of(r * tq, tq), tq)
            for kh in range(A_KV_HEADS):
                cols = slice(kh * hd, (kh + 1) * hd)
                kb = k_ref[rows, cols].astype(F32)
                ms = jnp.mean(kb * kb, axis=-1, keepdims=True)
                kn_scr[rows, cols] = ((kb * lax.rsqrt(ms + EPS)) * gk).astype(BF16)
                vt_scr[r, cols, :] = v_ref[rows, cols].astype(F32).T.astype(BF16)
            t = tk_ref[rows, :].astype(F32)
            kke_scr[rows, :] = jnp.where(lane < IDX_DIM, t, 0.0).astype(BF16)
            kko_scr[rows, :] = jnp.where(lane >= IDX_DIM, pltpu.roll(t, IDX_DIM, 1), 0.0).astype(BF16)
            return carry

        lax.fori_loop(0, seq // tq, body, 0)

    for j in range(IDX_HEADS // 2):
        qst_scr[j * tq:(j + 1) * tq, :] = qi_ref[:, j * LANES:(j + 1) * LANES]
    w_scale = (IDX_DIM ** -0.5) * (IDX_HEADS ** -0.5)
    wib_scr[...] = tq_ref[...].astype(F32).T[IDX_DIM:IDX_DIM + IDX_HEADS, :] * w_scale

    def idx_body(c, carry):
        rows = pl.ds(pl.multiple_of(c * tq, tq), tq)
        qst = qst_scr[...]
        re = _dot_nt(kke_scr[rows, :], qst)
        ro = _dot_nt(kko_scr[rows, :], qst)
        acc = jnp.zeros((tq, tq), F32)
        for j in range(IDX_HEADS // 2):
            acc = acc + jnp.maximum(re[:, j * tq:(j + 1) * tq], 0.0) * wib_scr[2 * j:2 * j + 1, :]
            acc = acc + jnp.maximum(ro[:, j * tq:(j + 1) * tq], 0.0) * wib_scr[2 * j + 1:2 * j + 2, :]
        bits = pltpu.bitcast(acc, I32)
        key = bits ^ ((bits >> 31) & 0x7FFFFFFF)
        causal = (c < i) | (krow <= qcol)
        key_scr[c] = jnp.where(causal, key, INT_MIN)
        return carry

    lax.fori_loop(0, nc, idx_body, 0)

    def bit_body(bi, t_u):
        cand_u = t_u | lax.shift_left(jnp.int32(1), 31 - bi)
        cand_s = cand_u ^ INT_MIN

        def cnt_body(c, cnt):
            ge = jnp.where(key_scr[c] >= cand_s, 1, 0)
            return cnt + jnp.sum(ge.reshape(tq // 8, 8, tq), axis=0)

        cnt = lax.fori_loop(0, nc, cnt_body, jnp.zeros((8, tq), I32))
        total = jnp.sum(cnt, axis=0, keepdims=True)
        return jnp.where(total >= topk, cand_u, t_u)

    nbits = jnp.where(nc * tq > topk, 32, 0)
    t_u = lax.fori_loop(0, nbits, bit_body, jnp.zeros((1, tq), I32))
    thr = t_u ^ INT_MIN

    qpos = pr_ref[0, i]

    def bkt_body(c, carry):
        rows = pl.ds(pl.multiple_of(c * tq, tq), tq)
        bkt = _t5_bucket(qpos - pc_ref[rows, :])
        causal = (c < i) | (krow <= qcol)
        sel = (key_scr[c] >= thr) & causal
        key_scr[c] = jnp.where(sel, bkt, N_BUCKETS)
        return carry

    lax.fori_loop(0, nc, bkt_body, 0)

    gq = gq_ref[...]
    scale = (hd ** -0.5) * LOG2E
    for g in range(A_KV_HEADS):
        for hh in range(grp):
            h = g * grp + hh
            qh = q_ref[:, h * hd:(h + 1) * hd].astype(F32)
            ms = jnp.mean(qh * qh, axis=-1, keepdims=True)
            qn_scr[hh * tq:(hh + 1) * tq, :] = (((qh * lax.rsqrt(ms + EPS)) * gq) * scale).astype(BF16)
        m_scr[...] = jnp.full(m_scr.shape, NEG, F32)
        l_scr[...] = jnp.zeros(l_scr.shape, F32)
        acc_scr[...] = jnp.zeros(acc_scr.shape, F32)

        def scores(c, g=g):
            rows = pl.ds(pl.multiple_of(c * tq, tq), tq)
            s = _dot_nt(kn_scr[rows, g * hd:(g + 1) * hd], qn_scr[...])
            bias = _bias_lookup(tab_ref, range(g * grp, (g + 1) * grp), key_scr[c])
            s = jnp.concatenate([s[:, hh * tq:(hh + 1) * tq] + bias[hh] for hh in range(grp)], axis=1)
            s_buf[c % 2] = s
            cm_buf[c % 2] = jnp.max(s, axis=0, keepdims=True)

        def accumulate(c, g=g):
            s = s_buf[c % 2]
            m_old = m_scr[...]
            m_new = jnp.maximum(m_old, cm_buf[c % 2])
            p = jnp.exp2(s - m_new)
            alpha = jnp.exp2(m_old - m_new)
            l_scr[...] = alpha * l_scr[...] + jnp.sum(p, axis=0, keepdims=True)
            acc_scr[...] = alpha * acc_scr[...] + _dot(vt_scr[c, g * hd:(g + 1) * hd, :], p.astype(BF16))
            m_scr[...] = m_new

        def att_body(c, carry):
            accumulate(c)
            scores(c + 1)
            return carry

        scores(0)
        lax.fori_loop(0, nc - 1, att_body, 0)
        accumulate(nc - 1)
        o = acc_scr[...] * (1.0 / l_scr[...])
        for hh in range(grp):
            h = g * grp + hh
            o_ref[:, h * hd:(h + 1) * hd] = o[:, hh * tq:(hh + 1) * tq].T.astype(o_ref.dtype)


def _dsa_attention(qkv, tail, pos, tab, gq, gk, batch, seq, tq=256):
    tq = min(tq, seq)
    nq = seq // tq
    a_q = A_HEADS * A_HEAD_DIM
    a_kv = A_KV_HEADS * A_HEAD_DIM
    a_qi = IDX_HEADS * IDX_DIM
    topk = min(TOPK_MAX, seq // 4)
    grp = A_HEADS // A_KV_HEADS
    body = functools.partial(_dsa_body, topk=topk, tq=tq, seq=seq)
    return pl.pallas_call(
        body,
        grid=(batch, nq),
        in_specs=[
            pl.BlockSpec((tq, a_q), lambda b, i: (b * nq + i, 0)),
            pl.BlockSpec((tq, a_qi), lambda b, i: (b * nq + i, (a_q + 2 * a_kv) // a_qi)),
            pl.BlockSpec((seq, a_kv), lambda b, i: (b, a_q // a_kv)),
            pl.BlockSpec((seq, a_kv), lambda b, i: (b, a_q // a_kv + 1)),
            pl.BlockSpec((tq, LANES), lambda b, i: (b * nq + i, 0)),
            pl.BlockSpec((seq, LANES), lambda b, i: (b, 0)),
            pl.BlockSpec((1, nq, 1, tq), lambda b, i: (b, 0, 0, 0)),
            pl.BlockSpec((seq, 1), lambda b, i: (b, 0)),
            pl.BlockSpec((A_HEADS, LANES), lambda b, i: (0, 0)),
            pl.BlockSpec((1, A_HEAD_DIM), lambda b, i: (0, 0)),
            pl.BlockSpec((1, A_HEAD_DIM), lambda b, i: (0, 0)),
        ],
        out_specs=pl.BlockSpec((tq, a_q), lambda b, i: (b * nq + i, 0)),
        out_shape=jax.ShapeDtypeStruct((batch * seq, a_q), BF16),
        scratch_shapes=[
            pltpu.VMEM((seq, a_kv), BF16),
            pltpu.VMEM((nq, a_kv, tq), BF16),
            pltpu.VMEM((seq, LANES), BF16),
            pltpu.VMEM((seq, LANES), BF16),
            pltpu.VMEM((IDX_HEADS // 2 * tq, LANES), BF16),
            pltpu.VMEM((IDX_HEADS, tq), F32),
            pltpu.VMEM((nq, tq, tq), I32),
            pltpu.VMEM((grp * tq, A_HEAD_DIM), BF16),
            pltpu.VMEM((1, grp * tq), F32),
            pltpu.VMEM((1, grp * tq), F32),
            pltpu.VMEM((A_HEAD_DIM, grp * tq), F32),
            pltpu.VMEM((2, tq, grp * tq), F32),
            pltpu.VMEM((2, 1, grp * tq), F32),
        ],
        compiler_params=_params("arbitrary", "arbitrary"),
        name="dsa_attention",
    )(qkv, qkv, qkv, qkv, tail, tail, pos.reshape(batch, nq, 1, tq), pos.reshape(batch * seq, 1),
      tab, gq, gk)


def _win_body(*refs, cls_len, wk, use_prev, tq):
    if use_prev:
        (q_ref, kc_ref, vc_ref, kp_ref, vp_ref, pr_ref, pcc_ref, pcp_ref, tab_ref, gq_ref, gk_ref,
         o_ref, lse_ref, kn_scr, vt_scr, qn_scr, bkt_scr, bias_scr, lse_scr) = refs
    else:
        (q_ref, kc_ref, vc_ref, pr_ref, pcc_ref, tab_ref, gq_ref, gk_ref,
         o_ref, lse_ref, kn_scr, vt_scr, qn_scr, bkt_scr, bias_scr, lse_scr) = refs
    t = pl.program_id(1)
    hd = B_HEAD_DIM
    pw = 2 * hd
    blk = LANES
    shift = cls_len.bit_length() - 1
    koff = blk if use_prev else 0
    win = 2 * blk if cls_len > blk else blk
    lo = lax.broadcasted_iota(I32, (1, pw), 1) < hd
    lo_rows = lax.broadcasted_iota(I32, (pw, 1), 0) < hd
    gq = gq_ref[...]
    gk = gk_ref[...]
    scale = (hd ** -0.5) * LOG2E

    def pair_norm(x, g):
        sq = x * x
        ms_lo = jnp.sum(jnp.where(lo, sq, 0.0), axis=-1, keepdims=True) * (1.0 / hd)
        ms_hi = jnp.sum(jnp.where(lo, 0.0, sq), axis=-1, keepdims=True) * (1.0 / hd)
        inv = jnp.where(lo, lax.rsqrt(ms_lo + EPS), lax.rsqrt(ms_hi + EPS))
        return (x * inv) * g

    for p in range(B_HEADS // 2):
        cols = slice(p * pw, (p + 1) * pw)
        qn_scr[:, cols] = (pair_norm(q_ref[:, cols].astype(F32), gq) * scale).astype(BF16)
        if use_prev:
            kn_scr[0:blk, cols] = pair_norm(kp_ref[:, cols].astype(F32), gk).astype(BF16)
            vt_scr[p, :, 0:blk] = vp_ref[:, cols].astype(F32).T.astype(BF16)
        kn_scr[koff:koff + tq, cols] = pair_norm(kc_ref[:, cols].astype(F32), gk).astype(BF16)
        vt_scr[p, :, koff:koff + tq] = vc_ref[:, cols].astype(F32).T.astype(BF16)

    lse_scr[...] = jnp.zeros(lse_scr.shape, F32)
    qpos_all = pr_ref[0]
    for j in range(tq // blk):
        qrows = slice(j * blk, (j + 1) * blk)
        own = koff + j * blk
        k0 = own - blk if (win > blk and own >= blk) else own
        if use_prev and k0 < koff:
            kpos = jnp.concatenate([pcp_ref[k0:koff, :], pcc_ref[0:k0 + win - koff, :]], axis=0)
        else:
            kpos = pcc_ref[k0 - koff:k0 - koff + win, :]
        fq = t * tq + j * blk + lax.broadcasted_iota(I32, (1, blk), 1)
        fk = t * tq + (k0 - koff) + lax.broadcasted_iota(I32, (win, 1), 0)
        same = ((fq + cls_len) >> shift) == ((fk + cls_len) >> shift)
        dist = (fq & (cls_len - 1)) - (fk & (cls_len - 1))
        ok = same & (dist >= 0) & (dist <= wk)
        bkt_scr[...] = jnp.where(ok, _t5_bucket(qpos_all[:, qrows] - kpos), N_BUCKETS)

        def bias_rows(r8, carry):
            rows = pl.ds(pl.multiple_of(r8 * 8, 8), 8)
            idx = bkt_scr[rows, :]
            for h in range(B_HEADS):
                tab = jnp.broadcast_to(tab_ref[h:h + 1, :], (8, LANES))
                bias_scr[h // 2, rows, (h % 2) * blk:(h % 2 + 1) * blk] = jnp.take_along_axis(tab, idx, axis=1)
            return carry

        lax.fori_loop(0, win // 8, bias_rows, 0, unroll=8)

        for p in range(B_HEADS // 2):
            cols = slice(p * pw, (p + 1) * pw)
            qp = qn_scr[qrows, cols]
            zero = jnp.zeros_like(qp)
            heads = (jnp.where(lo, qp, zero), jnp.where(lo, zero, qp))
            kw = kn_scr[k0:k0 + win, cols]
            vtw = vt_scr[p, :, k0:k0 + win]
            hp = 2 * blk // win
            outs = []
            for c in range(2 // hp):
                q2 = heads[c] if hp == 1 else jnp.concatenate(heads, axis=0)
                s = _dot_nt(kw, q2) + bias_scr[p, :, c * hp * blk:(c + 1) * hp * blk]
                m = jnp.max(s, axis=0, keepdims=True)
                e = jnp.exp2(s - m)
                l = jnp.sum(e, axis=0, keepdims=True)
                ot = _dot(vtw, e.astype(BF16)) * (1.0 / l)
                lse = m + jnp.log(l) * LOG2E
                for k in range(hp):
                    h = 2 * p + c * hp + k
                    outs.append(ot[:, k * blk:(k + 1) * blk])
                    lse_scr[h:h + 1, :] = lse[:, k * blk:(k + 1) * blk]
            o_ref[qrows, cols] = jnp.where(lo_rows, outs[0], outs[1]).T.astype(o_ref.dtype)
        lse_ref[qrows, :] = lse_scr[...].T


def _win_attention(q_arr, q_col, k_arr, k_col, v_arr, v_col, pos, tab, gq2, gk2,
                   batch, seq, dilation, window, tq=512):
    cls_len = seq // dilation
    wk = window // dilation
    assert cls_len & (cls_len - 1) == 0 and wk <= LANES
    tq = min(tq, seq)
    use_prev = cls_len > tq
    assert use_prev or tq % cls_len == 0
    nt = seq // tq
    w = B_HEADS * B_HEAD_DIM
    sub = tq // LANES
    pos_col = pos.reshape(batch * seq, 1)
    pos_row = pos.reshape(batch * nt, 1, tq)
    prev = lambda b, t: jnp.maximum((b * nt + t) * sub - 1, 0)
    in_specs = [pl.BlockSpec((tq, w), lambda b, t: (b * nt + t, q_col)),
                pl.BlockSpec((tq, w), lambda b, t: (b * nt + t, k_col)),
                pl.BlockSpec((tq, w), lambda b, t: (b * nt + t, v_col))]
    args = [q_arr, k_arr, v_arr]
    if use_prev:
        in_specs += [pl.BlockSpec((LANES, w), lambda b, t: (prev(b, t), k_col)),
                     pl.BlockSpec((LANES, w), lambda b, t: (prev(b, t), v_col))]
        args += [k_arr, v_arr]
    in_specs += [pl.BlockSpec((1, 1, tq), lambda b, t: (b * nt + t, 0, 0)),
                 pl.BlockSpec((tq, 1), lambda b, t: (b * nt + t, 0))]
    args += [pos_row, pos_col]
    if use_prev:
        in_specs.append(pl.BlockSpec((LANES, 1), lambda b, t: (prev(b, t), 0)))
        args.append(pos_col)
    in_specs += [pl.BlockSpec((B_HEADS, LANES), lambda b, t: (0, 0)),
                 pl.BlockSpec((1, LANES), lambda b, t: (0, 0)),
                 pl.BlockSpec((1, LANES), lambda b, t: (0, 0))]
    args += [tab, gq2, gk2]
    nk = tq + (LANES if use_prev else 0)
    win = 2 * LANES if cls_len > LANES else LANES
    body = functools.partial(_win_body, cls_len=cls_len, wk=wk, use_prev=use_prev, tq=tq)
    return pl.pallas_call(
        body,
        grid=(batch, nt),
        in_specs=in_specs,
        out_specs=[pl.BlockSpec((tq, w), lambda b, t: (b * nt + t, 0)),
                   pl.BlockSpec((tq, LANES), lambda b, t: (b * nt + t, 0))],
        out_shape=[jax.ShapeDtypeStruct((batch * seq, w), BF16),
                   jax.ShapeDtypeStruct((batch * seq, LANES), F32)],
        scratch_shapes=[pltpu.VMEM((nk, w), BF16),
                        pltpu.VMEM((B_HEADS // 2, LANES, nk), BF16),
                        pltpu.VMEM((tq, w), BF16),
                        pltpu.VMEM((win, LANES), I32),
                        pltpu.VMEM((B_HEADS // 2, win, 2 * LANES), F32),
                        pltpu.VMEM((LANES, LANES), F32)],
        compiler_params=_params("arbitrary", "arbitrary"),
        name="dilated_attention",
    )(*args)


def _merge_body(*refs, dilations):
    ng = len(dilations)
    o_refs, l_refs = refs[:ng], refs[ng:2 * ng]
    e_ref, w_ref, x_ref, gt_ref, out_ref, a_scr, o_scr, l_scr = refs[2 * ng:]
    tm, k = a_scr.shape

    @pl.when(pl.program_id(1) == 0)
    def _():
        for g, r in enumerate(dilations):
            cls = tm // r
            for rho in range(r):
                l_scr[g, pl.ds(rho, cls, stride=r), :] = l_refs[g][0, rho]
                for c in range(k // LANES):
                    o_scr[g, c, pl.ds(rho, cls, stride=r), :] = (
                        o_refs[g][0, rho, :, c * LANES:(c + 1) * LANES].astype(F32))
        ls = [l_scr[g] for g in range(ng)]
        m = functools.reduce(jnp.maximum, ls)
        ws = [jnp.exp2(l - m) for l in ls]
        inv = 1.0 / functools.reduce(lambda a, b: a + b, ws)
        wides = []
        for wg in ws:
            wn = wg * inv
            hi = wn.astype(BF16)
            lo = (wn - hi.astype(F32)).astype(BF16)
            wides.append(_dot(hi, e_ref[...]) + _dot(lo, e_ref[...]))
        for c in range(k // LANES):
            cols = slice(c * LANES, (c + 1) * LANES)
            num = wides[0][:, cols] * o_scr[0, c]
            for g in range(1, ng):
                num = num + wides[g][:, cols] * o_scr[g, c]
            a_scr[:, cols] = num.astype(BF16)

    out_ref[...] = x_ref[...] + gt_ref[0] * _dot(a_scr[...], w_ref[...].astype(BF16))


def _merge_out(os_, ls_, dilations, w3, layer, x, gt, seq, tm=512, tn=512):
    m, k = os_[0].shape
    d = x.shape[1]
    nb = seq // tm
    batch = m // seq
    head = jnp.arange(k, dtype=I32)[None, :] // B_HEAD_DIM
    expand = (jnp.arange(LANES, dtype=I32)[:, None] == head).astype(BF16)
    ng = len(dilations)
    o_specs = [pl.BlockSpec((1, r, tm // r, k), lambda i, j: (i // nb, 0, i % nb, 0)) for r in dilations]
    l_specs = [pl.BlockSpec((1, r, tm // r, LANES), lambda i, j: (i // nb, 0, i % nb, 0)) for r in dilations]
    o_args = [o.reshape(batch, r, seq // r, k) for o, r in zip(os_, dilations)]
    l_args = [l.reshape(batch, r, seq // r, LANES) for l, r in zip(ls_, dilations)]
    return pl.pallas_call(
        functools.partial(_merge_body, dilations=tuple(dilations)),
        grid=(m // tm, d // tn),
        in_specs=o_specs + l_specs + [
            pl.BlockSpec((LANES, k), lambda i, j: (0, 0)),
            pl.BlockSpec((None, k, tn), lambda i, j: (layer, 0, j)),
            pl.BlockSpec((tm, tn), lambda i, j: (i, j)),
            pl.BlockSpec((1, 1, tn), lambda i, j: (i // nb, 0, j))],
        out_specs=pl.BlockSpec((tm, tn), lambda i, j: (i, j)),
        out_shape=jax.ShapeDtypeStruct((m, d), F32),
        scratch_shapes=[pltpu.VMEM((tm, k), BF16),
                        pltpu.VMEM((ng, k // LANES, tm, LANES), F32),
                        pltpu.VMEM((ng, tm, LANES), F32)],
        compiler_params=_params("arbitrary", "arbitrary"),
        name="merge_out_proj",
    )(*o_args, *l_args, expand, w3, x, gt.reshape(-1, 1, d))


def _router_body(x_ref, g_ref, sc_ref, sh_ref, rh_ref, rl_ref, rb_ref, h_ref, rt_ref, cnt_ref, cnt_scr):
    h = _norm_mod(x_ref[...], g_ref[...], sc_ref[0], sh_ref[0])
    hh = h.astype(BF16)
    bits = pltpu.bitcast(hh.astype(F32), I32)
    half = h.shape[1] // 2
    h_ref[...] = lax.shift_right_logical(bits[:, :half], 16) | (bits[:, half:] & -65536)
    hl = (h - hh.astype(F32)).astype(BF16)
    logits = _dot(hh, rh_ref[...]) + _dot(hl, rh_ref[...]) + _dot(hh, rl_ref[...]) + rb_ref[...]
    lane = lax.broadcasted_iota(I32, logits.shape, 1)
    logits = jnp.where(lane < N_EXPERTS, logits, NEG)
    v1 = jnp.max(logits, axis=1, keepdims=True)
    i1 = jnp.min(jnp.where(logits == v1, lane, LANES), axis=1, keepdims=True)
    rest = jnp.where(lane == i1, NEG, logits)
    v2 = jnp.max(rest, axis=1, keepdims=True)
    i2 = jnp.min(jnp.where(rest == v2, lane, LANES), axis=1, keepdims=True)
    e = jnp.exp(v2 - v1)
    g1 = 1.0 / (1.0 + e)
    g2 = e * g1

    @pl.when(pl.program_id(0) == 0)
    def _():
        cnt_scr[...] = jnp.zeros(cnt_scr.shape, F32)

    tm = logits.shape[0]
    pick1 = lane == i1
    pick2 = lane == i2
    chosen = jnp.where(pick1 | pick2, 1.0, 0.0)
    earlier = (lax.broadcasted_iota(I32, (tm, tm), 1) < lax.broadcasted_iota(I32, (tm, tm), 0))
    before = _dot(jnp.where(earlier, 1.0, 0.0).astype(BF16), chosen.astype(BF16)) + cnt_scr[...]
    r1 = jnp.sum(jnp.where(pick1, before, 0.0), axis=1, keepdims=True)
    r2 = jnp.sum(jnp.where(pick2, before, 0.0), axis=1, keepdims=True)
    cnt_scr[...] = cnt_scr[...] + jnp.sum(chosen, axis=0, keepdims=True)
    cnt_ref[...] = cnt_scr[...]

    vals = (i1.astype(F32), i2.astype(F32), g1, g2, r1, r2)
    out = jnp.zeros(logits.shape, F32)
    for k, v in enumerate(vals):
        out = jnp.where(lane == k, v, out)
    rt_ref[...] = out


def _router(x, g, sc, sh, rw, rb, seq, tm=512):
    m, d = x.shape
    nb = seq // tm
    ne = rw.shape[1]
    rw_p = jnp.zeros((d, LANES), F32).at[:, :ne].set(rw)
    rh = rw_p.astype(BF16)
    rl = (rw_p - rh.astype(F32)).astype(BF16)
    rb_p = jnp.zeros((1, LANES), F32).at[0, :ne].set(rb)
    return pl.pallas_call(
        _router_body,
        grid=(m // tm,),
        in_specs=[pl.BlockSpec((tm, d), lambda i: (i, 0)),
                  pl.BlockSpec((1, d), lambda i: (0, 0)),
                  pl.BlockSpec((1, 1, d), lambda i: (i // nb, 0, 0)),
                  pl.BlockSpec((1, 1, d), lambda i: (i // nb, 0, 0)),
                  pl.BlockSpec((d, LANES), lambda i: (0, 0)),
                  pl.BlockSpec((d, LANES), lambda i: (0, 0)),
                  pl.BlockSpec((1, LANES), lambda i: (0, 0))],
        out_specs=[pl.BlockSpec((tm, d // 2), lambda i: (i, 0)),
                   pl.BlockSpec((tm, LANES), lambda i: (i, 0)),
                   pl.BlockSpec((1, LANES), lambda i: (0, 0))],
        out_shape=[jax.ShapeDtypeStruct((m, d // 2), I32),
                   jax.ShapeDtypeStruct((m, LANES), F32),
                   jax.ShapeDtypeStruct((1, LANES), F32)],
        scratch_shapes=[pltpu.VMEM((1, LANES), F32)],
        compiler_params=_params("arbitrary"),
        name="router_top2",
    )(x, g.reshape(1, d), sc.reshape(-1, 1, d), sh.reshape(-1, 1, d), rh, rl, rb_p)


def _row_copy(src_hbm, idx, buf, r, sem):
    return pltpu.make_async_copy(src_hbm.at[pl.ds(idx, 1), :], buf.at[pl.ds(r, 1), :], sem)


def _gather_body(src_ref, nact_ref, h_hbm, o_ref, buf, sem, *, rows):
    i = pl.program_id(0)
    n_active = nact_ref[0]

    def issue(step):
        slot = step % 2

        def body(r, carry):
            _row_copy(h_hbm, src_ref[step * rows + r], buf.at[slot], r, sem.at[slot]).start()
            return carry

        lax.fori_loop(0, rows, body, 0, unroll=8)

    @pl.when(i == 0)
    def _():
        issue(i)

    @pl.when((i + 1) * rows < n_active)
    def _():
        issue(i + 1)

    slot = i % 2

    @pl.when(i * rows < n_active)
    def _():
        pltpu.make_async_copy(h_hbm.at[pl.ds(0, rows), :], buf.at[slot], sem.at[slot]).wait()
        o_ref[...] = buf[slot]

    @pl.when(i * rows >= n_active)
    def _():
        o_ref[...] = jnp.zeros(o_ref.shape, o_ref.dtype)


def _gather_rows(h, src, n_active, rows):
    r_total = src.shape[0]
    w = h.shape[1]
    return pl.pallas_call(
        functools.partial(_gather_body, rows=rows),
        grid_spec=pltpu.PrefetchScalarGridSpec(
            num_scalar_prefetch=2,
            grid=(r_total // rows,),
            in_specs=[pl.BlockSpec(memory_space=pl.ANY)],
            out_specs=pl.BlockSpec((rows, w), lambda i, s, n: (i, 0)),
            scratch_shapes=[pltpu.VMEM((2, rows, w), h.dtype), pltpu.SemaphoreType.DMA((2,))],
        ),
        out_shape=jax.ShapeDtypeStruct((r_total, w), h.dtype),
        compiler_params=_params("arbitrary"),
        name="moe_dispatch_gather",
    )(src, n_active, h)


def _moe_body(te_ref, tv_ref, hs_ref, w1_ref, w3_ref, w2_ref, o_ref, h_scr, *, ncol):
    t = pl.program_id(0)
    f = pl.program_id(1)
    valid = tv_ref[t]
    d = o_ref.shape[1]
    cw = d // ncol

    @pl.when((valid > 0) & (f == 0))
    def _():
        words = hs_ref[...]
        h_scr[:, :d // 2] = pltpu.bitcast(words << 16, F32).astype(BF16)
        h_scr[:, d // 2:] = pltpu.bitcast(words & -65536, F32).astype(BF16)

    @pl.when(valid > 0)
    def _():
        h = h_scr[...]
        a = _dot(h, w1_ref[...].astype(BF16))
        b = _dot(h, w3_ref[...].astype(BF16))
        u = (a * _sigmoid(a) * b).astype(BF16)
        w2 = w2_ref[...].astype(BF16)

        @pl.when(f == 0)
        def _():
            for n in range(ncol):
                o_ref[:, n * cw:(n + 1) * cw] = _dot(u, w2[:, n * cw:(n + 1) * cw])

        @pl.when(f > 0)
        def _():
            for n in range(ncol):
                o_ref[:, n * cw:(n + 1) * cw] += _dot(u, w2[:, n * cw:(n + 1) * cw])

    @pl.when((valid == 0) & (f == 0))
    def _():
        o_ref[...] = jnp.zeros(o_ref.shape, o_ref.dtype)


def _moe_experts(hs, tile_expert, tile_valid, w1, w3, w2, tm, tf=256, ncol=4):
    r_total = hs.shape[0]
    d = w1.shape[1]
    n_tiles = r_total // tm
    nf = w1.shape[2] // tf

    def w13_map(t, f, te, tv):
        return (te[t], 0, jnp.where(tv[t] > 0, f, nf - 1))

    def w2_map(t, f, te, tv):
        return (te[t], jnp.where(tv[t] > 0, f, nf - 1), 0)

    return pl.pallas_call(
        functools.partial(_moe_body, ncol=ncol),
        grid_spec=pltpu.PrefetchScalarGridSpec(
            num_scalar_prefetch=2,
            grid=(n_tiles, nf),
            in_specs=[pl.BlockSpec((tm, d // 2), lambda t, f, te, tv: (t, 0)),
                      pl.BlockSpec((None, d, tf), w13_map),
                      pl.BlockSpec((None, d, tf), w13_map),
                      pl.BlockSpec((None, tf, d), w2_map)],
            out_specs=pl.BlockSpec((tm, d), lambda t, f, te, tv: (t, 0)),
            scratch_shapes=[pltpu.VMEM((tm, d), BF16)],
        ),
        out_shape=jax.ShapeDtypeStruct((r_total, d), F32),
        compiler_params=_params("arbitrary", "arbitrary"),
        name="moe_experts",
    )(tile_expert, tile_valid, hs, w1, w3, w2)


def _combine_body(p1_ref, p2_ref, y_hbm, x_ref, gt_ref, rt_ref, o_ref, buf_a, buf_b, sem, *, rows):
    base = pl.program_id(0) * rows

    def issue(r, carry):
        _row_copy(y_hbm, p1_ref[base + r], buf_a, r, sem).start()
        _row_copy(y_hbm, p2_ref[base + r], buf_b, r, sem).start()
        return carry

    lax.fori_loop(0, rows, issue, 0, unroll=4)
    pltpu.make_async_copy(y_hbm.at[pl.ds(0, rows), :], buf_a, sem).wait()
    pltpu.make_async_copy(y_hbm.at[pl.ds(0, rows), :], buf_b, sem).wait()
    rt = rt_ref[...]
    g1 = rt[:, 2:3]
    g2 = rt[:, 3:4]
    o_ref[...] = x_ref[...] + gt_ref[0] * (g1 * buf_a[...] + g2 * buf_b[...])


def _combine(ys, p1, p2, x, gt, route, seq, rows=256):
    m, d = x.shape
    nb = seq // rows
    return pl.pallas_call(
        functools.partial(_combine_body, rows=rows),
        grid_spec=pltpu.PrefetchScalarGridSpec(
            num_scalar_prefetch=2,
            grid=(m // rows,),
            in_specs=[pl.BlockSpec(memory_space=pl.ANY),
                      pl.BlockSpec((rows, d), lambda i, a, b: (i, 0)),
                      pl.BlockSpec((1, 1, d), lambda i, a, b: (i // nb, 0, 0)),
                      pl.BlockSpec((rows, LANES), lambda i, a, b: (i, 0))],
            out_specs=pl.BlockSpec((rows, d), lambda i, a, b: (i, 0)),
            scratch_shapes=[pltpu.VMEM((rows, d), F32), pltpu.VMEM((rows, d), F32),
                            pltpu.SemaphoreType.DMA(())],
        ),
        out_shape=jax.ShapeDtypeStruct((m, d), F32),
        compiler_params=_params("arbitrary"),
        name="moe_combine",
    )(p1, p2, ys, x, gt.reshape(-1, 1, d), route)


def _routing_tables(route, counts, tm, n_tiles):
    t = route.shape[0]
    experts = route[:, :2].astype(I32).reshape(-1)
    rank = route[:, 4:6].astype(I32).reshape(-1)
    counts = counts[0, :N_EXPERTS].astype(I32)
    tiles = (counts + tm - 1) // tm
    tend = jnp.cumsum(tiles)
    tstart = tend - tiles
    slot = tstart[experts] * tm + rank
    token = jnp.arange(2 * t, dtype=I32) // 2
    src = jnp.zeros((n_tiles * tm,), I32).at[slot].set(token, unique_indices=True, mode="promise_in_bounds")
    tile_id = jnp.arange(n_tiles, dtype=I32)
    te = jnp.sum((tile_id[:, None] >= tend[None, :]).astype(I32), axis=1)
    active = tile_id < tend[-1]
    last_e = jnp.sum((tend[-1] - 1 >= tend).astype(I32))
    te = jnp.where(active, te, last_e)
    tv = jnp.where(active, jnp.clip(counts[te] - (tile_id - tstart[te]) * tm, 0, tm), 0)
    slots = slot.reshape(t, 2)
    n_active = (tend[-1] * tm).astype(I32).reshape(1)
    return src, n_active, te.astype(I32), tv.astype(I32), slots[:, 0], slots[:, 1]


def _bias_table(rel_bias):
    h = rel_bias.shape[1]
    return jnp.full((h, LANES), NEG, F32).at[:, :N_BUCKETS].set(rel_bias.T * LOG2E)


def kernel(x, c, positions, rel_bias, w_mod, b_mod, g_attn, g_ffn, a_w_in, a_w_out, a_g_qn, a_g_kn,
           kv_w_mod, kv_b_mod, kv_g, kv_w, b_g_kn, b_w_q, b_w_out, b_g_qn, ffn_w1, ffn_w3, ffn_w2,
           moe_router, moe_router_b, moe_w1, moe_w3, moe_w2):
    batch, seq, d = x.shape
    m = batch * seq
    x2 = x.reshape(m, d)
    positions = positions.astype(I32)

    c8 = jnp.zeros((8, d), F32).at[:batch].set(c)
    mod0 = _mod_call(c8, w_mod, 0, b_mod)[:batch]
    mod1 = _mod_call(c8, w_mod, 1, b_mod)[:batch]
    kvm = _mod_call(c8, kv_w_mod[None], 0, kv_b_mod[None])[:batch]
    sh1_0, sc1_0, gt1_0, sh2_0, sc2_0, gt2_0 = jnp.split(mod0, 6, axis=-1)
    sh1_1, sc1_1, gt1_1, sh2_1, sc2_1, gt2_1 = jnp.split(mod1, 6, axis=-1)
    kv_sh, kv_sc = jnp.split(kvm, 2, axis=-1)

    tab = _bias_table(rel_bias)

    a_main = A_HEADS * A_HEAD_DIM + 2 * A_KV_HEADS * A_HEAD_DIM + IDX_HEADS * IDX_DIM
    n_tail = IDX_DIM + IDX_HEADS
    w_tail = jnp.zeros((d, LANES), F32).at[:, :n_tail].set(a_w_in[0, :, a_main:a_main + n_tail])
    qkv, tail = _nm_matmul(x2, g_attn[0], sc1_0, sh1_0, a_w_in, 0, a_main, seq, w_tail=w_tail)
    attn = _dsa_attention(qkv, tail, positions, tab,
                          a_g_qn[0].reshape(1, -1), a_g_kn[0].reshape(1, -1), batch, seq)
    x2 = _matmul_residual(attn, a_w_out, 0, x2, gt1_0, seq, tn=512)
    u = _nm_swiglu(x2, g_ffn[0], sc2_0, sh2_0, ffn_w1, ffn_w3, 0, seq)
    x2 = _matmul_residual(u, ffn_w2, 0, x2, gt2_0, seq)

    b_q = len(B_DILATIONS) * B_HEADS * B_HEAD_DIM
    dil = tuple(r for _, r in B_DILATIONS)
    gcols = B_HEADS * B_HEAD_DIM
    kvall = _nm_matmul(x2, kv_g, kv_sc, kv_sh, kv_w[None], 0, 2 * b_q, seq,
                       dilations=dil, group_cols=gcols, tn=256)
    qall = _nm_matmul(x2, g_attn[1], sc1_1, sh1_1, b_w_q, 0, b_q, seq,
                      dilations=dil, group_cols=gcols, tn=256)
    gq2 = jnp.tile(b_g_qn[0], 2).reshape(1, LANES)
    gk2 = jnp.tile(b_g_kn, 2).reshape(1, LANES)
    w = B_HEADS * B_HEAD_DIM
    ng = len(B_DILATIONS)
    outs, lses = [], []
    for g, (window, r) in enumerate(B_DILATIONS):
        pos_g = positions.reshape(batch, seq // r, r).transpose(0, 2, 1).reshape(batch, seq)
        o_g, lse_g = _win_attention(qall, g, kvall, g, kvall, ng + g, pos_g, tab, gq2, gk2,
                                    batch, seq, r, window)
        outs.append(o_g)
        lses.append(lse_g)
    x2 = _merge_out(outs, lses, dil, b_w_out, 0, x2, gt1_1, seq)

    h, route, counts = _router(x2, g_ffn[1], sc2_1, sh2_1, moe_router[0], moe_router_b[0], seq)
    share = (2 * m) // N_EXPERTS
    tm = -(-(share * 17 // 32) // 64) * 64
    n_tiles = (2 * m) // tm + N_EXPERTS
    src, n_active, te, tv, p1, p2 = _routing_tables(route, counts, tm, n_tiles)
    hs = _gather_rows(h, src, n_active, tm // 4)
    ys = _moe_experts(hs, te, tv, moe_w1.reshape(moe_w1.shape[1:]), moe_w3.reshape(moe_w3.shape[1:]),
                      moe_w2.reshape(moe_w2.shape[1:]), tm)
    out = _combine(ys, p1, p2, x2, gt2_1, route, seq)
    return out.reshape(batch, seq, d)
```

```python
import functools
import math

import jax
import jax.numpy as jnp
from jax import lax
from jax.experimental import pallas as pl
from jax.experimental.pallas import tpu as pltpu

F32 = jnp.float32
BF16 = jnp.bfloat16
I32 = jnp.int32

EPS = 1e-6
NEG = -1e30
INT_MIN = -(2 ** 31)
LOG2E = 1.0 / math.log(2.0)

A_HEADS, A_KV_HEADS, A_HEAD_DIM = 16, 4, 128
IDX_HEADS, IDX_DIM = 16, 64
TOPK_MAX = 256
B_DILATIONS = ((128, 1), (512, 4), (2048, 16))
B_HEADS, B_HEAD_DIM = 16, 64
N_BUCKETS, MAX_DISTANCE = 32, 2048
N_EXPERTS = 8
LANES = 128

VMEM_LIMIT_BYTES = 56 * 1024 * 1024

_NT = (((1,), (1,)), ((), ()))


def _params(*sem):
    return pltpu.CompilerParams(dimension_semantics=sem, vmem_limit_bytes=VMEM_LIMIT_BYTES)


def _dot(a, b):
    return jnp.dot(a, b, preferred_element_type=F32)


def _dot_nt(a, b):
    return lax.dot_general(a, b, _NT, preferred_element_type=F32)


def _sigmoid(x):
    return 1.0 / (1.0 + jnp.exp(-x))


def _t5_bucket(rel):
    n = jnp.maximum(rel, 0)
    max_exact = N_BUCKETS // 2
    nf = jnp.maximum(n, 1).astype(F32)
    large = max_exact + (jnp.log(nf / max_exact) / math.log(MAX_DISTANCE / max_exact)
                         * (N_BUCKETS - max_exact)).astype(I32)
    large = jnp.minimum(large, N_BUCKETS - 1)
    return jnp.where(n < max_exact, n, large)


def _norm_mod(x, g, sc, sh):
    y = x * lax.rsqrt(jnp.mean(x * x, axis=-1, keepdims=True) + EPS)
    return (y * g) * (1.0 + sc) + sh


def _bias_lookup(tab_ref, heads, bkt):
    rows, cols = bkt.shape
    tabs = [jnp.broadcast_to(tab_ref[h:h + 1, :], (8, LANES)) for h in heads]
    tiles = [[] for _ in tabs]
    for r in range(rows // 8):
        pieces = [[] for _ in tabs]
        for c in range(cols // LANES):
            idx = bkt[r * 8:(r + 1) * 8, c * LANES:(c + 1) * LANES]
            for k, tab in enumerate(tabs):
                pieces[k].append(jnp.take_along_axis(tab, idx, axis=1))
        for k in range(len(tabs)):
            tiles[k].append(pieces[k][0] if len(pieces[k]) == 1 else jnp.concatenate(pieces[k], axis=1))
    return [jnp.concatenate(t, axis=0) for t in tiles]


def _mod_body(c_ref, w_ref, b_ref, o_ref):
    c = c_ref[...]
    cs = c * _sigmoid(c)
    o_ref[...] = _dot(cs.astype(BF16), w_ref[...].astype(BF16)) + b_ref[...]


def _mod_call(c8, w3, layer, b2):
    _, d, n = w3.shape
    tn = 1024
    return pl.pallas_call(
        _mod_body,
        grid=(n // tn,),
        in_specs=[pl.BlockSpec((8, d), lambda j: (0, 0)),
                  pl.BlockSpec((None, d, tn), lambda j: (layer, 0, j)),
                  pl.BlockSpec((None, 1, tn), lambda j: (layer, 0, j))],
        out_specs=pl.BlockSpec((8, tn), lambda j: (0, j)),
        out_shape=jax.ShapeDtypeStruct((8, n), F32),
        compiler_params=_params("arbitrary"),
        name="adaln_mod",
    )(c8, w3, b2.reshape(b2.shape[0], 1, n))


def _nm_body(x_ref, g_ref, sc_ref, sh_ref, w_ref, *rest, has_tail, dilations, group_cols):
    if has_tail:
        wt_ref, o_ref, ot_ref, h_scr = rest
    elif dilations:
        o_ref, h_scr, y_scr = rest
    else:
        o_ref, h_scr = rest

    @pl.when(pl.program_id(1) == 0)
    def _():
        h = _norm_mod(x_ref[...], g_ref[...], sc_ref[0], sh_ref[0]).astype(BF16)
        h_scr[...] = h
        if has_tail:
            ot_ref[...] = _dot(h, wt_ref[...].astype(BF16)).astype(ot_ref.dtype)

    y = _dot(h_scr[...], w_ref[...].astype(BF16))
    if not dilations:
        o_ref[...] = y.astype(o_ref.dtype)
        return
    tm, tn = o_ref.shape
    group = (pl.program_id(1) * tn // group_cols) % len(dilations)
    for k, r in enumerate(dilations):
        @pl.when(group == k)
        def _(r=r):
            if r == 1:
                o_ref[...] = y.astype(o_ref.dtype)
                return
            for c in range(tn // LANES):
                y_scr[c] = y[:, c * LANES:(c + 1) * LANES]
            cls = tm // r
            for rho in range(r):
                for c in range(tn // LANES):
                    o_ref[rho * cls:(rho + 1) * cls, c * LANES:(c + 1) * LANES] = (
                        y_scr[c, pl.ds(rho, cls, stride=r), :].astype(o_ref.dtype))


def _nm_matmul(x, g, sc, sh, w3, layer, n_cols, seq, w_tail=None, dilations=None, group_cols=None,
               tm=2048, tn=512):
    m, d = x.shape
    tm = min(tm, seq)
    nb = seq // tm
    has_tail = w_tail is not None
    assert not dilations or (tm == seq and not has_tail and group_cols % tn == 0)
    in_specs = [pl.BlockSpec((tm, d), lambda i, j: (i, 0), pipeline_mode=pl.Buffered(1)),
                pl.BlockSpec((1, d), lambda i, j: (0, 0)),
                pl.BlockSpec((1, 1, d), lambda i, j: (i // nb, 0, 0)),
                pl.BlockSpec((1, 1, d), lambda i, j: (i // nb, 0, 0)),
                pl.BlockSpec((None, d, tn), lambda i, j: (layer, 0, j))]
    args = [x, g.reshape(1, d), sc.reshape(-1, 1, d), sh.reshape(-1, 1, d), w3]
    out_specs = [pl.BlockSpec((tm, tn), lambda i, j: (i, j))]
    out_shape = [jax.ShapeDtypeStruct((m, n_cols), BF16)]
    if has_tail:
        in_specs.append(pl.BlockSpec((d, LANES), lambda i, j: (0, 0)))
        args.append(w_tail)
        out_specs.append(pl.BlockSpec((tm, LANES), lambda i, j: (i, 0)))
        out_shape.append(jax.ShapeDtypeStruct((m, LANES), BF16))
    scratch = [pltpu.VMEM((tm, d), BF16)]
    if dilations:
        scratch.append(pltpu.VMEM((tn // LANES, tm, LANES), F32))
    res = pl.pallas_call(
        functools.partial(_nm_body, has_tail=has_tail, dilations=dilations, group_cols=group_cols),
        grid=(m // tm, n_cols // tn),
        in_specs=in_specs,
        out_specs=out_specs,
        out_shape=out_shape,
        scratch_shapes=scratch,
        compiler_params=_params("arbitrary", "arbitrary"),
        name="norm_mod_matmul",
    )(*args)
    return res if has_tail else res[0]


def _mmres_body(a_ref, w_ref, x_ref, gt_ref, o_ref):
    o_ref[...] = x_ref[...] + gt_ref[0] * _dot(a_ref[...], w_ref[...].astype(BF16))


def _matmul_residual(a, w3, layer, x, gt, seq, tm=2048, tn=256):
    m, k = a.shape
    d = x.shape[1]
    tm = min(tm, seq)
    nb = seq // tm
    return pl.pallas_call(
        _mmres_body,
        grid=(m // tm, d // tn),
        in_specs=[pl.BlockSpec((tm, k), lambda i, j: (i, 0), pipeline_mode=pl.Buffered(1)),
                  pl.BlockSpec((None, k, tn), lambda i, j: (layer, 0, j)),
                  pl.BlockSpec((tm, tn), lambda i, j: (i, j)),
                  pl.BlockSpec((1, 1, tn), lambda i, j: (i // nb, 0, j))],
        out_specs=pl.BlockSpec((tm, tn), lambda i, j: (i, j)),
        out_shape=jax.ShapeDtypeStruct((m, d), F32),
        compiler_params=_params("arbitrary", "arbitrary"),
        name="matmul_residual",
    )(a, w3, x, gt.reshape(-1, 1, d))


def _nm_swiglu_body(x_ref, g_ref, sc_ref, sh_ref, w1_ref, w3_ref, o_ref, h_scr):
    @pl.when(pl.program_id(1) == 0)
    def _():
        h_scr[...] = _norm_mod(x_ref[...], g_ref[...], sc_ref[0], sh_ref[0]).astype(BF16)

    h = h_scr[...]
    a = _dot(h, w1_ref[...].astype(BF16))
    b = _dot(h, w3_ref[...].astype(BF16))
    o_ref[...] = (a * _sigmoid(a) * b).astype(o_ref.dtype)


def _nm_swiglu(x, g, sc, sh, w1, w3, layer, seq, tm=2048, tf=256):
    m, d = x.shape
    f = w1.shape[2]
    tm = min(tm, seq)
    nb = seq // tm
    return pl.pallas_call(
        _nm_swiglu_body,
        grid=(m // tm, f // tf),
        in_specs=[pl.BlockSpec((tm, d), lambda i, j: (i, 0), pipeline_mode=pl.Buffered(1)),
                  pl.BlockSpec((1, d), lambda i, j: (0, 0)),
                  pl.BlockSpec((1, 1, d), lambda i, j: (i // nb, 0, 0)),
                  pl.BlockSpec((1, 1, d), lambda i, j: (i // nb, 0, 0)),
                  pl.BlockSpec((None, d, tf), lambda i, j: (layer, 0, j)),
                  pl.BlockSpec((None, d, tf), lambda i, j: (layer, 0, j))],
        out_specs=pl.BlockSpec((tm, tf), lambda i, j: (i, j)),
        out_shape=jax.ShapeDtypeStruct((m, f), BF16),
        scratch_shapes=[pltpu.VMEM((tm, d), BF16)],
        compiler_params=_params("arbitrary", "arbitrary"),
        name="norm_mod_swiglu_up",
    )(x, g.reshape(1, d), sc.reshape(-1, 1, d), sh.reshape(-1, 1, d), w1, w3)


def _dsa_body(q_ref, qi_ref, k_ref, v_ref, tq_ref, tk_ref, pr_ref, pc_ref, tab_ref, gq_ref, gk_ref,
              o_ref,
              kn_scr, vt_scr, kke_scr, kko_scr, qst_scr, wib_scr, key_scr, qn_scr, m_scr, l_scr, acc_scr,
              s_buf, cm_buf,
              *, topk, tq, seq):
    i = pl.program_id(1)
    nc = i + 1
    grp = A_HEADS // A_KV_HEADS
    hd = A_HEAD_DIM
    lane = lax.broadcasted_iota(I32, (1, LANES), 1)
    krow = lax.broadcasted_iota(I32, (tq, tq), 0)
    qcol = lax.broadcasted_iota(I32, (tq, tq), 1)

    @pl.when(i == 0)
    def _prepare_keys():
        gk = gk_ref[...]

        def body(r, carry):
            rows = pl.ds(pl.multiple_of(r * tq, tq), tq)
            for kh in range(A_KV_HEADS):
                cols = slice(kh * hd, (kh + 1) * hd)
                kb = k_ref[rows, cols].astype(F32)
                ms = jnp.mean(kb * kb, axis=-1, keepdims=True)
                kn_scr[rows, cols] = ((kb * lax.rsqrt(ms + EPS)) * gk).astype(BF16)
                vt_scr[r, cols, :] = v_ref[rows, cols].astype(F32).T.astype(BF16)
            t = tk_ref[rows, :].astype(F32)
            kke_scr[rows, :] = jnp.where(lane < IDX_DIM, t, 0.0).astype(BF16)
            kko_scr[rows, :] = jnp.where(lane >= IDX_DIM, pltpu.roll(t, IDX_DIM, 1), 0.0).astype(BF16)
            return carry

        lax.fori_loop(0, seq // tq, body, 0)

    for j in range(IDX_HEADS // 2):
        qst_scr[j * tq:(j + 1) * tq, :] = qi_ref[:, j * LANES:(j + 1) * LANES]
    w_scale = (IDX_DIM ** -0.5) * (IDX_HEADS ** -0.5)
    wib_scr[...] = tq_ref[...].astype(F32).T[IDX_DIM:IDX_DIM + IDX_HEADS, :] * w_scale

    def idx_body(c, carry):
        rows = pl.ds(pl.multiple_of(c * tq, tq), tq)
        qst = qst_scr[...]
        re = _dot_nt(kke_scr[rows, :], qst)
        ro = _dot_nt(kko_scr[rows, :], qst)
        acc = jnp.zeros((tq, tq), F32)
        for j in range(IDX_HEADS // 2):
            acc = acc + jnp.maximum(re[:, j * tq:(j + 1) * tq], 0.0) * wib_scr[2 * j:2 * j + 1, :]
            acc = acc + jnp.maximum(ro[:, j * tq:(j + 1) * tq], 0.0) * wib_scr[2 * j + 1:2 * j + 2, :]
        bits = pltpu.bitcast(acc, I32)
        key = bits ^ ((bits >> 31) & 0x7FFFFFFF)
        causal = (c < i) | (krow <= qcol)
        key_scr[c] = jnp.where(causal, key, INT_MIN)
        return carry

    lax.fori_loop(0, nc, idx_body, 0)

    def bit_body(bi, t_u):
        cand_u = t_u | lax.shift_left(jnp.int32(1), 31 - bi)
        cand_s = cand_u ^ INT_MIN

        def cnt_body(c, cnt):
            ge = jnp.where(key_scr[c] >= cand_s, 1, 0)
            return cnt + jnp.sum(ge.reshape(tq // 8, 8, tq), axis=0)

        cnt = lax.fori_loop(0, nc, cnt_body, jnp.zeros((8, tq), I32))
        total = jnp.sum(cnt, axis=0, keepdims=True)
        return jnp.where(total >= topk, cand_u, t_u)

    nbits = jnp.where(nc * tq > topk, 32, 0)
    t_u = lax.fori_loop(0, nbits, bit_body, jnp.zeros((1, tq), I32))
    thr = t_u ^ INT_MIN

    qpos = pr_ref[0, i]

    def bkt_body(c, carry):
        rows = pl.ds(pl.multiple_of(c * tq, tq), tq)
        bkt = _t5_bucket(qpos - pc_ref[rows, :])
        causal = (c < i) | (krow <= qcol)
        sel = (key_scr[c] >= thr) & causal
        key_scr[c] = jnp.where(sel, bkt, N_BUCKETS)
        return carry

    lax.fori_loop(0, nc, bkt_body, 0)

    gq = gq_ref[...]
    scale = (hd ** -0.5) * LOG2E
    for g in range(A_KV_HEADS):
        for hh in range(grp):
            h = g * grp + hh
            qh = q_ref[:, h * hd:(h + 1) * hd].astype(F32)
            ms = jnp.mean(qh * qh, axis=-1, keepdims=True)
            qn_scr[hh * tq:(hh + 1) * tq, :] = (((qh * lax.rsqrt(ms + EPS)) * gq) * scale).astype(BF16)
        m_scr[...] = jnp.full(m_scr.shape, NEG, F32)
        l_scr[...] = jnp.zeros(l_scr.shape, F32)
        acc_scr[...] = jnp.zeros(acc_scr.shape, F32)

        def scores(c, g=g):
            rows = pl.ds(pl.multiple_of(c * tq, tq), tq)
            s = _dot_nt(kn_scr[rows, g * hd:(g + 1) * hd], qn_scr[...])
            bias = _bias_lookup(tab_ref, range(g * grp, (g + 1) * grp), key_scr[c])
            s = jnp.concatenate([s[:, hh * tq:(hh + 1) * tq] + bias[hh] for hh in range(grp)], axis=1)
            s_buf[c % 2] = s
            cm_buf[c % 2] = jnp.max(s, axis=0, keepdims=True)

        def accumulate(c, g=g):
            s = s_buf[c % 2]
            m_old = m_scr[...]
            m_new = jnp.maximum(m_old, cm_buf[c % 2])
            p = jnp.exp2(s - m_new)
            alpha = jnp.exp2(m_old - m_new)
            l_scr[...] = alpha * l_scr[...] + jnp.sum(p, axis=0, keepdims=True)
            acc_scr[...] = alpha * acc_scr[...] + _dot(vt_scr[c, g * hd:(g + 1) * hd, :], p.astype(BF16))
            m_scr[...] = m_new

        def att_body(c, carry):
            accumulate(c)
            scores(c + 1)
            return carry

        scores(0)
        lax.fori_loop(0, nc - 1, att_body, 0)
        accumulate(nc - 1)
        o = acc_scr[...] * (1.0 / l_scr[...])
        for hh in range(grp):
            h = g * grp + hh
            o_ref[:, h * hd:(h + 1) * hd] = o[:, hh * tq:(hh + 1) * tq].T.astype(o_ref.dtype)


def _dsa_attention(qkv, tail, pos, tab, gq, gk, batch, seq, tq=256):
    tq = min(tq, seq)
    nq = seq // tq
    a_q = A_HEADS * A_HEAD_DIM
    a_kv = A_KV_HEADS * A_HEAD_DIM
    a_qi = IDX_HEADS * IDX_DIM
    topk = min(TOPK_MAX, seq // 4)
    grp = A_HEADS // A_KV_HEADS
    body = functools.partial(_dsa_body, topk=topk, tq=tq, seq=seq)
    return pl.pallas_call(
        body,
        grid=(batch, nq),
        in_specs=[
            pl.BlockSpec((tq, a_q), lambda b, i: (b * nq + i, 0)),
            pl.BlockSpec((tq, a_qi), lambda b, i: (b * nq + i, (a_q + 2 * a_kv) // a_qi)),
            pl.BlockSpec((seq, a_kv), lambda b, i: (b, a_q // a_kv)),
            pl.BlockSpec((seq, a_kv), lambda b, i: (b, a_q // a_kv + 1)),
            pl.BlockSpec((tq, LANES), lambda b, i: (b * nq + i, 0)),
            pl.BlockSpec((seq, LANES), lambda b, i: (b, 0)),
            pl.BlockSpec((1, nq, 1, tq), lambda b, i: (b, 0, 0, 0)),
            pl.BlockSpec((seq, 1), lambda b, i: (b, 0)),
            pl.BlockSpec((A_HEADS, LANES), lambda b, i: (0, 0)),
            pl.BlockSpec((1, A_HEAD_DIM), lambda b, i: (0, 0)),
            pl.BlockSpec((1, A_HEAD_DIM), lambda b, i: (0, 0)),
        ],
        out_specs=pl.BlockSpec((tq, a_q), lambda b, i: (b * nq + i, 0)),
        out_shape=jax.ShapeDtypeStruct((batch * seq, a_q), BF16),
        scratch_shapes=[
            pltpu.VMEM((seq, a_kv), BF16),
            pltpu.VMEM((nq, a_kv, tq), BF16),
            pltpu.VMEM((seq, LANES), BF16),
            pltpu.VMEM((seq, LANES), BF16),
            pltpu.VMEM((IDX_HEADS // 2 * tq, LANES), BF16),
            pltpu.VMEM((IDX_HEADS, tq), F32),
            pltpu.VMEM((nq, tq, tq), I32),
            pltpu.VMEM((grp * tq, A_HEAD_DIM), BF16),
            pltpu.VMEM((1, grp * tq), F32),
            pltpu.VMEM((1, grp * tq), F32),
            pltpu.VMEM((A_HEAD_DIM, grp * tq), F32),
            pltpu.VMEM((2, tq, grp * tq), F32),
            pltpu.VMEM((2, 1, grp * tq), F32),
        ],
        compiler_params=_params("arbitrary", "arbitrary"),
        name="dsa_attention",
    )(qkv, qkv, qkv, qkv, tail, tail, pos.reshape(batch, nq, 1, tq), pos.reshape(batch * seq, 1),
      tab, gq, gk)


def _win_body(*refs, cls_len, wk, use_prev, tq):
    if use_prev:
        (q_ref, kc_ref, vc_ref, kp_ref, vp_ref, pr_ref, pcc_ref, pcp_ref, tab_ref, gq_ref, gk_ref,
         o_ref, lse_ref, kn_scr, vt_scr, qn_scr, bkt_scr, bias_scr, lse_scr) = refs
    else:
        (q_ref, kc_ref, vc_ref, pr_ref, pcc_ref, tab_ref, gq_ref, gk_ref,
         o_ref, lse_ref, kn_scr, vt_scr, qn_scr, bkt_scr, bias_scr, lse_scr) = refs
    t = pl.program_id(1)
    hd = B_HEAD_DIM
    pw = 2 * hd
    blk = LANES
    shift = cls_len.bit_length() - 1
    koff = blk if use_prev else 0
    win = 2 * blk if cls_len > blk else blk
    lo = lax.broadcasted_iota(I32, (1, pw), 1) < hd
    lo_rows = lax.broadcasted_iota(I32, (pw, 1), 0) < hd
    gq = gq_ref[...]
    gk = gk_ref[...]
    scale = (hd ** -0.5) * LOG2E

    def pair_norm(x, g):
        sq = x * x
        ms_lo = jnp.sum(jnp.where(lo, sq, 0.0), axis=-1, keepdims=True) * (1.0 / hd)
        ms_hi = jnp.sum(jnp.where(lo, 0.0, sq), axis=-1, keepdims=True) * (1.0 / hd)
        inv = jnp.where(lo, lax.rsqrt(ms_lo + EPS), lax.rsqrt(ms_hi + EPS))
        return (x * inv) * g

    for p in range(B_HEADS // 2):
        cols = slice(p * pw, (p + 1) * pw)
        qn_scr[:, cols] = (pair_norm(q_ref[:, cols].astype(F32), gq) * scale).astype(BF16)
        if use_prev:
            kn_scr[0:blk, cols] = pair_norm(kp_ref[:, cols].astype(F32), gk).astype(BF16)
            vt_scr[p, :, 0:blk] = vp_ref[:, cols].astype(F32).T.astype(BF16)
        kn_scr[koff:koff + tq, cols] = pair_norm(kc_ref[:, cols].astype(F32), gk).astype(BF16)
        vt_scr[p, :, koff:koff + tq] = vc_ref[:, cols].astype(F32).T.astype(BF16)

    lse_scr[...] = jnp.zeros(lse_scr.shape, F32)
    qpos_all = pr_ref[0]
    for j in range(tq // blk):
        qrows = slice(j * blk, (j + 1) * blk)
        own = koff + j * blk
        k0 = own - blk if (win > blk and own >= blk) else own
        if use_prev and k0 < koff:
            kpos = jnp.concatenate([pcp_ref[k0:koff, :], pcc_ref[0:k0 + win - koff, :]], axis=0)
        else:
            kpos = pcc_ref[k0 - koff:k0 - koff + win, :]
        fq = t * tq + j * blk + lax.broadcasted_iota(I32, (1, blk), 1)
        fk = t * tq + (k0 - koff) + lax.broadcasted_iota(I32, (win, 1), 0)
        same = ((fq + cls_len) >> shift) == ((fk + cls_len) >> shift)
        dist = (fq & (cls_len - 1)) - (fk & (cls_len - 1))
        ok = same & (dist >= 0) & (dist <= wk)
        bkt_scr[...] = jnp.where(ok, _t5_bucket(qpos_all[:, qrows] - kpos), N_BUCKETS)

        def bias_rows(r8, carry):
            rows = pl.ds(pl.multiple_of(r8 * 8, 8), 8)
            idx = bkt_scr[rows, :]
            for h in range(B_HEADS):
                tab = jnp.broadcast_to(tab_ref[h:h + 1, :], (8, LANES))
                bias_scr[h // 2, rows, (h % 2) * blk:(h % 2 + 1) * blk] = jnp.take_along_axis(tab, idx, axis=1)
            return carry

        lax.fori_loop(0, win // 8, bias_rows, 0, unroll=8)

        for p in range(B_HEADS // 2):
            cols = slice(p * pw, (p + 1) * pw)
            qp = qn_scr[qrows, cols]
            zero = jnp.zeros_like(qp)
            heads = (jnp.where(lo, qp, zero), jnp.where(lo, zero, qp))
            kw = kn_scr[k0:k0 + win, cols]
            vtw = vt_scr[p, :, k0:k0 + win]
            hp = 2 * blk // win
            outs = []
            for c in range(2 // hp):
                q2 = heads[c] if hp == 1 else jnp.concatenate(heads, axis=0)
                s = _dot_nt(kw, q2) + bias_scr[p, :, c * hp * blk:(c + 1) * hp * blk]
                m = jnp.max(s, axis=0, keepdims=True)
                e = jnp.exp2(s - m)
                l = jnp.sum(e, axis=0, keepdims=True)
                ot = _dot(vtw, e.astype(BF16)) * (1.0 / l)
                lse = m + jnp.log(l) * LOG2E
                for k in range(hp):
                    h = 2 * p + c * hp + k
                    outs.append(ot[:, k * blk:(k + 1) * blk])
                    lse_scr[h:h + 1, :] = lse[:, k * blk:(k + 1) * blk]
            o_ref[qrows, cols] = jnp.where(lo_rows, outs[0], outs[1]).T.astype(o_ref.dtype)
        lse_ref[qrows, :] = lse_scr[...].T


def _win_attention(q_arr, q_col, k_arr, k_col, v_arr, v_col, pos, tab, gq2, gk2,
                   batch, seq, dilation, window, tq=512):
    cls_len = seq // dilation
    wk = window // dilation
    assert cls_len & (cls_len - 1) == 0 and wk <= LANES
    tq = min(tq, seq)
    use_prev = cls_len > tq
    assert use_prev or tq % cls_len == 0
    nt = seq // tq
    w = B_HEADS * B_HEAD_DIM
    sub = tq // LANES
    pos_col = pos.reshape(batch * seq, 1)
    pos_row = pos.reshape(batch * nt, 1, tq)
    prev = lambda b, t: jnp.maximum((b * nt + t) * sub - 1, 0)
    in_specs = [pl.BlockSpec((tq, w), lambda b, t: (b * nt + t, q_col)),
                pl.BlockSpec((tq, w), lambda b, t: (b * nt + t, k_col)),
                pl.BlockSpec((tq, w), lambda b, t: (b * nt + t, v_col))]
    args = [q_arr, k_arr, v_arr]
    if use_prev:
        in_specs += [pl.BlockSpec((LANES, w), lambda b, t: (prev(b, t), k_col)),
                     pl.BlockSpec((LANES, w), lambda b, t: (prev(b, t), v_col))]
        args += [k_arr, v_arr]
    in_specs += [pl.BlockSpec((1, 1, tq), lambda b, t: (b * nt + t, 0, 0)),
                 pl.BlockSpec((tq, 1), lambda b, t: (b * nt + t, 0))]
    args += [pos_row, pos_col]
    if use_prev:
        in_specs.append(pl.BlockSpec((LANES, 1), lambda b, t: (prev(b, t), 0)))
        args.append(pos_col)
    in_specs += [pl.BlockSpec((B_HEADS, LANES), lambda b, t: (0, 0)),
                 pl.BlockSpec((1, LANES), lambda b, t: (0, 0)),
                 pl.BlockSpec((1, LANES), lambda b, t: (0, 0))]
    args += [tab, gq2, gk2]
    nk = tq + (LANES if use_prev else 0)
    win = 2 * LANES if cls_len > LANES else LANES
    body = functools.partial(_win_body, cls_len=cls_len, wk=wk, use_prev=use_prev, tq=tq)
    return pl.pallas_call(
        body,
        grid=(batch, nt),
        in_specs=in_specs,
        out_specs=[pl.BlockSpec((tq, w), lambda b, t: (b * nt + t, 0)),
                   pl.BlockSpec((tq, LANES), lambda b, t: (b * nt + t, 0))],
        out_shape=[jax.ShapeDtypeStruct((batch * seq, w), BF16),
                   jax.ShapeDtypeStruct((batch * seq, LANES), F32)],
        scratch_shapes=[pltpu.VMEM((nk, w), BF16),
                        pltpu.VMEM((B_HEADS // 2, LANES, nk), BF16),
                        pltpu.VMEM((tq, w), BF16),
                        pltpu.VMEM((win, LANES), I32),
                        pltpu.VMEM((B_HEADS // 2, win, 2 * LANES), F32),
                        pltpu.VMEM((LANES, LANES), F32)],
        compiler_params=_params("arbitrary", "arbitrary"),
        name="dilated_attention",
    )(*args)


def _merge_body(*refs, dilations):
    ng = len(dilations)
    o_refs, l_refs = refs[:ng], refs[ng:2 * ng]
    e_ref, w_ref, x_ref, gt_ref, out_ref, a_scr, o_scr, l_scr = refs[2 * ng:]
    tm, k = a_scr.shape

    @pl.when(pl.program_id(1) == 0)
    def _():
        for g, r in enumerate(dilations):
            cls = tm // r
            for rho in range(r):
                l_scr[g, pl.ds(rho, cls, stride=r), :] = l_refs[g][0, rho]
                for c in range(k // LANES):
                    o_scr[g, c, pl.ds(rho, cls, stride=r), :] = (
                        o_refs[g][0, rho, :, c * LANES:(c + 1) * LANES].astype(F32))
        ls = [l_scr[g] for g in range(ng)]
        m = functools.reduce(jnp.maximum, ls)
        ws = [jnp.exp2(l - m) for l in ls]
        inv = 1.0 / functools.reduce(lambda a, b: a + b, ws)
        wides = []
        for wg in ws:
            wn = wg * inv
            hi = wn.astype(BF16)
            lo = (wn - hi.astype(F32)).astype(BF16)
            wides.append(_dot(hi, e_ref[...]) + _dot(lo, e_ref[...]))
        for c in range(k // LANES):
            cols = slice(c * LANES, (c + 1) * LANES)
            num = wides[0][:, cols] * o_scr[0, c]
            for g in range(1, ng):
                num = num + wides[g][:, cols] * o_scr[g, c]
            a_scr[:, cols] = num.astype(BF16)

    out_ref[...] = x_ref[...] + gt_ref[0] * _dot(a_scr[...], w_ref[...].astype(BF16))


def _merge_out(os_, ls_, dilations, w3, layer, x, gt, seq, tm=512, tn=512):
    m, k = os_[0].shape
    d = x.shape[1]
    nb = seq // tm
    batch = m // seq
    head = jnp.arange(k, dtype=I32)[None, :] // B_HEAD_DIM
    expand = (jnp.arange(LANES, dtype=I32)[:, None] == head).astype(BF16)
    ng = len(dilations)
    o_specs = [pl.BlockSpec((1, r, tm // r, k), lambda i, j: (i // nb, 0, i % nb, 0)) for r in dilations]
    l_specs = [pl.BlockSpec((1, r, tm // r, LANES), lambda i, j: (i // nb, 0, i % nb, 0)) for r in dilations]
    o_args = [o.reshape(batch, r, seq // r, k) for o, r in zip(os_, dilations)]
    l_args = [l.reshape(batch, r, seq // r, LANES) for l, r in zip(ls_, dilations)]
    return pl.pallas_call(
        functools.partial(_merge_body, dilations=tuple(dilations)),
        grid=(m // tm, d // tn),
        in_specs=o_specs + l_specs + [
            pl.BlockSpec((LANES, k), lambda i, j: (0, 0)),
            pl.BlockSpec((None, k, tn), lambda i, j: (layer, 0, j)),
            pl.BlockSpec((tm, tn), lambda i, j: (i, j)),
            pl.BlockSpec((1, 1, tn), lambda i, j: (i // nb, 0, j))],
        out_specs=pl.BlockSpec((tm, tn), lambda i, j: (i, j)),
        out_shape=jax.ShapeDtypeStruct((m, d), F32),
        scratch_shapes=[pltpu.VMEM((tm, k), BF16),
                        pltpu.VMEM((ng, k // LANES, tm, LANES), F32),
                        pltpu.VMEM((ng, tm, LANES), F32)],
        compiler_params=_params("arbitrary", "arbitrary"),
        name="merge_out_proj",
    )(*o_args, *l_args, expand, w3, x, gt.reshape(-1, 1, d))


def _router_body(x_ref, g_ref, sc_ref, sh_ref, rh_ref, rl_ref, rb_ref, h_ref, rt_ref):
    h = _norm_mod(x_ref[...], g_ref[...], sc_ref[0], sh_ref[0])
    h_ref[...] = h
    hh = h.astype(BF16)
    hl = (h - hh.astype(F32)).astype(BF16)
    logits = _dot(hh, rh_ref[...]) + _dot(hl, rh_ref[...]) + _dot(hh, rl_ref[...]) + rb_ref[...]
    lane = lax.broadcasted_iota(I32, logits.shape, 1)
    logits = jnp.where(lane < N_EXPERTS, logits, NEG)
    v1 = jnp.max(logits, axis=1, keepdims=True)
    i1 = jnp.min(jnp.where(logits == v1, lane, LANES), axis=1, keepdims=True)
    rest = jnp.where(lane == i1, NEG, logits)
    v2 = jnp.max(rest, axis=1, keepdims=True)
    i2 = jnp.min(jnp.where(rest == v2, lane, LANES), axis=1, keepdims=True)
    e = jnp.exp(v2 - v1)
    g1 = 1.0 / (1.0 + e)
    g2 = e * g1
    rt_ref[...] = jnp.where(lane == 0, i1.astype(F32),
                            jnp.where(lane == 1, i2.astype(F32),
                                      jnp.where(lane == 2, g1, jnp.where(lane == 3, g2, 0.0))))


def _router(x, g, sc, sh, rw, rb, seq, tm=512):
    m, d = x.shape
    nb = seq // tm
    ne = rw.shape[1]
    rw_p = jnp.zeros((d, LANES), F32).at[:, :ne].set(rw)
    rh = rw_p.astype(BF16)
    rl = (rw_p - rh.astype(F32)).astype(BF16)
    rb_p = jnp.zeros((1, LANES), F32).at[0, :ne].set(rb)
    return pl.pallas_call(
        _router_body,
        grid=(m // tm,),
        in_specs=[pl.BlockSpec((tm, d), lambda i: (i, 0)),
                  pl.BlockSpec((1, d), lambda i: (0, 0)),
                  pl.BlockSpec((1, 1, d), lambda i: (i // nb, 0, 0)),
                  pl.BlockSpec((1, 1, d), lambda i: (i // nb, 0, 0)),
                  pl.BlockSpec((d, LANES), lambda i: (0, 0)),
                  pl.BlockSpec((d, LANES), lambda i: (0, 0)),
                  pl.BlockSpec((1, LANES), lambda i: (0, 0))],
        out_specs=[pl.BlockSpec((tm, d), lambda i: (i, 0)),
                   pl.BlockSpec((tm, LANES), lambda i: (i, 0))],
        out_shape=[jax.ShapeDtypeStruct((m, d), F32),
                   jax.ShapeDtypeStruct((m, LANES), F32)],
        compiler_params=_params("arbitrary"),
        name="router_top2",
    )(x, g.reshape(1, d), sc.reshape(-1, 1, d), sh.reshape(-1, 1, d), rh, rl, rb_p)


def _row_copy(src_hbm, idx, buf, r, sem):
    return pltpu.make_async_copy(src_hbm.at[pl.ds(idx, 1), :], buf.at[pl.ds(r, 1), :], sem)


def _gather_body(src_ref, nact_ref, h_hbm, o_ref, buf, sem, *, rows):
    i = pl.program_id(0)
    n_active = nact_ref[0]

    def issue(step):
        slot = step % 2

        def body(k, carry):
            for p in range(2):
                r = 2 * k + p
                _row_copy(h_hbm, src_ref[step * rows + r], buf.at[slot], r, sem.at[slot]).start(priority=p)
            return carry

        lax.fori_loop(0, rows // 2, body, 0, unroll=4)

    @pl.when(i == 0)
    def _():
        issue(i)

    @pl.when((i + 1) * rows < n_active)
    def _():
        issue(i + 1)

    slot = i % 2

    @pl.when(i * rows < n_active)
    def _():
        pltpu.make_async_copy(h_hbm.at[pl.ds(0, rows), :], buf.at[slot], sem.at[slot]).wait()
        o_ref[...] = buf[slot].astype(o_ref.dtype)

    @pl.when(i * rows >= n_active)
    def _():
        o_ref[...] = jnp.zeros(o_ref.shape, o_ref.dtype)


def _gather_rows(h, src, n_active, rows):
    r_total = src.shape[0]
    d = h.shape[1]
    return pl.pallas_call(
        functools.partial(_gather_body, rows=rows),
        grid_spec=pltpu.PrefetchScalarGridSpec(
            num_scalar_prefetch=2,
            grid=(r_total // rows,),
            in_specs=[pl.BlockSpec(memory_space=pl.ANY)],
            out_specs=pl.BlockSpec((rows, d), lambda i, s, n: (i, 0)),
            scratch_shapes=[pltpu.VMEM((2, rows, d), F32), pltpu.SemaphoreType.DMA((2,))],
        ),
        out_shape=jax.ShapeDtypeStruct((r_total, d), BF16),
        compiler_params=_params("arbitrary"),
        name="moe_dispatch_gather",
    )(src, n_active, h)


def _moe_body(te_ref, tv_ref, hs_ref, w1_ref, w3_ref, w2_ref, o_ref, *, ncol):
    t = pl.program_id(0)
    f = pl.program_id(1)
    valid = tv_ref[t]
    d = o_ref.shape[1]
    cw = d // ncol

    @pl.when(valid > 0)
    def _():
        h = hs_ref[...]
        a = _dot(h, w1_ref[...].astype(BF16))
        b = _dot(h, w3_ref[...].astype(BF16))
        u = (a * _sigmoid(a) * b).astype(BF16)
        w2 = w2_ref[...].astype(BF16)

        @pl.when(f == 0)
        def _():
            for n in range(ncol):
                o_ref[:, n * cw:(n + 1) * cw] = _dot(u, w2[:, n * cw:(n + 1) * cw])

        @pl.when(f > 0)
        def _():
            for n in range(ncol):
                o_ref[:, n * cw:(n + 1) * cw] += _dot(u, w2[:, n * cw:(n + 1) * cw])

    @pl.when((valid == 0) & (f == 0))
    def _():
        o_ref[...] = jnp.zeros(o_ref.shape, o_ref.dtype)


def _moe_experts(hs, tile_expert, tile_valid, w1, w3, w2, tm, tf=256, ncol=4):
    r_total, d = hs.shape
    n_tiles = r_total // tm
    nf = w1.shape[2] // tf

    def w13_map(t, f, te, tv):
        return (te[t], 0, jnp.where(tv[t] > 0, f, nf - 1))

    def w2_map(t, f, te, tv):
        return (te[t], jnp.where(tv[t] > 0, f, nf - 1), 0)

    return pl.pallas_call(
        functools.partial(_moe_body, ncol=ncol),
        grid_spec=pltpu.PrefetchScalarGridSpec(
            num_scalar_prefetch=2,
            grid=(n_tiles, nf),
            in_specs=[pl.BlockSpec((tm, d), lambda t, f, te, tv: (t, 0)),
                      pl.BlockSpec((None, d, tf), w13_map),
                      pl.BlockSpec((None, d, tf), w13_map),
                      pl.BlockSpec((None, tf, d), w2_map)],
            out_specs=pl.BlockSpec((tm, d), lambda t, f, te, tv: (t, 0)),
        ),
        out_shape=jax.ShapeDtypeStruct((r_total, d), F32),
        compiler_params=_params("arbitrary", "arbitrary"),
        name="moe_experts",
    )(tile_expert, tile_valid, hs, w1, w3, w2)


def _combine_body(p1_ref, p2_ref, y_hbm, x_ref, gt_ref, rt_ref, o_ref, buf_a, buf_b, sem, *, rows):
    base = pl.program_id(0) * rows

    def issue(r, carry):
        _row_copy(y_hbm, p1_ref[base + r], buf_a, r, sem).start(priority=0)
        _row_copy(y_hbm, p2_ref[base + r], buf_b, r, sem).start(priority=1)
        return carry

    lax.fori_loop(0, rows, issue, 0, unroll=4)
    pltpu.make_async_copy(y_hbm.at[pl.ds(0, rows), :], buf_a, sem).wait()
    pltpu.make_async_copy(y_hbm.at[pl.ds(0, rows), :], buf_b, sem).wait()
    rt = rt_ref[...]
    g1 = rt[:, 2:3]
    g2 = rt[:, 3:4]
    o_ref[...] = x_ref[...] + gt_ref[0] * (g1 * buf_a[...] + g2 * buf_b[...])


def _combine(ys, p1, p2, x, gt, route, seq, rows=256):
    m, d = x.shape
    nb = seq // rows
    return pl.pallas_call(
        functools.partial(_combine_body, rows=rows),
        grid_spec=pltpu.PrefetchScalarGridSpec(
            num_scalar_prefetch=2,
            grid=(m // rows,),
            in_specs=[pl.BlockSpec(memory_space=pl.ANY),
                      pl.BlockSpec((rows, d), lambda i, a, b: (i, 0)),
                      pl.BlockSpec((1, 1, d), lambda i, a, b: (i // nb, 0, 0)),
                      pl.BlockSpec((rows, LANES), lambda i, a, b: (i, 0))],
            out_specs=pl.BlockSpec((rows, d), lambda i, a, b: (i, 0)),
            scratch_shapes=[pltpu.VMEM((rows, d), F32), pltpu.VMEM((rows, d), F32),
                            pltpu.SemaphoreType.DMA(())],
        ),
        out_shape=jax.ShapeDtypeStruct((m, d), F32),
        compiler_params=_params("arbitrary"),
        name="moe_combine",
    )(p1, p2, ys, x, gt.reshape(-1, 1, d), route)


def _routing_tables(route, tm, n_tiles):
    t = route.shape[0]
    experts = route[:, :2].astype(I32).reshape(-1)
    onehot = (experts[:, None] == jnp.arange(N_EXPERTS, dtype=I32)[None, :]).astype(I32)
    csum = jnp.cumsum(onehot, axis=0)
    rank = jnp.sum(csum * onehot, axis=1) - 1
    counts = csum[-1]
    tiles = (counts + tm - 1) // tm
    tend = jnp.cumsum(tiles)
    tstart = tend - tiles
    slot = tstart[experts] * tm + rank
    token = jnp.arange(2 * t, dtype=I32) // 2
    src = jnp.zeros((n_tiles * tm,), I32).at[slot].set(token)
    tile_id = jnp.arange(n_tiles, dtype=I32)
    te = jnp.sum((tile_id[:, None] >= tend[None, :]).astype(I32), axis=1)
    active = tile_id < tend[-1]
    last_e = jnp.sum((tend[-1] - 1 >= tend).astype(I32))
    te = jnp.where(active, te, last_e)
    tv = jnp.where(active, jnp.clip(counts[te] - (tile_id - tstart[te]) * tm, 0, tm), 0)
    slots = slot.reshape(t, 2)
    n_active = (tend[-1] * tm).astype(I32).reshape(1)
    return src, n_active, te.astype(I32), tv.astype(I32), slots[:, 0], slots[:, 1]


def _bias_table(rel_bias):
    h = rel_bias.shape[1]
    return jnp.full((h, LANES), NEG, F32).at[:, :N_BUCKETS].set(rel_bias.T * LOG2E)


def kernel(x, c, positions, rel_bias, w_mod, b_mod, g_attn, g_ffn, a_w_in, a_w_out, a_g_qn, a_g_kn,
           kv_w_mod, kv_b_mod, kv_g, kv_w, b_g_kn, b_w_q, b_w_out, b_g_qn, ffn_w1, ffn_w3, ffn_w2,
           moe_router, moe_router_b, moe_w1, moe_w3, moe_w2):
    batch, seq, d = x.shape
    m = batch * seq
    x2 = x.reshape(m, d)
    positions = positions.astype(I32)

    c8 = jnp.zeros((8, d), F32).at[:batch].set(c)
    mod0 = _mod_call(c8, w_mod, 0, b_mod)[:batch]
    mod1 = _mod_call(c8, w_mod, 1, b_mod)[:batch]
    kvm = _mod_call(c8, kv_w_mod[None], 0, kv_b_mod[None])[:batch]
    sh1_0, sc1_0, gt1_0, sh2_0, sc2_0, gt2_0 = jnp.split(mod0, 6, axis=-1)
    sh1_1, sc1_1, gt1_1, sh2_1, sc2_1, gt2_1 = jnp.split(mod1, 6, axis=-1)
    kv_sh, kv_sc = jnp.split(kvm, 2, axis=-1)

    tab = _bias_table(rel_bias)

    a_main = A_HEADS * A_HEAD_DIM + 2 * A_KV_HEADS * A_HEAD_DIM + IDX_HEADS * IDX_DIM
    n_tail = IDX_DIM + IDX_HEADS
    w_tail = jnp.zeros((d, LANES), F32).at[:, :n_tail].set(a_w_in[0, :, a_main:a_main + n_tail])
    qkv, tail = _nm_matmul(x2, g_attn[0], sc1_0, sh1_0, a_w_in, 0, a_main, seq, w_tail=w_tail)
    attn = _dsa_attention(qkv, tail, positions, tab,
                          a_g_qn[0].reshape(1, -1), a_g_kn[0].reshape(1, -1), batch, seq)
    x2 = _matmul_residual(attn, a_w_out, 0, x2, gt1_0, seq, tn=512)
    u = _nm_swiglu(x2, g_ffn[0], sc2_0, sh2_0, ffn_w1, ffn_w3, 0, seq)
    x2 = _matmul_residual(u, ffn_w2, 0, x2, gt2_0, seq)

    b_q = len(B_DILATIONS) * B_HEADS * B_HEAD_DIM
    dil = tuple(r for _, r in B_DILATIONS)
    gcols = B_HEADS * B_HEAD_DIM
    kvall = _nm_matmul(x2, kv_g, kv_sc, kv_sh, kv_w[None], 0, 2 * b_q, seq,
                       dilations=dil, group_cols=gcols, tn=256)
    qall = _nm_matmul(x2, g_attn[1], sc1_1, sh1_1, b_w_q, 0, b_q, seq,
                      dilations=dil, group_cols=gcols, tn=256)
    gq2 = jnp.tile(b_g_qn[0], 2).reshape(1, LANES)
    gk2 = jnp.tile(b_g_kn, 2).reshape(1, LANES)
    w = B_HEADS * B_HEAD_DIM
    ng = len(B_DILATIONS)
    outs, lses = [], []
    for g, (window, r) in enumerate(B_DILATIONS):
        pos_g = positions.reshape(batch, seq // r, r).transpose(0, 2, 1).reshape(batch, seq)
        o_g, lse_g = _win_attention(qall, g, kvall, g, kvall, ng + g, pos_g, tab, gq2, gk2,
                                    batch, seq, r, window)
        outs.append(o_g)
        lses.append(lse_g)
    x2 = _merge_out(outs, lses, dil, b_w_out, 0, x2, gt1_1, seq)

    h, route = _router(x2, g_ffn[1], sc2_1, sh2_1, moe_router[0], moe_router_b[0], seq)
    share = (2 * m) // N_EXPERTS
    tm = -(-(share * 17 // 32) // 64) * 64
    n_tiles = (2 * m) // tm + N_EXPERTS
    src, n_active, te, tv, p1, p2 = _routing_tables(route, tm, n_tiles)
    hs = _gather_rows(h, src, n_active, tm // 4)
    ys = _moe_experts(hs, te, tv, moe_w1.reshape(moe_w1.shape[1:]), moe_w3.reshape(moe_w3.shape[1:]),
                      moe_w2.reshape(moe_w2.shape[1:]), tm)
    out = _combine(ys, p1, p2, x2, gt2_1, route, seq)
    return out.reshape(batch, seq, d)
```
